```python
import jax, jax.numpy as jnp
from jax import lax
import numpy as np

D_MODEL = 2048
BATCH = 2
SEQ = 4096
DEPTH = 1

CHUNK = 64
RET_WIDTH = D_MODEL // 2
ATT_WIDTH = D_MODEL - RET_WIDTH
RET_HEADS = 8
ATT_HEADS = 8
RET_HEAD_DIM = RET_WIDTH // RET_HEADS
ATT_HEAD_DIM = ATT_WIDTH // ATT_HEADS
MIX_WIDTH = RET_WIDTH + ATT_WIDTH
IN_WIDTH = 4 * RET_WIDTH + 3 * ATT_WIDTH
LEFT_CHUNKS = 8
BAND = (LEFT_CHUNKS + 1) * CHUNK
REL_CLIP = 128
REL_SIZE = (CHUNK - 1) + REL_CLIP + 1
D_FF = 4 * D_MODEL
ROPE_BASE = 10000.0
EPS = 1e-6
GN_EPS = 1e-5

kernel_name = "hymba_retention_chunkattn_sqrelu_block"


def rms_norm(x, g):
    xf = x.astype(jnp.float32)
    y = xf * lax.rsqrt(jnp.mean(xf * xf, axis=-1, keepdims=True) + EPS)
    return (y * g.astype(jnp.float32)).astype(x.dtype)


def rotary(x, pos):
    half = x.shape[-1] // 2
    inv_freq = ROPE_BASE ** (-jnp.arange(half, dtype=jnp.float32) / half)
    ang = pos[:, None] * inv_freq[None, :]
    cos = jnp.cos(ang)[None, :, None, :]
    sin = jnp.sin(ang)[None, :, None, :]
    xf = x.astype(jnp.float32)
    x1, x2 = xf[..., :half], xf[..., half:]
    out = jnp.concatenate([x1 * cos - x2 * sin, x1 * sin + x2 * cos], axis=-1)
    return out.astype(x.dtype)


def retention(q, k, v):
    b, s, h, d = q.shape
    nc = s // CHUNK
    f32 = jnp.float32
    q = q.astype(f32).reshape(b, nc, CHUNK, h, d)
    k = k.astype(f32).reshape(b, nc, CHUNK, h, d) * (d ** -0.5)
    v = v.astype(f32).reshape(b, nc, CHUNK, h, d)
    log_g = jnp.log1p(-jnp.exp2(-(5.0 + jnp.arange(h, dtype=f32))))
    pos = jnp.arange(CHUNK, dtype=f32)
    intra_decay = jnp.exp(log_g[:, None, None] * jnp.abs(pos[:, None] - pos[None, :]))
    scores = jnp.einsum('bnihd,bnjhd->bnhij', q, k) * intra_decay
    intra = jnp.einsum('bnhij,bnjhe->bnihe', scores, v)
    k_dec = jnp.exp(log_g[None, :] * (CHUNK - 1.0 - pos)[:, None])
    contrib = jnp.einsum('bnjhd,bnjhe->nbhde', k * k_dec[:, :, None], v)
    chunk_dec = jnp.exp(log_g * CHUNK)[:, None, None]

    def step(state, c):
        return chunk_dec * state + c, state

    _, prev = lax.scan(step, jnp.zeros_like(contrib[0]), contrib)
    q_dec = jnp.exp(log_g[None, :] * (pos + 1.0)[:, None])
    cross = jnp.einsum('bnihd,nbhde->bnihe', q * q_dec[:, :, None], prev)
    return (intra + cross).reshape(b, s, h, d)


def head_group_norm(y, g):
    b, s, h, d = y.shape
    mu = jnp.mean(y, axis=-1, keepdims=True)
    var = jnp.mean(jnp.square(y - mu), axis=-1, keepdims=True)
    yn = (y - mu) * lax.rsqrt(var + GN_EPS) * g.astype(jnp.float32).reshape(h, d)
    return yn.reshape(b, s, h * d)


def chunk_attention(q, k, v, rel_bias):
    b, s, h, d = q.shape
    nc = s // CHUNK
    q = q.reshape(b, nc, CHUNK, h, d)
    pad = ((0, 0), (LEFT_CHUNKS, 0), (0, 0), (0, 0), (0, 0))
    kp = jnp.pad(k.reshape(b, nc, CHUNK, h, d), pad)
    vp = jnp.pad(v.reshape(b, nc, CHUNK, h, d), pad)
    band_idx = jnp.arange(nc)[:, None] + jnp.arange(LEFT_CHUNKS + 1)[None, :]
    kb = jnp.take(kp, band_idx, axis=1).reshape(b, nc, BAND, h, d)
    vb = jnp.take(vp, band_idx, axis=1).reshape(b, nc, BAND, h, d)
    scores = jnp.einsum('bnihd,bnjhd->bnhij', q, kb,
                        preferred_element_type=jnp.float32) * (d ** -0.5)
    qi = jnp.arange(CHUNK)[:, None]
    kj = jnp.arange(BAND)[None, :]
    rel = jnp.clip(qi + LEFT_CHUNKS * CHUNK - kj, -(CHUNK - 1), REL_CLIP) + (CHUNK - 1)
    bias = rel_bias.astype(jnp.float32)[:, rel]
    valid = (jnp.arange(nc)[:, None] - LEFT_CHUNKS + kj // CHUNK) >= 0
    scores = jnp.where(valid[None, :, None, None, :], scores + bias[None, None], -1e30)
    probs = jax.nn.softmax(scores, axis=-1).astype(v.dtype)
    out = jnp.einsum('bnhij,bnjhd->bnihd', probs, vb)
    return out.reshape(b, s, h * d)


def setup_inputs(seed: int = 0) -> dict:
    key = jax.random.key(seed)
    ks = jax.random.split(key, 12)
    f32 = jnp.float32
    nrm = lambda k, shape, scale: jax.random.normal(k, shape, f32) * scale
    return {
        "x": jax.random.normal(ks[0], (BATCH, SEQ, D_MODEL), f32),
        "norm1_g": 1.0 + nrm(ks[1], (DEPTH, D_MODEL), 0.05),
        "w_in": nrm(ks[2], (DEPTH, D_MODEL, IN_WIDTH), D_MODEL ** -0.5),
        "ret_norm_g": 1.0 + nrm(ks[3], (DEPTH, RET_WIDTH), 0.05),
        "q_norm_g": 1.0 + nrm(ks[4], (DEPTH, ATT_HEAD_DIM), 0.05),
        "k_norm_g": 1.0 + nrm(ks[5], (DEPTH, ATT_HEAD_DIM), 0.05),
        "rel_bias": nrm(ks[6], (DEPTH, ATT_HEADS, REL_SIZE), 0.2),
        "w_out": nrm(ks[7], (DEPTH, MIX_WIDTH, D_MODEL), MIX_WIDTH ** -0.5),
        "norm2_g": 1.0 + nrm(ks[8], (DEPTH, D_MODEL), 0.05),
        "w_ff1": nrm(ks[9], (DEPTH, D_MODEL, D_FF), D_MODEL ** -0.5),
        "w_ff2": nrm(ks[10], (DEPTH, D_FF, D_MODEL), D_FF ** -0.5),
    }


def reference(x, norm1_g, w_in, ret_norm_g, q_norm_g, k_norm_g, rel_bias,
              w_out, norm2_g, w_ff1, w_ff2):
    b, s, _ = x.shape
    pos = jnp.arange(s, dtype=jnp.float32)
    split_at = [RET_WIDTH, 2 * RET_WIDTH, 3 * RET_WIDTH, 4 * RET_WIDTH,
                4 * RET_WIDTH + ATT_WIDTH, 4 * RET_WIDTH + 2 * ATT_WIDTH]
    for layer in range(DEPTH):
        h = rms_norm(x, norm1_g[layer])
        proj = h @ w_in[layer]
        rq, rk, rv, rg, aq, ak, av = jnp.split(proj, split_at, axis=-1)
        rq = rotary(rq.reshape(b, s, RET_HEADS, RET_HEAD_DIM), pos)
        rk = rotary(rk.reshape(b, s, RET_HEADS, RET_HEAD_DIM), pos)
        rv = rv.reshape(b, s, RET_HEADS, RET_HEAD_DIM)
        ret = head_group_norm(retention(rq, rk, rv), ret_norm_g[layer])
        ret = (jax.nn.silu(rg.astype(jnp.float32)) * ret).astype(x.dtype)
        aq = rms_norm(aq.reshape(b, s, ATT_HEADS, ATT_HEAD_DIM), q_norm_g[layer])
        ak = rms_norm(ak.reshape(b, s, ATT_HEADS, ATT_HEAD_DIM), k_norm_g[layer])
        av = av.reshape(b, s, ATT_HEADS, ATT_HEAD_DIM)
        att = chunk_attention(aq, ak, av, rel_bias[layer])
        x = x + jnp.concatenate([ret, att], axis=-1) @ w_out[layer]
        h = rms_norm(x, norm2_g[layer])
        x = x + jnp.square(jax.nn.relu(h @ w_ff1[layer])) @ w_ff2[layer]
    return x
```

```python
import functools
import math

import numpy as np
import jax
import jax.numpy as jnp
from jax import lax
from jax.experimental import pallas as pl
from jax.experimental.pallas import tpu as pltpu

CHUNK = 64
LEFT_CHUNKS = 8
REL_CLIP = 128
HEAD_DIM = 128
RET_HEADS = 8
ATT_HEADS = 8
ROPE_BASE = 10000.0
EPS = 1e-6
GN_EPS = 1e-5
NEG = -1e30

V7X_VMEM_LIMIT_BYTES = 56 * 1024 * 1024

SEQ_BLOCK = 256
LEFT_BLOCKS = LEFT_CHUNKS * CHUNK // SEQ_BLOCK
WINDOW = (LEFT_BLOCKS + 1) * SEQ_BLOCK
TOEPLITZ_WIDTH = 1024

F32 = jnp.float32
BF16 = jnp.bfloat16


def _params(n_axes, vmem_bytes=V7X_VMEM_LIMIT_BYTES):
    return pltpu.CompilerParams(
        dimension_semantics=("arbitrary",) * n_axes, vmem_limit_bytes=vmem_bytes)


def _dot(a, b):
    return jnp.dot(a, b, preferred_element_type=F32)


def _dot_nt(a, b):
    return lax.dot_general(a, b, (((1,), (1,)), ((), ())), preferred_element_type=F32)


def _dot_tn(a, b):
    return lax.dot_general(a, b, (((0,), (0,)), ((), ())), preferred_element_type=F32)


def _in_proj_kernel(x_ref, g_ref, w_ref, cos_ref, sin_ref, qg_ref, kg_ref, o_ref, h_ref,
                    *, n_heads_per_block):
    j = pl.program_id(1)

    @pl.when(j == 0)
    def _():
        x = x_ref[...]
        ms = jnp.mean(x * x, axis=-1, keepdims=True)
        h_ref[...] = (x * lax.rsqrt(ms + EPS) * g_ref[...]).astype(BF16)

    acc = _dot(h_ref[...], w_ref[...])
    heads = [slice(h * HEAD_DIM, (h + 1) * HEAD_DIM) for h in range(n_heads_per_block)]
    scale = HEAD_DIM ** -0.5

    def rotary(post_scale):
        cos = cos_ref[...]
        sin = sin_ref[...]
        for sl in heads:
            a = acc[:, sl]
            r = a * cos + pltpu.roll(a, HEAD_DIM // 2, 1) * sin
            if post_scale is not None:
                r = r * post_scale
            o_ref[:, sl] = r.astype(BF16)

    def head_rms(gain_ref, post_scale):
        gain = gain_ref[...]
        for sl in heads:
            a = acc[:, sl]
            ms = jnp.mean(a * a, axis=-1, keepdims=True)
            r = a * lax.rsqrt(ms + EPS) * gain
            if post_scale is not None:
                r = r * post_scale
            o_ref[:, sl] = r.astype(BF16)

    pl.when(j == 0)(lambda: rotary(None))
    pl.when(j == 1)(lambda: rotary(scale))
    pl.when(j == 4)(lambda: head_rms(qg_ref, scale))
    pl.when(j == 5)(lambda: head_rms(kg_ref, None))

    @pl.when((j == 2) | (j == 3) | (j == 6))
    def _():
        o_ref[...] = acc.astype(BF16)


def _in_proj(x2d, g1, w_bf16, cos2, sin2, qg, kg, *, seq, tm):
    m, d = x2d.shape
    n = w_bf16.shape[1]
    tn = n // 7
    assert tn % HEAD_DIM == 0 and m % tm == 0 and seq % tm == 0
    blocks_per_seq = seq // tm
    return pl.pallas_call(
        functools.partial(_in_proj_kernel, n_heads_per_block=tn // HEAD_DIM),
        out_shape=jax.ShapeDtypeStruct((m, n), BF16),
        grid=(m // tm, 7),
        in_specs=[
            pl.BlockSpec((tm, d), lambda i, j: (i, 0)),
            pl.BlockSpec((1, d), lambda i, j: (0, 0)),
            pl.BlockSpec((d, tn), lambda i, j: (0, j)),
            pl.BlockSpec((tm, HEAD_DIM), lambda i, j: (i % blocks_per_seq, 0)),
            pl.BlockSpec((tm, HEAD_DIM), lambda i, j: (i % blocks_per_seq, 0)),
            pl.BlockSpec((1, HEAD_DIM), lambda i, j: (0, 0)),
            pl.BlockSpec((1, HEAD_DIM), lambda i, j: (0, 0)),
        ],
        out_specs=pl.BlockSpec((tm, tn), lambda i, j: (i, j)),
        scratch_shapes=[pltpu.VMEM((tm, d), BF16)],
        compiler_params=_params(2),
        name="in_proj",
    )(x2d, g1, w_bf16, cos2, sin2, qg, kg)


def _bias_tile_kernel(g_ref, o_ref):
    row = jnp.broadcast_to(g_ref[0], (SEQ_BLOCK, TOEPLITZ_WIDTH))
    t = pltpu.roll(row, 0, 1, stride=1, stride_axis=0)[:, :WINDOW]
    qc = lax.broadcasted_iota(jnp.int32, (SEQ_BLOCK, WINDOW), 0) // CHUNK
    kc = lax.broadcasted_iota(jnp.int32, (SEQ_BLOCK, WINDOW), 1) // CHUNK
    visible = (kc >= qc) & (kc <= qc + LEFT_CHUNKS)
    o_ref[0] = jnp.where(visible, t, NEG)


def _bias_tile(rel_bias):
    h, rel_size = rel_bias.shape
    assert rel_size == CHUNK + REL_CLIP
    far = rel_bias[:, rel_size - 1:]
    near = rel_bias[:, :1]
    left = LEFT_CHUNKS * CHUNK - REL_CLIP
    g = jnp.concatenate([
        jnp.broadcast_to(far, (h, left)),
        jnp.flip(rel_bias, axis=-1),
        jnp.broadcast_to(near, (h, WINDOW - left - rel_size)),
        jnp.broadcast_to(far, (h, TOEPLITZ_WIDTH - WINDOW)),
    ], axis=-1).astype(F32).reshape(h, 1, TOEPLITZ_WIDTH)
    return pl.pallas_call(
        _bias_tile_kernel,
        out_shape=jax.ShapeDtypeStruct((h, SEQ_BLOCK, WINDOW), F32),
        grid=(h,),
        in_specs=[pl.BlockSpec((1, 1, TOEPLITZ_WIDTH), lambda i: (i, 0, 0))],
        out_specs=pl.BlockSpec((1, SEQ_BLOCK, WINDOW), lambda i: (i, 0, 0)),
        compiler_params=_params(1),
        name="rel_bias_tile",
    )(g)


def _retention_tables():
    h = np.arange(RET_HEADS, dtype=np.float64)
    log_g = np.log1p(-np.exp2(-(5.0 + h)))
    pos = np.arange(SEQ_BLOCK, dtype=np.float64)
    dist = np.abs(pos[:, None] - pos[None, :])
    vis = (pos[None, :] // CHUNK) <= (pos[:, None] // CHUNK)
    decay = np.exp(log_g[:, None, None] * dist) * vis
    q_dec = np.exp(log_g[None, :] * (pos + 1.0)[:, None])
    k_dec = np.exp(log_g[None, :] * (SEQ_BLOCK - 1.0 - pos)[:, None])
    blk_dec = np.exp(log_g * SEQ_BLOCK)
    lanes = lambda t: np.repeat(t, HEAD_DIM, axis=1).astype(np.float32)
    return decay.astype(np.float32), lanes(q_dec), lanes(k_dec), [float(v) for v in blk_dec]


def _retention_kernel(q_ref, k_ref, v_ref, gate_ref, dec_ref, qdec_ref, kdec_ref, gn_ref,
                      o_ref, st_ref, *, blocks_per_seq, blk_dec):
    @pl.when(pl.program_id(0) % blocks_per_seq == 0)
    def _():
        st_ref[...] = jnp.zeros_like(st_ref)

    for h in range(RET_HEADS):
        sl = slice(h * HEAD_DIM, (h + 1) * HEAD_DIM)
        q = q_ref[:, sl]
        k = k_ref[:, sl]
        v = v_ref[:, sl]
        p = (_dot_nt(q, k) * dec_ref[h]).astype(BF16)
        qd = (q.astype(F32) * qdec_ref[:, sl]).astype(BF16)
        state = st_ref[h]
        y = _dot(p, v) + _dot(qd, state.astype(BF16))
        kd = (k.astype(F32) * kdec_ref[:, sl]).astype(BF16)
        st_ref[h] = blk_dec[h] * state + _dot_tn(kd, v)
        mu = jnp.mean(y, axis=-1, keepdims=True)
        yc = y - mu
        var = jnp.mean(yc * yc, axis=-1, keepdims=True)
        yn = yc * lax.rsqrt(var + GN_EPS) * gn_ref[:, sl]
        gate = gate_ref[:, sl].astype(F32)
        swish = gate / (1.0 + jnp.exp(-gate))
        o_ref[:, sl] = (swish * yn).astype(BF16)


def _retention(proj, gn, *, seq):
    m = proj.shape[0]
    width = RET_HEADS * HEAD_DIM
    decay, q_dec, k_dec, blk_dec = _retention_tables()
    blocks_per_seq = seq // SEQ_BLOCK
    section = lambda s: pl.BlockSpec((SEQ_BLOCK, width), lambda i: (i, s))
    whole = lambda shape: pl.BlockSpec(shape, lambda i: (0,) * len(shape))
    return pl.pallas_call(
        functools.partial(_retention_kernel, blocks_per_seq=blocks_per_seq, blk_dec=blk_dec),
        out_shape=jax.ShapeDtypeStruct((m, width), BF16),
        grid=(m // SEQ_BLOCK,),
        in_specs=[section(0), section(1), section(2), section(3),
                  whole(decay.shape), whole(q_dec.shape), whole(k_dec.shape), whole((1, width))],
        out_specs=pl.BlockSpec((SEQ_BLOCK, width), lambda i: (i, 0)),
        scratch_shapes=[pltpu.VMEM((RET_HEADS, HEAD_DIM, HEAD_DIM), F32)],
        compiler_params=_params(1),
        name="retention",
    )(proj, proj, proj, proj, decay, q_dec, k_dec, gn)


def _chunk_attn_kernel(q_ref, k0_ref, k1_ref, k2_ref, v0_ref, v1_ref, v2_ref, bias_ref, o_ref,
                       *, blocks_per_seq):
    t = pl.program_id(0) % blocks_per_seq
    k_refs = (k0_ref, k1_ref, k2_ref)
    v_refs = (v0_ref, v1_ref, v2_ref)
    pen = [jnp.where(t >= LEFT_BLOCKS - b, 0.0, NEG).astype(F32) for b in range(LEFT_BLOCKS)] + [None]

    for h in range(ATT_HEADS):
        sl = slice(h * HEAD_DIM, (h + 1) * HEAD_DIM)
        q = q_ref[:, sl]
        s = []
        for b in range(LEFT_BLOCKS + 1):
            sb = _dot_nt(q, k_refs[b][:, sl]) + bias_ref[h, :, b * SEQ_BLOCK:(b + 1) * SEQ_BLOCK]
            if pen[b] is not None:
                sb = sb + pen[b]
            s.append(sb)
        mx = functools.reduce(jnp.maximum, [jnp.max(sb, axis=-1, keepdims=True) for sb in s])
        p = [jnp.exp(sb - mx) for sb in s]
        denom = functools.reduce(lambda a, b: a + b, [jnp.sum(pb, axis=-1, keepdims=True) for pb in p])
        out = functools.reduce(
            lambda a, b: a + b,
            [_dot(pb.astype(BF16), v_refs[b][:, sl]) for b, pb in enumerate(p)])
        o_ref[:, sl] = (out / denom).astype(BF16)


def _chunk_attn(proj, bias_tile, *, seq):
    m = proj.shape[0]
    width = ATT_HEADS * HEAD_DIM
    blocks_per_seq = seq // SEQ_BLOCK

    def window(section, b):
        def index(i):
            t = i % blocks_per_seq
            return (i - t + jnp.maximum(t - (LEFT_BLOCKS - b), 0), section)
        return pl.BlockSpec((SEQ_BLOCK, width), index)

    return pl.pallas_call(
        functools.partial(_chunk_attn_kernel, blocks_per_seq=blocks_per_seq),
        out_shape=jax.ShapeDtypeStruct((m, width), BF16),
        grid=(m // SEQ_BLOCK,),
        in_specs=[pl.BlockSpec((SEQ_BLOCK, width), lambda i: (i, 4))]
                 + [window(5, b) for b in range(LEFT_BLOCKS + 1)]
                 + [window(6, b) for b in range(LEFT_BLOCKS + 1)]
                 + [pl.BlockSpec(bias_tile.shape, lambda i: (0, 0, 0))],
        out_specs=pl.BlockSpec((SEQ_BLOCK, width), lambda i: (i, 0)),
        compiler_params=_params(1),
        name="chunk_attn",
    )(proj, proj, proj, proj, proj, proj, proj, bias_tile)


def _out_proj_kernel(ret_ref, att_ref, wr_ref, wa_ref, x_ref, o_ref):
    o_ref[...] = x_ref[...] + (_dot(ret_ref[...], wr_ref[...]) + _dot(att_ref[...], wa_ref[...]))


def _out_proj(ret, att, w_bf16, x2d, *, tm, tn):
    m, d = x2d.shape
    kr, ka = ret.shape[1], att.shape[1]
    assert kr == ka and kr + ka == w_bf16.shape[0] and m % tm == 0 and d % tn == 0
    return pl.pallas_call(
        _out_proj_kernel,
        out_shape=jax.ShapeDtypeStruct((m, d), F32),
        grid=(m // tm, d // tn),
        in_specs=[
            pl.BlockSpec((tm, kr), lambda i, j: (i, 0)),
            pl.BlockSpec((tm, ka), lambda i, j: (i, 0)),
            pl.BlockSpec((kr, tn), lambda i, j: (0, j)),
            pl.BlockSpec((ka, tn), lambda i, j: (1, j)),
            pl.BlockSpec((tm, tn), lambda i, j: (i, j)),
        ],
        out_specs=pl.BlockSpec((tm, tn), lambda i, j: (i, j)),
        compiler_params=_params(2),
        name="out_proj",
    )(ret, att, w_bf16, w_bf16, x2d)


def _ffn_kernel(x_ref, g_ref, w1_ref, w2_ref, o_ref, h_ref):
    f = pl.program_id(1)

    @pl.when(f == 0)
    def _():
        x = x_ref[...]
        ms = jnp.mean(x * x, axis=-1, keepdims=True)
        h_ref[...] = (x * lax.rsqrt(ms + EPS) * g_ref[...]).astype(BF16)
        o_ref[...] = x

    a = jnp.maximum(_dot(h_ref[...], w1_ref[...]), 0.0)
    o_ref[...] += _dot((a * a).astype(BF16), w2_ref[...])


def _ffn(x2d, g2, w1_bf16, w2_bf16, *, tm, tf):
    m, d = x2d.shape
    d_ff = w1_bf16.shape[1]
    assert m % tm == 0 and d_ff % tf == 0
    return pl.pallas_call(
        _ffn_kernel,
        out_shape=jax.ShapeDtypeStruct((m, d), F32),
        grid=(m // tm, d_ff // tf),
        in_specs=[
            pl.BlockSpec((tm, d), lambda i, f: (i, 0)),
            pl.BlockSpec((1, d), lambda i, f: (0, 0)),
            pl.BlockSpec((d, tf), lambda i, f: (0, f)),
            pl.BlockSpec((tf, d), lambda i, f: (f, 0)),
        ],
        out_specs=pl.BlockSpec((tm, d), lambda i, f: (i, 0)),
        scratch_shapes=[pltpu.VMEM((tm, d), BF16)],
        compiler_params=_params(2),
        name="ffn",
    )(x2d, g2, w1_bf16, w2_bf16)


def _rotary_tables(seq):
    half = HEAD_DIM // 2
    inv_freq = ROPE_BASE ** (-np.arange(half, dtype=np.float64) / half)
    ang = np.arange(seq, dtype=np.float64)[:, None] * inv_freq[None, :]
    cos, sin = np.cos(ang), np.sin(ang)
    cos2 = np.concatenate([cos, cos], axis=1).astype(np.float32)
    sin2 = np.concatenate([-sin, sin], axis=1).astype(np.float32)
    return cos2, sin2


def kernel(x, norm1_g, w_in, ret_norm_g, q_norm_g, k_norm_g, rel_bias, w_out, norm2_g, w_ff1, w_ff2):
    b, s, d = x.shape
    depth = w_in.shape[0]
    assert s % SEQ_BLOCK == 0 and SEQ_BLOCK % CHUNK == 0
    assert w_in.shape[2] == 7 * RET_HEADS * HEAD_DIM and RET_HEADS == ATT_HEADS
    cos2, sin2 = _rotary_tables(s)
    row = lambda v: v.astype(F32).reshape(1, -1)
    x2d = x.reshape(b * s, d)
    for layer in range(depth):
        proj = _in_proj(x2d, row(norm1_g[layer]), w_in[layer].astype(BF16), cos2, sin2,
                        row(q_norm_g[layer]), row(k_norm_g[layer]), seq=s, tm=1024)
        ret = _retention(proj, row(ret_norm_g[layer]), seq=s)
        att = _chunk_attn(proj, _bias_tile(rel_bias[layer]), seq=s)
        x2d = _out_proj(ret, att, w_out[layer].astype(BF16), x2d, tm=1024, tn=1024)
        x2d = _ffn(x2d, row(norm2_g[layer]), w_ff1[layer].astype(BF16), w_ff2[layer].astype(BF16),
                   tm=512, tf=1024)
    return x2d.reshape(b, s, d)
```

```python
import functools
import math

import numpy as np
import jax
import jax.numpy as jnp
from jax import lax
from jax.experimental import pallas as pl
from jax.experimental.pallas import tpu as pltpu

CHUNK = 64
LEFT_CHUNKS = 8
REL_CLIP = 128
HEAD_DIM = 128
RET_HEADS = 8
ATT_HEADS = 8
ROPE_BASE = 10000.0
EPS = 1e-6
GN_EPS = 1e-5
NEG = -1e30

V7X_VMEM_LIMIT_BYTES = 56 * 1024 * 1024

SEQ_BLOCK = 256
LEFT_BLOCKS = LEFT_CHUNKS * CHUNK // SEQ_BLOCK
WINDOW = (LEFT_BLOCKS + 1) * SEQ_BLOCK
TOEPLITZ_WIDTH = 1024

F32 = jnp.float32
BF16 = jnp.bfloat16


def _params(n_axes, vmem_bytes=V7X_VMEM_LIMIT_BYTES):
    return pltpu.CompilerParams(
        dimension_semantics=("arbitrary",) * n_axes, vmem_limit_bytes=vmem_bytes)


def _dot(a, b):
    return jnp.dot(a, b, preferred_element_type=F32)


def _dot_nt(a, b):
    return lax.dot_general(a, b, (((1,), (1,)), ((), ())), preferred_element_type=F32)


def _dot_tn(a, b):
    return lax.dot_general(a, b, (((0,), (0,)), ((), ())), preferred_element_type=F32)


def _in_proj_kernel(x_ref, g_ref, w_ref, cos_ref, sin_ref, qg_ref, kg_ref, o_ref, h_ref, r_ref,
                    *, n_heads_per_block):
    j = pl.program_id(1)
    heads = [slice(h * HEAD_DIM, (h + 1) * HEAD_DIM) for h in range(n_heads_per_block)]
    scale = HEAD_DIM ** -0.5

    def project():
        return _dot(h_ref[...], w_ref[...])

    def rotary(acc, post_scale):
        cos = cos_ref[...]
        sin = sin_ref[...]
        for sl in heads:
            a = acc[:, sl] * r_ref[...]
            y = a * cos + pltpu.roll(a, HEAD_DIM // 2, 1) * sin
            if post_scale is not None:
                y = y * post_scale
            o_ref[:, sl] = y.astype(BF16)

    def head_rms(acc, gain_ref, post_scale):
        gain = gain_ref[...]
        for sl in heads:
            a = acc[:, sl] * r_ref[...]
            ms = jnp.mean(a * a, axis=-1, keepdims=True)
            y = a * lax.rsqrt(ms + EPS) * gain
            if post_scale is not None:
                y = y * post_scale
            o_ref[:, sl] = y.astype(BF16)

    @pl.when(j == 0)
    def _():
        x = x_ref[...]
        ms = jnp.mean(x * x, axis=-1, keepdims=True)
        r_ref[...] = jnp.broadcast_to(lax.rsqrt(ms + EPS), r_ref.shape)
        h_ref[...] = (x * g_ref[...]).astype(BF16)
        rotary(project(), None)

    pl.when(j == 1)(lambda: rotary(project(), scale))
    pl.when(j == 4)(lambda: head_rms(project(), qg_ref, scale))
    pl.when(j == 5)(lambda: head_rms(project(), kg_ref, None))

    @pl.when((j == 2) | (j == 3) | (j == 6))
    def _():
        acc = project()
        for sl in heads:
            o_ref[:, sl] = (acc[:, sl] * r_ref[...]).astype(BF16)


def _in_proj(x2d, g1, w_bf16, cos2, sin2, qg, kg, *, seq, tm):
    m, d = x2d.shape
    n = w_bf16.shape[1]
    tn = n // 7
    assert tn % HEAD_DIM == 0 and m % tm == 0 and seq % tm == 0
    blocks_per_seq = seq // tm
    return pl.pallas_call(
        functools.partial(_in_proj_kernel, n_heads_per_block=tn // HEAD_DIM),
        out_shape=jax.ShapeDtypeStruct((m, n), BF16),
        grid=(m // tm, 7),
        in_specs=[
            pl.BlockSpec((tm, d), lambda i, j: (i, 0)),
            pl.BlockSpec((1, d), lambda i, j: (0, 0)),
            pl.BlockSpec((d, tn), lambda i, j: (0, j)),
            pl.BlockSpec((tm, HEAD_DIM), lambda i, j: (i % blocks_per_seq, 0)),
            pl.BlockSpec((tm, HEAD_DIM), lambda i, j: (i % blocks_per_seq, 0)),
            pl.BlockSpec((1, HEAD_DIM), lambda i, j: (0, 0)),
            pl.BlockSpec((1, HEAD_DIM), lambda i, j: (0, 0)),
        ],
        out_specs=pl.BlockSpec((tm, tn), lambda i, j: (i, j)),
        scratch_shapes=[pltpu.VMEM((tm, d), BF16), pltpu.VMEM((tm, HEAD_DIM), F32)],
        compiler_params=_params(2),
        name="in_proj",
    )(x2d, g1, w_bf16, cos2, sin2, qg, kg)


def _bias_tile_kernel(g_ref, o_ref):
    row = jnp.broadcast_to(g_ref[0], (SEQ_BLOCK, TOEPLITZ_WIDTH))
    t = pltpu.roll(row, 0, 1, stride=1, stride_axis=0)[:, :WINDOW]
    qc = lax.broadcasted_iota(jnp.int32, (SEQ_BLOCK, WINDOW), 0) // CHUNK
    kc = lax.broadcasted_iota(jnp.int32, (SEQ_BLOCK, WINDOW), 1) // CHUNK
    visible = (kc >= qc) & (kc <= qc + LEFT_CHUNKS)
    o_ref[0] = jnp.where(visible, t, NEG)


def _bias_tile(rel_bias):
    h, rel_size = rel_bias.shape
    assert rel_size == CHUNK + REL_CLIP
    far = rel_bias[:, rel_size - 1:]
    near = rel_bias[:, :1]
    left = LEFT_CHUNKS * CHUNK - REL_CLIP
    g = jnp.concatenate([
        jnp.broadcast_to(far, (h, left)),
        jnp.flip(rel_bias, axis=-1),
        jnp.broadcast_to(near, (h, WINDOW - left - rel_size)),
        jnp.broadcast_to(far, (h, TOEPLITZ_WIDTH - WINDOW)),
    ], axis=-1).astype(F32).reshape(h, 1, TOEPLITZ_WIDTH)
    return pl.pallas_call(
        _bias_tile_kernel,
        out_shape=jax.ShapeDtypeStruct((h, SEQ_BLOCK, WINDOW), F32),
        grid=(h,),
        in_specs=[pl.BlockSpec((1, 1, TOEPLITZ_WIDTH), lambda i: (i, 0, 0))],
        out_specs=pl.BlockSpec((1, SEQ_BLOCK, WINDOW), lambda i: (i, 0, 0)),
        compiler_params=_params(1),
        name="rel_bias_tile",
    )(g)


def _retention_tables():
    h = np.arange(RET_HEADS, dtype=np.float64)
    log_g = np.log1p(-np.exp2(-(5.0 + h)))
    pos = np.arange(SEQ_BLOCK, dtype=np.float64)
    dist = np.abs(pos[:, None] - pos[None, :])
    vis = (pos[None, :] // CHUNK) <= (pos[:, None] // CHUNK)
    decay = np.exp(log_g[:, None, None] * dist) * vis
    q_dec = np.exp(log_g[None, :] * (pos + 1.0)[:, None])
    k_dec = np.exp(log_g[None, :] * (SEQ_BLOCK - 1.0 - pos)[:, None])
    blk_dec = np.exp(log_g * SEQ_BLOCK)
    lanes = lambda t: np.repeat(t, HEAD_DIM, axis=1).astype(np.float32)
    return decay.astype(np.float32), lanes(q_dec), lanes(k_dec), [float(v) for v in blk_dec]


def _retention_kernel(q_ref, k_ref, v_ref, gate_ref, dec_ref, qdec_ref, kdec_ref, gn_ref,
                      o_ref, st_ref, *, blocks_per_seq, blk_dec):
    @pl.when(pl.program_id(0) % blocks_per_seq == 0)
    def _():
        st_ref[...] = jnp.zeros_like(st_ref)

    for h in range(RET_HEADS):
        sl = slice(h * HEAD_DIM, (h + 1) * HEAD_DIM)
        q = q_ref[:, sl]
        k = k_ref[:, sl]
        v = v_ref[:, sl]
        p = (_dot_nt(q, k) * dec_ref[h]).astype(BF16)
        qd = (q.astype(F32) * qdec_ref[:, sl]).astype(BF16)
        state = st_ref[h]
        y = _dot(p, v) + _dot(qd, state.astype(BF16))
        kd = (k.astype(F32) * kdec_ref[:, sl]).astype(BF16)
        st_ref[h] = blk_dec[h] * state + _dot_tn(kd, v)
        mu = jnp.mean(y, axis=-1, keepdims=True)
        yc = y - mu
        var = jnp.mean(yc * yc, axis=-1, keepdims=True)
        yn = yc * lax.rsqrt(var + GN_EPS) * gn_ref[:, sl]
        gate = gate_ref[:, sl].astype(F32)
        swish = gate / (1.0 + jnp.exp(-gate))
        o_ref[:, sl] = (swish * yn).astype(BF16)


def _retention(proj, gn, *, seq):
    m = proj.shape[0]
    width = RET_HEADS * HEAD_DIM
    decay, q_dec, k_dec, blk_dec = _retention_tables()
    blocks_per_seq = seq // SEQ_BLOCK
    section = lambda s: pl.BlockSpec((SEQ_BLOCK, width), lambda i: (i, s))
    whole = lambda shape: pl.BlockSpec(shape, lambda i: (0,) * len(shape))
    return pl.pallas_call(
        functools.partial(_retention_kernel, blocks_per_seq=blocks_per_seq, blk_dec=blk_dec),
        out_shape=jax.ShapeDtypeStruct((m, width), BF16),
        grid=(m // SEQ_BLOCK,),
        in_specs=[section(0), section(1), section(2), section(3),
                  whole(decay.shape), whole(q_dec.shape), whole(k_dec.shape), whole((1, width))],
        out_specs=pl.BlockSpec((SEQ_BLOCK, width), lambda i: (i, 0)),
        scratch_shapes=[pltpu.VMEM((RET_HEADS, HEAD_DIM, HEAD_DIM), F32)],
        compiler_params=_params(1),
        name="retention",
    )(proj, proj, proj, proj, decay, q_dec, k_dec, gn)


def _chunk_attn_kernel(q_ref, k0_ref, k1_ref, k2_ref, v0_ref, v1_ref, v2_ref, bias_ref, o_ref,
                       *, blocks_per_seq):
    t = pl.program_id(0) % blocks_per_seq
    k_refs = (k0_ref, k1_ref, k2_ref)
    v_refs = (v0_ref, v1_ref, v2_ref)
    pen = [jnp.where(t >= LEFT_BLOCKS - b, 0.0, NEG).astype(F32) for b in range(LEFT_BLOCKS)] + [None]

    for h in range(ATT_HEADS):
        sl = slice(h * HEAD_DIM, (h + 1) * HEAD_DIM)
        q = q_ref[:, sl]
        s = []
        for b in range(LEFT_BLOCKS + 1):
            sb = _dot_nt(q, k_refs[b][:, sl]) + bias_ref[h, :, b * SEQ_BLOCK:(b + 1) * SEQ_BLOCK]
            if pen[b] is not None:
                sb = sb + pen[b]
            s.append(sb)
        mx = functools.reduce(jnp.maximum, [jnp.max(sb, axis=-1, keepdims=True) for sb in s])
        p = [jnp.exp(sb - mx) for sb in s]
        denom = functools.reduce(lambda a, b: a + b, [jnp.sum(pb, axis=-1, keepdims=True) for pb in p])
        out = functools.reduce(
            lambda a, b: a + b,
            [_dot(pb.astype(BF16), v_refs[b][:, sl]) for b, pb in enumerate(p)])
        o_ref[:, sl] = (out / denom).astype(BF16)


def _chunk_attn(proj, bias_tile, *, seq):
    m = proj.shape[0]
    width = ATT_HEADS * HEAD_DIM
    blocks_per_seq = seq // SEQ_BLOCK

    def window(section, b):
        def index(i):
            t = i % blocks_per_seq
            return (i - t + jnp.maximum(t - (LEFT_BLOCKS - b), 0), section)
        return pl.BlockSpec((SEQ_BLOCK, width), index)

    return pl.pallas_call(
        functools.partial(_chunk_attn_kernel, blocks_per_seq=blocks_per_seq),
        out_shape=jax.ShapeDtypeStruct((m, width), BF16),
        grid=(m // SEQ_BLOCK,),
        in_specs=[pl.BlockSpec((SEQ_BLOCK, width), lambda i: (i, 4))]
                 + [window(5, b) for b in range(LEFT_BLOCKS + 1)]
                 + [window(6, b) for b in range(LEFT_BLOCKS + 1)]
                 + [pl.BlockSpec(bias_tile.shape, lambda i: (0, 0, 0))],
        out_specs=pl.BlockSpec((SEQ_BLOCK, width), lambda i: (i, 0)),
        compiler_params=_params(1),
        name="chunk_attn",
    )(proj, proj, proj, proj, proj, proj, proj, bias_tile)


def _out_proj_kernel(ret_ref, att_ref, wr_ref, wa_ref, x_ref, o_ref):
    o_ref[...] = x_ref[...] + (_dot(ret_ref[...], wr_ref[...]) + _dot(att_ref[...], wa_ref[...]))


def _out_proj(ret, att, w_bf16, x2d, *, tm, tn):
    m, d = x2d.shape
    kr, ka = ret.shape[1], att.shape[1]
    assert kr == ka and kr + ka == w_bf16.shape[0] and m % tm == 0 and d % tn == 0
    return pl.pallas_call(
        _out_proj_kernel,
        out_shape=jax.ShapeDtypeStruct((m, d), F32),
        grid=(m // tm, d // tn),
        in_specs=[
            pl.BlockSpec((tm, kr), lambda i, j: (i, 0)),
            pl.BlockSpec((tm, ka), lambda i, j: (i, 0)),
            pl.BlockSpec((kr, tn), lambda i, j: (0, j)),
            pl.BlockSpec((ka, tn), lambda i, j: (1, j)),
            pl.BlockSpec((tm, tn), lambda i, j: (i, j)),
        ],
        out_specs=pl.BlockSpec((tm, tn), lambda i, j: (i, j)),
        compiler_params=_params(2),
        name="out_proj",
    )(ret, att, w_bf16, w_bf16, x2d)


def _ffn_kernel(x_ref, g_ref, w1_ref, w2_ref, o_ref, h_ref):
    f = pl.program_id(1)

    @pl.when(f == 0)
    def _():
        x = x_ref[...]
        ms = jnp.mean(x * x, axis=-1, keepdims=True)
        h_ref[...] = (x * lax.rsqrt(ms + EPS) * g_ref[...]).astype(BF16)
        o_ref[...] = x

    a = jnp.maximum(_dot(h_ref[...], w1_ref[...]), 0.0)
    o_ref[...] += _dot((a * a).astype(BF16), w2_ref[...])


def _ffn(x2d, g2, w1_bf16, w2_bf16, *, tm, tf):
    m, d = x2d.shape
    d_ff = w1_bf16.shape[1]
    assert m % tm == 0 and d_ff % tf == 0
    return pl.pallas_call(
        _ffn_kernel,
        out_shape=jax.ShapeDtypeStruct((m, d), F32),
        grid=(m // tm, d_ff // tf),
        in_specs=[
            pl.BlockSpec((tm, d), lambda i, f: (i, 0)),
            pl.BlockSpec((1, d), lambda i, f: (0, 0)),
            pl.BlockSpec((d, tf), lambda i, f: (0, f)),
            pl.BlockSpec((tf, d), lambda i, f: (f, 0)),
        ],
        out_specs=pl.BlockSpec((tm, d), lambda i, f: (i, 0)),
        scratch_shapes=[pltpu.VMEM((tm, d), BF16)],
        compiler_params=_params(2),
        name="ffn",
    )(x2d, g2, w1_bf16, w2_bf16)


def _rotary_tables(seq):
    half = HEAD_DIM // 2
    inv_freq = ROPE_BASE ** (-np.arange(half, dtype=np.float64) / half)
    ang = np.arange(seq, dtype=np.float64)[:, None] * inv_freq[None, :]
    cos, sin = np.cos(ang), np.sin(ang)
    cos2 = np.concatenate([cos, cos], axis=1).astype(np.float32)
    sin2 = np.concatenate([-sin, sin], axis=1).astype(np.float32)
    return cos2, sin2


def kernel(x, norm1_g, w_in, ret_norm_g, q_norm_g, k_norm_g, rel_bias, w_out, norm2_g, w_ff1, w_ff2):
    b, s, d = x.shape
    depth = w_in.shape[0]
    assert s % SEQ_BLOCK == 0 and SEQ_BLOCK % CHUNK == 0
    assert w_in.shape[2] == 7 * RET_HEADS * HEAD_DIM and RET_HEADS == ATT_HEADS
    cos2, sin2 = _rotary_tables(s)
    row = lambda v: v.astype(F32).reshape(1, -1)
    x2d = x.reshape(b * s, d)
    for layer in range(depth):
        proj = _in_proj(x2d, row(norm1_g[layer]), w_in[layer].astype(BF16), cos2, sin2,
                        row(q_norm_g[layer]), row(k_norm_g[layer]), seq=s, tm=1024)
        ret = _retention(proj, row(ret_norm_g[layer]), seq=s)
        att = _chunk_attn(proj, _bias_tile(rel_bias[layer]), seq=s)
        x2d = _out_proj(ret, att, w_out[layer].astype(BF16), x2d, tm=1024, tn=1024)
        x2d = _ffn(x2d, row(norm2_g[layer]), w_ff1[layer].astype(BF16), w_ff2[layer].astype(BF16),
                   tm=512, tf=1024)
    return x2d.reshape(b, s, d)
```

```python
import functools
import math

import numpy as np
import jax
import jax.numpy as jnp
from jax import lax
from jax.experimental import pallas as pl
from jax.experimental.pallas import tpu as pltpu

CHUNK = 64
LEFT_CHUNKS = 8
REL_CLIP = 128
HEAD_DIM = 128
RET_HEADS = 8
ATT_HEADS = 8
ROPE_BASE = 10000.0
EPS = 1e-6
GN_EPS = 1e-5
NEG = -1e30
LOG2E = math.log2(math.e)

V7X_VMEM_LIMIT_BYTES = 56 * 1024 * 1024

SEQ_BLOCK = 256
LEFT_BLOCKS = LEFT_CHUNKS * CHUNK // SEQ_BLOCK
WINDOW = (LEFT_BLOCKS + 1) * SEQ_BLOCK
TOEPLITZ_WIDTH = 1024

F32 = jnp.float32
BF16 = jnp.bfloat16
BF16_SUBLANES = 16


def _params(n_axes, vmem_bytes=V7X_VMEM_LIMIT_BYTES):
    return pltpu.CompilerParams(
        dimension_semantics=("arbitrary",) * n_axes, vmem_limit_bytes=vmem_bytes)


def _dot(a, b):
    return jnp.dot(a, b, preferred_element_type=F32)


def _dot_nt(a, b):
    return lax.dot_general(a, b, (((1,), (1,)), ((), ())), preferred_element_type=F32)


def _dot_tn(a, b):
    return lax.dot_general(a, b, (((0,), (0,)), ((), ())), preferred_element_type=F32)


def _in_proj_kernel(x_ref, g_ref, w_ref, cos_ref, sin_ref, qg_ref, kg_ref, o_ref, h_ref, r_ref,
                    *, n_heads_per_block):
    j = pl.program_id(1)
    heads = [slice(h * HEAD_DIM, (h + 1) * HEAD_DIM) for h in range(n_heads_per_block)]
    scale = HEAD_DIM ** -0.5

    def project():
        return _dot(h_ref[...], w_ref[...])

    def rotary(acc, post_scale):
        cos = cos_ref[...]
        sin = sin_ref[...]
        for sl in heads:
            a = acc[:, sl] * r_ref[...]
            y = a * cos + pltpu.roll(a, HEAD_DIM // 2, 1) * sin
            if post_scale is not None:
                y = y * post_scale
            o_ref[:, sl] = y.astype(BF16)

    def head_rms(acc, gain_ref, post_scale):
        gain = gain_ref[...]
        for sl in heads:
            a = acc[:, sl] * r_ref[...]
            ms = jnp.mean(a * a, axis=-1, keepdims=True)
            y = a * lax.rsqrt(ms + EPS) * gain
            if post_scale is not None:
                y = y * post_scale
            o_ref[:, sl] = y.astype(BF16)

    @pl.when(j == 0)
    def _():
        x = x_ref[...]
        ms = jnp.mean(x * x, axis=-1, keepdims=True)
        r_ref[...] = jnp.broadcast_to(lax.rsqrt(ms + EPS), r_ref.shape)
        h_ref[...] = (x * g_ref[...]).astype(BF16)
        rotary(project(), None)

    pl.when(j == 1)(lambda: rotary(project(), scale))
    pl.when(j == 4)(lambda: head_rms(project(), qg_ref, scale * LOG2E))
    pl.when(j == 5)(lambda: head_rms(project(), kg_ref, None))

    @pl.when((j == 2) | (j == 3) | (j == 6))
    def _():
        acc = project()
        for sl in heads:
            o_ref[:, sl] = (acc[:, sl] * r_ref[...]).astype(BF16)


def _in_proj(x2d, g1, w_bf16, cos2, sin2, qg, kg, *, seq, tm):
    m, d = x2d.shape
    n = w_bf16.shape[1]
    tn = n // 7
    assert tn % HEAD_DIM == 0 and m % tm == 0 and seq % tm == 0
    blocks_per_seq = seq // tm
    return pl.pallas_call(
        functools.partial(_in_proj_kernel, n_heads_per_block=tn // HEAD_DIM),
        out_shape=jax.ShapeDtypeStruct((m, n), BF16),
        grid=(m // tm, 7),
        in_specs=[
            pl.BlockSpec((tm, d), lambda i, j: (i, 0)),
            pl.BlockSpec((1, d), lambda i, j: (0, 0)),
            pl.BlockSpec((d, tn), lambda i, j: (0, j)),
            pl.BlockSpec((tm, HEAD_DIM), lambda i, j: (i % blocks_per_seq, 0)),
            pl.BlockSpec((tm, HEAD_DIM), lambda i, j: (i % blocks_per_seq, 0)),
            pl.BlockSpec((1, HEAD_DIM), lambda i, j: (0, 0)),
            pl.BlockSpec((1, HEAD_DIM), lambda i, j: (0, 0)),
        ],
        out_specs=pl.BlockSpec((tm, tn), lambda i, j: (i, j)),
        scratch_shapes=[pltpu.VMEM((tm, d), BF16), pltpu.VMEM((tm, HEAD_DIM), F32)],
        compiler_params=_params(2),
        name="in_proj",
    )(x2d, g1, w_bf16, cos2, sin2, qg, kg)


def _bias_tile_kernel(g_ref, o_ref):
    row = jnp.broadcast_to(g_ref[0], (SEQ_BLOCK, TOEPLITZ_WIDTH))
    t = pltpu.roll(row, 0, 1, stride=1, stride_axis=0)[:, :WINDOW]
    qc = lax.broadcasted_iota(jnp.int32, (SEQ_BLOCK, WINDOW), 0) // CHUNK
    kc = lax.broadcasted_iota(jnp.int32, (SEQ_BLOCK, WINDOW), 1) // CHUNK
    visible = (kc >= qc) & (kc <= qc + LEFT_CHUNKS)
    o_ref[0] = jnp.where(visible, t * LOG2E, NEG)


def _bias_tile(rel_bias):
    h, rel_size = rel_bias.shape
    assert rel_size == CHUNK + REL_CLIP
    far = rel_bias[:, rel_size - 1:]
    near = rel_bias[:, :1]
    left = LEFT_CHUNKS * CHUNK - REL_CLIP
    g = jnp.concatenate([
        jnp.broadcast_to(far, (h, left)),
        jnp.flip(rel_bias, axis=-1),
        jnp.broadcast_to(near, (h, WINDOW - left - rel_size)),
        jnp.broadcast_to(far, (h, TOEPLITZ_WIDTH - WINDOW)),
    ], axis=-1).astype(F32).reshape(h, 1, TOEPLITZ_WIDTH)
    return pl.pallas_call(
        _bias_tile_kernel,
        out_shape=jax.ShapeDtypeStruct((h, SEQ_BLOCK, WINDOW), F32),
        grid=(h,),
        in_specs=[pl.BlockSpec((1, 1, TOEPLITZ_WIDTH), lambda i: (i, 0, 0))],
        out_specs=pl.BlockSpec((1, SEQ_BLOCK, WINDOW), lambda i: (i, 0, 0)),
        compiler_params=_params(1),
        name="rel_bias_tile",
    )(g)


def _retention_tables():
    h = np.arange(RET_HEADS, dtype=np.float64)
    log_g = np.log1p(-np.exp2(-(5.0 + h)))
    pos = np.arange(SEQ_BLOCK, dtype=np.float64)
    dist = np.abs(pos[:, None] - pos[None, :])
    vis = (pos[None, :] // CHUNK) <= (pos[:, None] // CHUNK)
    decay = np.exp(log_g[:, None, None] * dist) * vis
    q_dec = np.exp(log_g[None, :] * (pos + 1.0)[:, None])
    k_dec = np.exp(log_g[None, :] * (SEQ_BLOCK - 1.0 - pos)[:, None])
    blk_dec = np.exp(log_g * SEQ_BLOCK)
    lanes = lambda t: np.repeat(t, HEAD_DIM, axis=1).astype(np.float32)
    return decay.astype(np.float32), lanes(q_dec), lanes(k_dec), [float(v) for v in blk_dec]


def _retention_kernel(q_ref, k_ref, v_ref, gate_ref, dec_ref, qdec_ref, kdec_ref, gn_ref,
                      o_ref, st_ref, *, blocks_per_seq, blk_dec):
    @pl.when(pl.program_id(0) % blocks_per_seq == 0)
    def _():
        st_ref[...] = jnp.zeros_like(st_ref)

    for h in range(RET_HEADS):
        sl = slice(h * HEAD_DIM, (h + 1) * HEAD_DIM)
        q = q_ref[:, sl]
        k = k_ref[:, sl]
        v = v_ref[:, sl]
        p = (_dot_nt(q, k) * dec_ref[h]).astype(BF16)
        qd = (q.astype(F32) * qdec_ref[:, sl]).astype(BF16)
        state = st_ref[h]
        y = _dot(p, v) + _dot(qd, state.astype(BF16))
        kd = (k.astype(F32) * kdec_ref[:, sl]).astype(BF16)
        st_ref[h] = blk_dec[h] * state + _dot_tn(kd, v)
        mu = jnp.mean(y, axis=-1, keepdims=True)
        yc = y - mu
        var = jnp.mean(yc * yc, axis=-1, keepdims=True)
        yn = yc * lax.rsqrt(var + GN_EPS) * gn_ref[:, sl]
        gate = gate_ref[:, sl].astype(F32)
        swish = gate / (1.0 + jnp.exp(-gate))
        o_ref[:, sl] = (swish * yn).astype(BF16)


def _retention(proj, gn, *, seq):
    m = proj.shape[0]
    width = RET_HEADS * HEAD_DIM
    decay, q_dec, k_dec, blk_dec = _retention_tables()
    blocks_per_seq = seq // SEQ_BLOCK
    section = lambda s: pl.BlockSpec((SEQ_BLOCK, width), lambda i: (i, s))
    whole = lambda shape: pl.BlockSpec(shape, lambda i: (0,) * len(shape))
    return pl.pallas_call(
        functools.partial(_retention_kernel, blocks_per_seq=blocks_per_seq, blk_dec=blk_dec),
        out_shape=jax.ShapeDtypeStruct((m, width), BF16),
        grid=(m // SEQ_BLOCK,),
        in_specs=[section(0), section(1), section(2), section(3),
                  whole(decay.shape), whole(q_dec.shape), whole(k_dec.shape), whole((1, width))],
        out_specs=pl.BlockSpec((SEQ_BLOCK, width), lambda i: (i, 0)),
        scratch_shapes=[pltpu.VMEM((RET_HEADS, HEAD_DIM, HEAD_DIM), F32)],
        compiler_params=_params(1),
        name="retention",
    )(proj, proj, proj, proj, decay, q_dec, k_dec, gn)


def _chunk_attn_kernel(q_ref, k0_ref, k1_ref, k2_ref, v0_ref, v1_ref, v2_ref, bias_ref, *rest,
                       blocks_per_seq, n_cast):
    w_refs, o_ref, wb_refs = rest[:n_cast], rest[n_cast], rest[n_cast + 1:]
    t = pl.program_id(0) % blocks_per_seq
    k_refs = (k0_ref, k1_ref, k2_ref)
    v_refs = (v0_ref, v1_ref, v2_ref)
    add = lambda a, b: a + b
    lane_tiles = SEQ_BLOCK // HEAD_DIM

    def attend(first_block):
        blocks = range(first_block, LEFT_BLOCKS + 1)
        for w_ref, wb_ref in zip(w_refs, wb_refs):
            wb_ref[...] = w_ref[...].astype(BF16)
        for h in range(ATT_HEADS):
            sl = slice(h * HEAD_DIM, (h + 1) * HEAD_DIM)
            q = q_ref[:, sl]
            s = [_dot_nt(q, k_refs[b][:, sl]) + bias_ref[h, :, b * SEQ_BLOCK:(b + 1) * SEQ_BLOCK]
                 for b in blocks]
            tiles = [sb[:, c * HEAD_DIM:(c + 1) * HEAD_DIM] for sb in s for c in range(lane_tiles)]
            mx = jnp.max(functools.reduce(jnp.maximum, tiles), axis=-1, keepdims=True)
            p = [jnp.exp2(tile - mx) for tile in tiles]
            denom = jnp.sum(functools.reduce(add, p), axis=-1, keepdims=True)
            out = functools.reduce(add, [
                _dot(jnp.concatenate(p[i * lane_tiles:(i + 1) * lane_tiles], axis=1).astype(BF16),
                     v_refs[b][:, sl])
                for i, b in enumerate(blocks)])
            o_ref[:, sl] = (out / denom).astype(BF16)

    for first_block in range(LEFT_BLOCKS):
        pl.when(t == first_block)(functools.partial(attend, LEFT_BLOCKS - first_block))
    pl.when(t >= LEFT_BLOCKS)(functools.partial(attend, 0))


def _chunk_attn(proj, bias_tile, weights, *, seq):
    m = proj.shape[0]
    width = ATT_HEADS * HEAD_DIM
    blocks_per_seq = seq // SEQ_BLOCK
    steps = m // SEQ_BLOCK

    def window(section, b):
        def index(i):
            t = i % blocks_per_seq
            return (i - t + jnp.maximum(t - (LEFT_BLOCKS - b), 0), section)
        return pl.BlockSpec((SEQ_BLOCK, width), index)

    def slab(w):
        rows, cols = w.shape
        assert rows % (steps * BF16_SUBLANES) == 0
        return pl.BlockSpec((rows // steps, cols), lambda i: (i, 0))

    att, *weights_bf16 = pl.pallas_call(
        functools.partial(_chunk_attn_kernel, blocks_per_seq=blocks_per_seq, n_cast=len(weights)),
        out_shape=[jax.ShapeDtypeStruct((m, width), BF16)]
                  + [jax.ShapeDtypeStruct(w.shape, BF16) for w in weights],
        grid=(steps,),
        in_specs=[pl.BlockSpec((SEQ_BLOCK, width), lambda i: (i, 4))]
                 + [window(5, b) for b in range(LEFT_BLOCKS + 1)]
                 + [window(6, b) for b in range(LEFT_BLOCKS + 1)]
                 + [pl.BlockSpec(bias_tile.shape, lambda i: (0, 0, 0))]
                 + [slab(w) for w in weights],
        out_specs=[pl.BlockSpec((SEQ_BLOCK, width), lambda i: (i, 0))] + [slab(w) for w in weights],
        compiler_params=_params(1),
        name="chunk_attn",
    )(proj, proj, proj, proj, proj, proj, proj, bias_tile, *weights)
    return att, weights_bf16


def _out_proj_kernel(ret_ref, att_ref, wr_ref, wa_ref, x_ref, o_ref):
    o_ref[...] = x_ref[...] + (_dot(ret_ref[...], wr_ref[...]) + _dot(att_ref[...], wa_ref[...]))


def _out_proj(ret, att, w_bf16, x2d, *, tm, tn):
    m, d = x2d.shape
    kr, ka = ret.shape[1], att.shape[1]
    assert kr == ka and kr + ka == w_bf16.shape[0] and m % tm == 0 and d % tn == 0
    return pl.pallas_call(
        _out_proj_kernel,
        out_shape=jax.ShapeDtypeStruct((m, d), F32),
        grid=(m // tm, d // tn),
        in_specs=[
            pl.BlockSpec((tm, kr), lambda i, j: (i, 0)),
            pl.BlockSpec((tm, ka), lambda i, j: (i, 0)),
            pl.BlockSpec((kr, tn), lambda i, j: (0, j)),
            pl.BlockSpec((ka, tn), lambda i, j: (1, j)),
            pl.BlockSpec((tm, tn), lambda i, j: (i, j)),
        ],
        out_specs=pl.BlockSpec((tm, tn), lambda i, j: (i, j)),
        compiler_params=_params(2),
        name="out_proj",
    )(ret, att, w_bf16, w_bf16, x2d)


def _ffn_kernel(x_ref, g_ref, w1_ref, w2_ref, o_ref, h_ref):
    f = pl.program_id(1)

    @pl.when(f == 0)
    def _():
        x = x_ref[...]
        ms = jnp.mean(x * x, axis=-1, keepdims=True)
        h_ref[...] = (x * lax.rsqrt(ms + EPS) * g_ref[...]).astype(BF16)
        o_ref[...] = x

    a = jnp.maximum(_dot(h_ref[...], w1_ref[...]), 0.0)
    o_ref[...] += _dot((a * a).astype(BF16), w2_ref[...])


def _ffn(x2d, g2, w1_bf16, w2_bf16, *, tm, tf):
    m, d = x2d.shape
    d_ff = w1_bf16.shape[1]
    assert m % tm == 0 and d_ff % tf == 0
    return pl.pallas_call(
        _ffn_kernel,
        out_shape=jax.ShapeDtypeStruct((m, d), F32),
        grid=(m // tm, d_ff // tf),
        in_specs=[
            pl.BlockSpec((tm, d), lambda i, f: (i, 0)),
            pl.BlockSpec((1, d), lambda i, f: (0, 0)),
            pl.BlockSpec((d, tf), lambda i, f: (0, f)),
            pl.BlockSpec((tf, d), lambda i, f: (f, 0)),
        ],
        out_specs=pl.BlockSpec((tm, d), lambda i, f: (i, 0)),
        scratch_shapes=[pltpu.VMEM((tm, d), BF16)],
        compiler_params=_params(2),
        name="ffn",
    )(x2d, g2, w1_bf16, w2_bf16)


def _rotary_tables(seq):
    half = HEAD_DIM // 2
    inv_freq = ROPE_BASE ** (-np.arange(half, dtype=np.float64) / half)
    ang = np.arange(seq, dtype=np.float64)[:, None] * inv_freq[None, :]
    cos, sin = np.cos(ang), np.sin(ang)
    cos2 = np.concatenate([cos, cos], axis=1).astype(np.float32)
    sin2 = np.concatenate([-sin, sin], axis=1).astype(np.float32)
    return cos2, sin2


def kernel(x, norm1_g, w_in, ret_norm_g, q_norm_g, k_norm_g, rel_bias, w_out, norm2_g, w_ff1, w_ff2):
    b, s, d = x.shape
    depth = w_in.shape[0]
    assert s % SEQ_BLOCK == 0 and SEQ_BLOCK % CHUNK == 0
    assert w_in.shape[2] == 7 * RET_HEADS * HEAD_DIM and RET_HEADS == ATT_HEADS
    cos2, sin2 = _rotary_tables(s)
    row = lambda v: v.astype(F32).reshape(1, -1)
    x2d = x.reshape(b * s, d)
    for layer in range(depth):
        proj = _in_proj(x2d, row(norm1_g[layer]), w_in[layer].astype(BF16), cos2, sin2,
                        row(q_norm_g[layer]), row(k_norm_g[layer]), seq=s, tm=1024)
        ret = _retention(proj, row(ret_norm_g[layer]), seq=s)
        att, (wo, w1, w2) = _chunk_attn(proj, _bias_tile(rel_bias[layer]),
                                        (w_out[layer], w_ff1[layer], w_ff2[layer]), seq=s)
        x2d = _out_proj(ret, att, wo, x2d, tm=1024, tn=1024)
        x2d = _ffn(x2d, row(norm2_g[layer]), w1, w2, tm=512, tf=1024)
    return x2d.reshape(b, s, d)
```

```python
import functools
import math

import numpy as np
import jax
import jax.numpy as jnp
from jax import lax
from jax.experimental import pallas as pl
from jax.experimental.pallas import tpu as pltpu

CHUNK = 64
LEFT_CHUNKS = 8
REL_CLIP = 128
HEAD_DIM = 128
RET_HEADS = 8
ATT_HEADS = 8
ROPE_BASE = 10000.0
EPS = 1e-6
GN_EPS = 1e-5
NEG = -1e30
LOG2E = math.log2(math.e)

V7X_VMEM_LIMIT_BYTES = 56 * 1024 * 1024

SEQ_BLOCK = 256
LEFT_BLOCKS = LEFT_CHUNKS * CHUNK // SEQ_BLOCK
WINDOW = (LEFT_BLOCKS + 1) * SEQ_BLOCK
ATT_BLOCKS = 4
TOEPLITZ_WIDTH = 1024

F32 = jnp.float32
BF16 = jnp.bfloat16
BF16_SUBLANES = 16


def _params(n_axes, vmem_bytes=V7X_VMEM_LIMIT_BYTES):
    return pltpu.CompilerParams(
        dimension_semantics=("arbitrary",) * n_axes, vmem_limit_bytes=vmem_bytes)


def _dot(a, b):
    return jnp.dot(a, b, preferred_element_type=F32)


def _dot_nt(a, b):
    return lax.dot_general(a, b, (((1,), (1,)), ((), ())), preferred_element_type=F32)


def _dot_tn(a, b):
    return lax.dot_general(a, b, (((0,), (0,)), ((), ())), preferred_element_type=F32)


def _in_proj_kernel(x_ref, g_ref, w_ref, cos_ref, sin_ref, qg_ref, kg_ref, *rest,
                    n_heads_per_block, n_cast):
    w_refs, o_ref, wb_refs = rest[:n_cast], rest[n_cast], rest[n_cast + 1:2 * n_cast + 1]
    h_ref, r_ref = rest[2 * n_cast + 1:]
    j = pl.program_id(1)

    def cast_slabs():
        for src_ref, dst_ref in zip(w_refs, wb_refs):
            dst_ref[...] = src_ref[...].astype(BF16)

    heads = [slice(h * HEAD_DIM, (h + 1) * HEAD_DIM) for h in range(n_heads_per_block)]
    scale = HEAD_DIM ** -0.5

    def project():
        return _dot(h_ref[...], w_ref[...])

    def rotary(acc, post_scale):
        cos = cos_ref[...]
        sin = sin_ref[...]
        for sl in heads:
            a = acc[:, sl] * r_ref[...]
            y = a * cos + pltpu.roll(a, HEAD_DIM // 2, 1) * sin
            if post_scale is not None:
                y = y * post_scale
            o_ref[:, sl] = y.astype(BF16)

    def head_rms(acc, gain_ref, post_scale):
        gain = gain_ref[...]
        for sl in heads:
            a = acc[:, sl] * r_ref[...]
            ms = jnp.mean(a * a, axis=-1, keepdims=True)
            y = a * lax.rsqrt(ms + EPS) * gain
            if post_scale is not None:
                y = y * post_scale
            o_ref[:, sl] = y.astype(BF16)

    @pl.when(j == 0)
    def _():
        x = x_ref[...]
        ms = jnp.mean(x * x, axis=-1, keepdims=True)
        r_ref[...] = jnp.broadcast_to(lax.rsqrt(ms + EPS), r_ref.shape)
        h_ref[...] = (x * g_ref[...]).astype(BF16)
        cast_slabs()
        rotary(project(), None)

    @pl.when(j == 1)
    def _():
        cast_slabs()
        rotary(project(), scale)

    @pl.when(j == 4)
    def _():
        cast_slabs()
        head_rms(project(), qg_ref, scale * LOG2E)

    @pl.when(j == 5)
    def _():
        cast_slabs()
        head_rms(project(), kg_ref, None)

    @pl.when((j == 2) | (j == 3) | (j == 6))
    def _():
        acc = project()
        for sl in heads:
            o_ref[:, sl] = (acc[:, sl] * r_ref[...]).astype(BF16)


CAST_SECTIONS = (0, 1, 4, 5)


def _in_proj(x2d, g1, w_bf16, cos2, sin2, qg, kg, weights, *, seq, tm):
    m, d = x2d.shape
    n = w_bf16.shape[1]
    tn = n // 7
    assert tn % HEAD_DIM == 0 and m % tm == 0 and seq % tm == 0
    blocks_per_seq = seq // tm
    n_slabs = (m // tm) * len(CAST_SECTIONS)

    def slab(w):
        rows, cols = w.shape
        assert rows % (n_slabs * BF16_SUBLANES) == 0

        def index(i, j):
            done = sum((j >= s).astype(jnp.int32) for s in CAST_SECTIONS[1:])
            return (i * len(CAST_SECTIONS) + done, 0)
        return pl.BlockSpec((rows // n_slabs, cols), index)

    proj, *weights_bf16 = pl.pallas_call(
        functools.partial(_in_proj_kernel, n_heads_per_block=tn // HEAD_DIM, n_cast=len(weights)),
        out_shape=[jax.ShapeDtypeStruct((m, n), BF16)]
                  + [jax.ShapeDtypeStruct(w.shape, BF16) for w in weights],
        grid=(m // tm, 7),
        in_specs=[
            pl.BlockSpec((tm, d), lambda i, j: (i, 0)),
            pl.BlockSpec((1, d), lambda i, j: (0, 0)),
            pl.BlockSpec((d, tn), lambda i, j: (0, j)),
            pl.BlockSpec((tm, HEAD_DIM), lambda i, j: (i % blocks_per_seq, 0)),
            pl.BlockSpec((tm, HEAD_DIM), lambda i, j: (i % blocks_per_seq, 0)),
            pl.BlockSpec((1, HEAD_DIM), lambda i, j: (0, 0)),
            pl.BlockSpec((1, HEAD_DIM), lambda i, j: (0, 0)),
        ] + [slab(w) for w in weights],
        out_specs=[pl.BlockSpec((tm, tn), lambda i, j: (i, j))] + [slab(w) for w in weights],
        scratch_shapes=[pltpu.VMEM((tm, d), BF16), pltpu.VMEM((tm, HEAD_DIM), F32)],
        compiler_params=_params(2),
        name="in_proj",
    )(x2d, g1, w_bf16, cos2, sin2, qg, kg, *weights)
    return proj, weights_bf16


def _bias_tile_kernel(g_ref, o_ref):
    row = jnp.broadcast_to(g_ref[0], (SEQ_BLOCK, TOEPLITZ_WIDTH))
    t = pltpu.roll(row, 0, 1, stride=1, stride_axis=0)[:, :WINDOW]
    qc = lax.broadcasted_iota(jnp.int32, (SEQ_BLOCK, WINDOW), 0) // CHUNK
    kc = lax.broadcasted_iota(jnp.int32, (SEQ_BLOCK, WINDOW), 1) // CHUNK
    visible = (kc >= qc) & (kc <= qc + LEFT_CHUNKS)
    o_ref[0] = jnp.where(visible, t * LOG2E, NEG)


def _bias_tile(rel_bias):
    h, rel_size = rel_bias.shape
    assert rel_size == CHUNK + REL_CLIP
    far = rel_bias[:, rel_size - 1:]
    near = rel_bias[:, :1]
    left = LEFT_CHUNKS * CHUNK - REL_CLIP
    g = jnp.concatenate([
        jnp.broadcast_to(far, (h, left)),
        jnp.flip(rel_bias, axis=-1),
        jnp.broadcast_to(near, (h, WINDOW - left - rel_size)),
        jnp.broadcast_to(far, (h, TOEPLITZ_WIDTH - WINDOW)),
    ], axis=-1).astype(F32).reshape(h, 1, TOEPLITZ_WIDTH)
    return pl.pallas_call(
        _bias_tile_kernel,
        out_shape=jax.ShapeDtypeStruct((h, SEQ_BLOCK, WINDOW), F32),
        grid=(h,),
        in_specs=[pl.BlockSpec((1, 1, TOEPLITZ_WIDTH), lambda i: (i, 0, 0))],
        out_specs=pl.BlockSpec((1, SEQ_BLOCK, WINDOW), lambda i: (i, 0, 0)),
        compiler_params=_params(1),
        name="rel_bias_tile",
    )(g)


def _retention_tables():
    h = np.arange(RET_HEADS, dtype=np.float64)
    log_g = np.log1p(-np.exp2(-(5.0 + h)))
    pos = np.arange(SEQ_BLOCK, dtype=np.float64)
    dist = np.abs(pos[:, None] - pos[None, :])
    vis = (pos[None, :] // CHUNK) <= (pos[:, None] // CHUNK)
    decay = np.exp(log_g[:, None, None] * dist) * vis
    q_dec = np.exp(log_g[None, :] * (pos + 1.0)[:, None])
    k_dec = np.exp(log_g[None, :] * (SEQ_BLOCK - 1.0 - pos)[:, None])
    blk_dec = np.exp(log_g * SEQ_BLOCK)
    lanes = lambda t: np.repeat(t, HEAD_DIM, axis=1).astype(np.float32)
    return decay.astype(np.float32), lanes(q_dec), lanes(k_dec), [float(v) for v in blk_dec]


def _retention_kernel(q_ref, k_ref, v_ref, gate_ref, dec_ref, qdec_ref, kdec_ref, gn_ref,
                      o_ref, st_ref, *, blocks_per_seq, blk_dec):
    @pl.when(pl.program_id(0) % blocks_per_seq == 0)
    def _():
        st_ref[...] = jnp.zeros_like(st_ref)

    for h in range(RET_HEADS):
        sl = slice(h * HEAD_DIM, (h + 1) * HEAD_DIM)
        q = q_ref[:, sl]
        k = k_ref[:, sl]
        v = v_ref[:, sl]
        p = (_dot_nt(q, k) * dec_ref[h]).astype(BF16)
        qd = (q.astype(F32) * qdec_ref[:, sl]).astype(BF16)
        state = st_ref[h]
        y = _dot(p, v) + _dot(qd, state.astype(BF16))
        kd = (k.astype(F32) * kdec_ref[:, sl]).astype(BF16)
        st_ref[h] = blk_dec[h] * state + _dot_tn(kd, v)
        mu = jnp.mean(y, axis=-1, keepdims=True)
        yc = y - mu
        var = jnp.mean(yc * yc, axis=-1, keepdims=True)
        yn = yc * lax.rsqrt(var + GN_EPS) * gn_ref[:, sl]
        gate = gate_ref[:, sl].astype(F32)
        swish = gate / (1.0 + jnp.exp(-gate))
        o_ref[:, sl] = (swish * yn).astype(BF16)


def _retention(proj, gn, *, seq):
    m = proj.shape[0]
    width = RET_HEADS * HEAD_DIM
    decay, q_dec, k_dec, blk_dec = _retention_tables()
    blocks_per_seq = seq // SEQ_BLOCK
    section = lambda s: pl.BlockSpec((SEQ_BLOCK, width), lambda i: (i, s))
    whole = lambda shape: pl.BlockSpec(shape, lambda i: (0,) * len(shape))
    return pl.pallas_call(
        functools.partial(_retention_kernel, blocks_per_seq=blocks_per_seq, blk_dec=blk_dec),
        out_shape=jax.ShapeDtypeStruct((m, width), BF16),
        grid=(m // SEQ_BLOCK,),
        in_specs=[section(0), section(1), section(2), section(3),
                  whole(decay.shape), whole(q_dec.shape), whole(k_dec.shape), whole((1, width))],
        out_specs=pl.BlockSpec((SEQ_BLOCK, width), lambda i: (i, 0)),
        scratch_shapes=[pltpu.VMEM((RET_HEADS, HEAD_DIM, HEAD_DIM), F32)],
        compiler_params=_params(1),
        name="retention",
    )(proj, proj, proj, proj, decay, q_dec, k_dec, gn)


def _chunk_attn_kernel(q_ref, kp0_ref, kp1_ref, k_ref, vp0_ref, vp1_ref, v_ref, bias_ref, *rest,
                       steps_per_seq, n_cast):
    w_refs, o_ref, wb_refs = rest[:n_cast], rest[n_cast], rest[n_cast + 1:]
    first_step = pl.program_id(0) % steps_per_seq == 0
    add = lambda a, b: a + b
    lane_tiles = SEQ_BLOCK // HEAD_DIM
    rows = lambda i0, i1: slice(i0 * SEQ_BLOCK, (i1 + 1) * SEQ_BLOCK)

    def key_block(prev_refs, main_ref, b, sl):
        if b < LEFT_BLOCKS:
            return prev_refs[b][:, sl]
        return main_ref[rows(b - LEFT_BLOCKS, b - LEFT_BLOCKS), sl]

    def attend(first_block):
        for w_ref, wb_ref in zip(w_refs, wb_refs):
            wb_ref[...] = w_ref[...].astype(BF16)
        key_blocks = range(first_block, ATT_BLOCKS + LEFT_BLOCKS)
        users = {b: (max(0, b - LEFT_BLOCKS), min(ATT_BLOCKS - 1, b)) for b in key_blocks}
        for h in range(ATT_HEADS):
            sl = slice(h * HEAD_DIM, (h + 1) * HEAD_DIM)
            tiles = {i: [] for i in range(ATT_BLOCKS)}
            for b in key_blocks:
                i0, i1 = users[b]
                s = _dot_nt(q_ref[rows(i0, i1), sl], key_block((kp0_ref, kp1_ref), k_ref, b, sl))
                for i in range(i0, i1 + 1):
                    w = b - i
                    sb = (s[rows(i - i0, i - i0), :]
                          + bias_ref[h, :, w * SEQ_BLOCK:(w + 1) * SEQ_BLOCK])
                    tiles[i] += [sb[:, c * HEAD_DIM:(c + 1) * HEAD_DIM] for c in range(lane_tiles)]
            probs, denom = {}, {}
            for i in range(ATT_BLOCKS):
                mx = jnp.max(functools.reduce(jnp.maximum, tiles[i]), axis=-1, keepdims=True)
                p = [jnp.exp2(tile - mx) for tile in tiles[i]]
                denom[i] = jnp.sum(functools.reduce(add, p), axis=-1, keepdims=True)
                visible = [b for b in key_blocks if users[b][0] <= i <= users[b][1]]
                for n, b in enumerate(visible):
                    probs[i, b] = jnp.concatenate(
                        p[n * lane_tiles:(n + 1) * lane_tiles], axis=1).astype(BF16)
            out = {i: [] for i in range(ATT_BLOCKS)}
            for b in key_blocks:
                i0, i1 = users[b]
                lhs = jnp.concatenate([probs[i, b] for i in range(i0, i1 + 1)], axis=0)
                pv = _dot(lhs, key_block((vp0_ref, vp1_ref), v_ref, b, sl))
                for i in range(i0, i1 + 1):
                    out[i].append(pv[rows(i - i0, i - i0), :])
            for i in range(ATT_BLOCKS):
                o_ref[rows(i, i), sl] = (functools.reduce(add, out[i]) / denom[i]).astype(BF16)

    pl.when(first_step)(functools.partial(attend, LEFT_BLOCKS))
    pl.when(jnp.logical_not(first_step))(functools.partial(attend, 0))


def _chunk_attn(proj, bias_tile, weights, *, seq):
    m = proj.shape[0]
    width = ATT_HEADS * HEAD_DIM
    step_rows = ATT_BLOCKS * SEQ_BLOCK
    assert seq % step_rows == 0 and ATT_BLOCKS >= LEFT_BLOCKS
    steps_per_seq = seq // step_rows
    steps = m // step_rows

    def previous(section, b):
        def index(i):
            seq_start = (i - i % steps_per_seq) * ATT_BLOCKS
            return (jnp.maximum(i * ATT_BLOCKS - (LEFT_BLOCKS - b), seq_start), section)
        return pl.BlockSpec((SEQ_BLOCK, width), index)

    main = lambda section: pl.BlockSpec((step_rows, width), lambda i: (i, section))

    def slab(w):
        rows, cols = w.shape
        assert rows % (steps * BF16_SUBLANES) == 0
        return pl.BlockSpec((rows // steps, cols), lambda i: (i, 0))

    att, *weights_bf16 = pl.pallas_call(
        functools.partial(_chunk_attn_kernel, steps_per_seq=steps_per_seq, n_cast=len(weights)),
        out_shape=[jax.ShapeDtypeStruct((m, width), BF16)]
                  + [jax.ShapeDtypeStruct(w.shape, BF16) for w in weights],
        grid=(steps,),
        in_specs=[main(4)]
                 + [previous(5, b) for b in range(LEFT_BLOCKS)] + [main(5)]
                 + [previous(6, b) for b in range(LEFT_BLOCKS)] + [main(6)]
                 + [pl.BlockSpec(bias_tile.shape, lambda i: (0, 0, 0))]
                 + [slab(w) for w in weights],
        out_specs=[pl.BlockSpec((step_rows, width), lambda i: (i, 0))] + [slab(w) for w in weights],
        compiler_params=_params(1),
        name="chunk_attn",
    )(proj, proj, proj, proj, proj, proj, proj, bias_tile, *weights)
    return att, weights_bf16


def _out_proj_kernel(ret_ref, att_ref, wr_ref, wa_ref, x_ref, o_ref):
    o_ref[...] = x_ref[...] + (_dot(ret_ref[...], wr_ref[...]) + _dot(att_ref[...], wa_ref[...]))


def _out_proj(ret, att, w_bf16, x2d, *, tm, tn):
    m, d = x2d.shape
    kr, ka = ret.shape[1], att.shape[1]
    assert kr == ka and kr + ka == w_bf16.shape[0] and m % tm == 0 and d % tn == 0
    return pl.pallas_call(
        _out_proj_kernel,
        out_shape=jax.ShapeDtypeStruct((m, d), F32),
        grid=(m // tm, d // tn),
        in_specs=[
            pl.BlockSpec((tm, kr), lambda i, j: (i, 0)),
            pl.BlockSpec((tm, ka), lambda i, j: (i, 0)),
            pl.BlockSpec((kr, tn), lambda i, j: (0, j)),
            pl.BlockSpec((ka, tn), lambda i, j: (1, j)),
            pl.BlockSpec((tm, tn), lambda i, j: (i, j)),
        ],
        out_specs=pl.BlockSpec((tm, tn), lambda i, j: (i, j)),
        compiler_params=_params(2),
        name="out_proj",
    )(ret, att, w_bf16, w_bf16, x2d)


def _ffn_kernel(x_ref, g_ref, w1_ref, w2_ref, o_ref, h_ref):
    f = pl.program_id(1)

    @pl.when(f == 0)
    def _():
        x = x_ref[...]
        ms = jnp.mean(x * x, axis=-1, keepdims=True)
        h_ref[...] = (x * lax.rsqrt(ms + EPS) * g_ref[...]).astype(BF16)
        o_ref[...] = x

    a = jnp.maximum(_dot(h_ref[...], w1_ref[...]), 0.0)
    o_ref[...] += _dot((a * a).astype(BF16), w2_ref[...])


def _ffn(x2d, g2, w1_bf16, w2_bf16, *, tm, tf):
    m, d = x2d.shape
    d_ff = w1_bf16.shape[1]
    assert m % tm == 0 and d_ff % tf == 0
    return pl.pallas_call(
        _ffn_kernel,
        out_shape=jax.ShapeDtypeStruct((m, d), F32),
        grid=(m // tm, d_ff // tf),
        in_specs=[
            pl.BlockSpec((tm, d), lambda i, f: (i, 0)),
            pl.BlockSpec((1, d), lambda i, f: (0, 0)),
            pl.BlockSpec((d, tf), lambda i, f: (0, f)),
            pl.BlockSpec((tf, d), lambda i, f: (f, 0)),
        ],
        out_specs=pl.BlockSpec((tm, d), lambda i, f: (i, 0)),
        scratch_shapes=[pltpu.VMEM((tm, d), BF16)],
        compiler_params=_params(2),
        name="ffn",
    )(x2d, g2, w1_bf16, w2_bf16)


def _rotary_tables(seq):
    half = HEAD_DIM // 2
    inv_freq = ROPE_BASE ** (-np.arange(half, dtype=np.float64) / half)
    ang = np.arange(seq, dtype=np.float64)[:, None] * inv_freq[None, :]
    cos, sin = np.cos(ang), np.sin(ang)
    cos2 = np.concatenate([cos, cos], axis=1).astype(np.float32)
    sin2 = np.concatenate([-sin, sin], axis=1).astype(np.float32)
    return cos2, sin2


def kernel(x, norm1_g, w_in, ret_norm_g, q_norm_g, k_norm_g, rel_bias, w_out, norm2_g, w_ff1, w_ff2):
    b, s, d = x.shape
    depth = w_in.shape[0]
    assert s % SEQ_BLOCK == 0 and SEQ_BLOCK % CHUNK == 0
    assert w_in.shape[2] == 7 * RET_HEADS * HEAD_DIM and RET_HEADS == ATT_HEADS
    cos2, sin2 = _rotary_tables(s)
    row = lambda v: v.astype(F32).reshape(1, -1)
    x2d = x.reshape(b * s, d)
    for layer in range(depth):
        proj, (w1, w2) = _in_proj(x2d, row(norm1_g[layer]), w_in[layer].astype(BF16), cos2, sin2,
                                  row(q_norm_g[layer]), row(k_norm_g[layer]),
                                  (w_ff1[layer], w_ff2[layer]), seq=s, tm=1024)
        ret = _retention(proj, row(ret_norm_g[layer]), seq=s)
        att, (wo,) = _chunk_attn(proj, _bias_tile(rel_bias[layer]), (w_out[layer],), seq=s)
        x2d = _out_proj(ret, att, wo, x2d, tm=1024, tn=1024)
        x2d = _ffn(x2d, row(norm2_g[layer]), w1, w2, tm=512, tf=1024)
    return x2d.reshape(b, s, d)
```

```python
import functools
import math

import numpy as np
import jax
import jax.numpy as jnp
from jax import lax
from jax.experimental import pallas as pl
from jax.experimental.pallas import tpu as pltpu

CHUNK = 64
LEFT_CHUNKS = 8
REL_CLIP = 128
HEAD_DIM = 128
RET_HEADS = 8
ATT_HEADS = 8
ROPE_BASE = 10000.0
EPS = 1e-6
GN_EPS = 1e-5
NEG = -1e30
LOG2E = math.log2(math.e)

V7X_VMEM_LIMIT_BYTES = 56 * 1024 * 1024

SEQ_BLOCK = 256
LEFT_BLOCKS = LEFT_CHUNKS * CHUNK // SEQ_BLOCK
WINDOW = (LEFT_BLOCKS + 1) * SEQ_BLOCK
ATT_BLOCKS = 4
TOEPLITZ_WIDTH = 1024

F32 = jnp.float32
BF16 = jnp.bfloat16
BF16_SUBLANES = 16


def _params(n_axes, vmem_bytes=V7X_VMEM_LIMIT_BYTES):
    return pltpu.CompilerParams(
        dimension_semantics=("arbitrary",) * n_axes, vmem_limit_bytes=vmem_bytes)


def _dot(a, b):
    return jnp.dot(a, b, preferred_element_type=F32)


def _dot_nt(a, b):
    return lax.dot_general(a, b, (((1,), (1,)), ((), ())), preferred_element_type=F32)


def _dot_tn(a, b):
    return lax.dot_general(a, b, (((0,), (0,)), ((), ())), preferred_element_type=F32)


def _in_proj_kernel(x_ref, g_ref, w_ref, cos_ref, sin_ref, qg_ref, kg_ref, *rest,
                    n_heads_per_block, cast_sections):
    n_cast = len(cast_sections)
    w_refs, o_ref, wb_refs = rest[:n_cast], rest[n_cast], rest[n_cast + 1:2 * n_cast + 1]
    h_ref, r_ref = rest[2 * n_cast + 1:]
    j = pl.program_id(1)

    def cast_slabs(*sections):
        for src_ref, dst_ref, cast_at in zip(w_refs, wb_refs, cast_sections):
            assert all(s in cast_at for s in sections) or not any(s in cast_at for s in sections)
            if sections[0] in cast_at:
                dst_ref[...] = src_ref[...].astype(BF16)

    heads = [slice(h * HEAD_DIM, (h + 1) * HEAD_DIM) for h in range(n_heads_per_block)]
    scale = HEAD_DIM ** -0.5

    def project():
        return _dot(h_ref[...], w_ref[...])

    def rotary(acc, post_scale):
        cos = cos_ref[...]
        sin = sin_ref[...]
        for sl in heads:
            a = acc[:, sl] * r_ref[...]
            y = a * cos + pltpu.roll(a, HEAD_DIM // 2, 1) * sin
            if post_scale is not None:
                y = y * post_scale
            o_ref[:, sl] = y.astype(BF16)

    def head_rms(acc, gain_ref, post_scale):
        gain = gain_ref[...]
        for sl in heads:
            a = acc[:, sl] * r_ref[...]
            ms = jnp.mean(a * a, axis=-1, keepdims=True)
            y = a * lax.rsqrt(ms + EPS) * gain
            if post_scale is not None:
                y = y * post_scale
            o_ref[:, sl] = y.astype(BF16)

    @pl.when(j == 0)
    def _():
        x = x_ref[...]
        ms = jnp.mean(x * x, axis=-1, keepdims=True)
        r_ref[...] = jnp.broadcast_to(lax.rsqrt(ms + EPS), r_ref.shape)
        h_ref[...] = (x * g_ref[...]).astype(BF16)
        cast_slabs(0)
        rotary(project(), None)

    @pl.when(j == 1)
    def _():
        cast_slabs(1)
        rotary(project(), scale)

    @pl.when(j == 4)
    def _():
        cast_slabs(4)
        head_rms(project(), qg_ref, scale * LOG2E)

    @pl.when(j == 5)
    def _():
        cast_slabs(5)
        head_rms(project(), kg_ref, None)

    @pl.when((j == 2) | (j == 3) | (j == 6))
    def _():
        cast_slabs(2, 3, 6)
        acc = project()
        for sl in heads:
            o_ref[:, sl] = (acc[:, sl] * r_ref[...]).astype(BF16)


def _in_proj(x2d, g1, w_bf16, cos2, sin2, qg, kg, weights, cast_sections, *, seq, tm):
    m, d = x2d.shape
    n = w_bf16.shape[1]
    tn = n // 7
    assert tn % HEAD_DIM == 0 and m % tm == 0 and seq % tm == 0
    blocks_per_seq = seq // tm

    def slab(w, cast_at):
        rows, cols = w.shape
        n_slabs = (m // tm) * len(cast_at)
        assert rows % (n_slabs * BF16_SUBLANES) == 0 and cast_at[0] == 0 and list(cast_at) == sorted(cast_at)

        def index(i, j):
            done = sum((j >= s).astype(jnp.int32) for s in cast_at[1:])
            return (i * len(cast_at) + done, 0)
        return pl.BlockSpec((rows // n_slabs, cols), index)

    slabs = [slab(w, cast_at) for w, cast_at in zip(weights, cast_sections)]
    proj, *weights_bf16 = pl.pallas_call(
        functools.partial(_in_proj_kernel, n_heads_per_block=tn // HEAD_DIM,
                          cast_sections=tuple(cast_sections)),
        out_shape=[jax.ShapeDtypeStruct((m, n), BF16)]
                  + [jax.ShapeDtypeStruct(w.shape, BF16) for w in weights],
        grid=(m // tm, 7),
        in_specs=[
            pl.BlockSpec((tm, d), lambda i, j: (i, 0)),
            pl.BlockSpec((1, d), lambda i, j: (0, 0)),
            pl.BlockSpec((d, tn), lambda i, j: (0, j)),
            pl.BlockSpec((tm, HEAD_DIM), lambda i, j: (i % blocks_per_seq, 0)),
            pl.BlockSpec((tm, HEAD_DIM), lambda i, j: (i % blocks_per_seq, 0)),
            pl.BlockSpec((1, HEAD_DIM), lambda i, j: (0, 0)),
            pl.BlockSpec((1, HEAD_DIM), lambda i, j: (0, 0)),
        ] + slabs,
        out_specs=[pl.BlockSpec((tm, tn), lambda i, j: (i, j))] + slabs,
        scratch_shapes=[pltpu.VMEM((tm, d), BF16), pltpu.VMEM((tm, HEAD_DIM), F32)],
        compiler_params=_params(2),
        name="in_proj",
    )(x2d, g1, w_bf16, cos2, sin2, qg, kg, *weights)
    return proj, weights_bf16


def _bias_tile_kernel(g_ref, o_ref):
    row = jnp.broadcast_to(g_ref[0], (SEQ_BLOCK, TOEPLITZ_WIDTH))
    t = pltpu.roll(row, 0, 1, stride=1, stride_axis=0)[:, :WINDOW]
    qc = lax.broadcasted_iota(jnp.int32, (SEQ_BLOCK, WINDOW), 0) // CHUNK
    kc = lax.broadcasted_iota(jnp.int32, (SEQ_BLOCK, WINDOW), 1) // CHUNK
    visible = (kc >= qc) & (kc <= qc + LEFT_CHUNKS)
    o_ref[0] = jnp.where(visible, t * LOG2E, NEG)


def _bias_tile(rel_bias):
    h, rel_size = rel_bias.shape
    assert rel_size == CHUNK + REL_CLIP
    far = rel_bias[:, rel_size - 1:]
    near = rel_bias[:, :1]
    left = LEFT_CHUNKS * CHUNK - REL_CLIP
    g = jnp.concatenate([
        jnp.broadcast_to(far, (h, left)),
        jnp.flip(rel_bias, axis=-1),
        jnp.broadcast_to(near, (h, WINDOW - left - rel_size)),
        jnp.broadcast_to(far, (h, TOEPLITZ_WIDTH - WINDOW)),
    ], axis=-1).astype(F32).reshape(h, 1, TOEPLITZ_WIDTH)
    return pl.pallas_call(
        _bias_tile_kernel,
        out_shape=jax.ShapeDtypeStruct((h, SEQ_BLOCK, WINDOW), F32),
        grid=(h,),
        in_specs=[pl.BlockSpec((1, 1, TOEPLITZ_WIDTH), lambda i: (i, 0, 0))],
        out_specs=pl.BlockSpec((1, SEQ_BLOCK, WINDOW), lambda i: (i, 0, 0)),
        compiler_params=_params(1),
        name="rel_bias_tile",
    )(g)


def _retention_tables():
    h = np.arange(RET_HEADS, dtype=np.float64)
    log_g = np.log1p(-np.exp2(-(5.0 + h)))
    pos = np.arange(SEQ_BLOCK, dtype=np.float64)
    dist = np.abs(pos[:, None] - pos[None, :])
    vis = (pos[None, :] // CHUNK) <= (pos[:, None] // CHUNK)
    decay = np.exp(log_g[:, None, None] * dist) * vis
    q_dec = np.exp(log_g[None, :] * (pos + 1.0)[:, None])
    k_dec = np.exp(log_g[None, :] * (SEQ_BLOCK - 1.0 - pos)[:, None])
    blk_dec = np.exp(log_g * SEQ_BLOCK)
    lanes = lambda t: np.repeat(t, HEAD_DIM, axis=1).astype(np.float32)
    return decay.astype(np.float32), lanes(q_dec), lanes(k_dec), [float(v) for v in blk_dec]


def _retention_kernel(q_ref, k_ref, v_ref, gate_ref, dec_ref, qdec_ref, kdec_ref, gn_ref,
                      o_ref, st_ref, *, blocks_per_seq, blk_dec):
    @pl.when(pl.program_id(0) % blocks_per_seq == 0)
    def _():
        st_ref[...] = jnp.zeros_like(st_ref)

    for h in range(RET_HEADS):
        sl = slice(h * HEAD_DIM, (h + 1) * HEAD_DIM)
        q = q_ref[:, sl]
        k = k_ref[:, sl]
        v = v_ref[:, sl]
        p = (_dot_nt(q, k) * dec_ref[h]).astype(BF16)
        qd = (q.astype(F32) * qdec_ref[:, sl]).astype(BF16)
        state = st_ref[h]
        y = _dot(p, v) + _dot(qd, state.astype(BF16))
        kd = (k.astype(F32) * kdec_ref[:, sl]).astype(BF16)
        st_ref[h] = blk_dec[h] * state + _dot_tn(kd, v)
        mu = jnp.mean(y, axis=-1, keepdims=True)
        yc = y - mu
        var = jnp.mean(yc * yc, axis=-1, keepdims=True)
        yn = yc * lax.rsqrt(var + GN_EPS) * gn_ref[:, sl]
        gate = gate_ref[:, sl].astype(F32)
        swish = gate / (1.0 + jnp.exp(-gate))
        o_ref[:, sl] = (swish * yn).astype(BF16)


def _retention(proj, gn, *, seq):
    m = proj.shape[0]
    width = RET_HEADS * HEAD_DIM
    decay, q_dec, k_dec, blk_dec = _retention_tables()
    blocks_per_seq = seq // SEQ_BLOCK
    section = lambda s: pl.BlockSpec((SEQ_BLOCK, width), lambda i: (i, s))
    whole = lambda shape: pl.BlockSpec(shape, lambda i: (0,) * len(shape))
    return pl.pallas_call(
        functools.partial(_retention_kernel, blocks_per_seq=blocks_per_seq, blk_dec=blk_dec),
        out_shape=jax.ShapeDtypeStruct((m, width), BF16),
        grid=(m // SEQ_BLOCK,),
        in_specs=[section(0), section(1), section(2), section(3),
                  whole(decay.shape), whole(q_dec.shape), whole(k_dec.shape), whole((1, width))],
        out_specs=pl.BlockSpec((SEQ_BLOCK, width), lambda i: (i, 0)),
        scratch_shapes=[pltpu.VMEM((RET_HEADS, HEAD_DIM, HEAD_DIM), F32)],
        compiler_params=_params(1),
        name="retention",
    )(proj, proj, proj, proj, decay, q_dec, k_dec, gn)


def _chunk_attn_kernel(q_ref, kp0_ref, kp1_ref, k_ref, vp0_ref, vp1_ref, v_ref, bias_ref, *rest,
                       steps_per_seq, n_cast):
    w_refs, o_ref, wb_refs = rest[:n_cast], rest[n_cast], rest[n_cast + 1:]
    first_step = pl.program_id(0) % steps_per_seq == 0
    add = lambda a, b: a + b
    lane_tiles = SEQ_BLOCK // HEAD_DIM
    rows = lambda i0, i1: slice(i0 * SEQ_BLOCK, (i1 + 1) * SEQ_BLOCK)

    def key_block(prev_refs, main_ref, b, sl):
        if b < LEFT_BLOCKS:
            return prev_refs[b][:, sl]
        return main_ref[rows(b - LEFT_BLOCKS, b - LEFT_BLOCKS), sl]

    def attend(first_block):
        for w_ref, wb_ref in zip(w_refs, wb_refs):
            wb_ref[...] = w_ref[...].astype(BF16)
        key_blocks = range(first_block, ATT_BLOCKS + LEFT_BLOCKS)
        users = {b: (max(0, b - LEFT_BLOCKS), min(ATT_BLOCKS - 1, b)) for b in key_blocks}
        for h in range(ATT_HEADS):
            sl = slice(h * HEAD_DIM, (h + 1) * HEAD_DIM)
            tiles = {i: [] for i in range(ATT_BLOCKS)}
            for b in key_blocks:
                i0, i1 = users[b]
                s = _dot_nt(q_ref[rows(i0, i1), sl], key_block((kp0_ref, kp1_ref), k_ref, b, sl))
                for i in range(i0, i1 + 1):
                    w = b - i
                    sb = (s[rows(i - i0, i - i0), :]
                          + bias_ref[h, :, w * SEQ_BLOCK:(w + 1) * SEQ_BLOCK])
                    tiles[i] += [sb[:, c * HEAD_DIM:(c + 1) * HEAD_DIM] for c in range(lane_tiles)]
            probs, denom = {}, {}
            for i in range(ATT_BLOCKS):
                mx = jnp.max(functools.reduce(jnp.maximum, tiles[i]), axis=-1, keepdims=True)
                p = [jnp.exp2(tile - mx) for tile in tiles[i]]
                denom[i] = jnp.sum(functools.reduce(add, p), axis=-1, keepdims=True)
                visible = [b for b in key_blocks if users[b][0] <= i <= users[b][1]]
                for n, b in enumerate(visible):
                    probs[i, b] = jnp.concatenate(
                        p[n * lane_tiles:(n + 1) * lane_tiles], axis=1).astype(BF16)
            out = {i: [] for i in range(ATT_BLOCKS)}
            for b in key_blocks:
                i0, i1 = users[b]
                lhs = jnp.concatenate([probs[i, b] for i in range(i0, i1 + 1)], axis=0)
                pv = _dot(lhs, key_block((vp0_ref, vp1_ref), v_ref, b, sl))
                for i in range(i0, i1 + 1):
                    out[i].append(pv[rows(i - i0, i - i0), :])
            for i in range(ATT_BLOCKS):
                o_ref[rows(i, i), sl] = (functools.reduce(add, out[i]) / denom[i]).astype(BF16)

    pl.when(first_step)(functools.partial(attend, LEFT_BLOCKS))
    pl.when(jnp.logical_not(first_step))(functools.partial(attend, 0))


def _chunk_attn(proj, bias_tile, weights, *, seq):
    m = proj.shape[0]
    width = ATT_HEADS * HEAD_DIM
    step_rows = ATT_BLOCKS * SEQ_BLOCK
    assert seq % step_rows == 0 and ATT_BLOCKS >= LEFT_BLOCKS
    steps_per_seq = seq // step_rows
    steps = m // step_rows

    def previous(section, b):
        def index(i):
            seq_start = (i - i % steps_per_seq) * ATT_BLOCKS
            return (jnp.maximum(i * ATT_BLOCKS - (LEFT_BLOCKS - b), seq_start), section)
        return pl.BlockSpec((SEQ_BLOCK, width), index)

    main = lambda section: pl.BlockSpec((step_rows, width), lambda i: (i, section))

    def slab(w):
        rows, cols = w.shape
        assert rows % (steps * BF16_SUBLANES) == 0
        return pl.BlockSpec((rows // steps, cols), lambda i: (i, 0))

    att, *weights_bf16 = pl.pallas_call(
        functools.partial(_chunk_attn_kernel, steps_per_seq=steps_per_seq, n_cast=len(weights)),
        out_shape=[jax.ShapeDtypeStruct((m, width), BF16)]
                  + [jax.ShapeDtypeStruct(w.shape, BF16) for w in weights],
        grid=(steps,),
        in_specs=[main(4)]
                 + [previous(5, b) for b in range(LEFT_BLOCKS)] + [main(5)]
                 + [previous(6, b) for b in range(LEFT_BLOCKS)] + [main(6)]
                 + [pl.BlockSpec(bias_tile.shape, lambda i: (0, 0, 0))]
                 + [slab(w) for w in weights],
        out_specs=[pl.BlockSpec((step_rows, width), lambda i: (i, 0))] + [slab(w) for w in weights],
        compiler_params=_params(1),
        name="chunk_attn",
    )(proj, proj, proj, proj, proj, proj, proj, bias_tile, *weights)
    return att, weights_bf16


def _out_proj_kernel(ret_ref, att_ref, wr_ref, wa_ref, x_ref, o_ref):
    o_ref[...] = x_ref[...] + (_dot(ret_ref[...], wr_ref[...]) + _dot(att_ref[...], wa_ref[...]))


def _out_proj(ret, att, w_bf16, x2d, *, tm, tn):
    m, d = x2d.shape
    kr, ka = ret.shape[1], att.shape[1]
    assert kr == ka and kr + ka == w_bf16.shape[0] and m % tm == 0 and d % tn == 0
    return pl.pallas_call(
        _out_proj_kernel,
        out_shape=jax.ShapeDtypeStruct((m, d), F32),
        grid=(m // tm, d // tn),
        in_specs=[
            pl.BlockSpec((tm, kr), lambda i, j: (i, 0)),
            pl.BlockSpec((tm, ka), lambda i, j: (i, 0)),
            pl.BlockSpec((kr, tn), lambda i, j: (0, j)),
            pl.BlockSpec((ka, tn), lambda i, j: (1, j)),
            pl.BlockSpec((tm, tn), lambda i, j: (i, j)),
        ],
        out_specs=pl.BlockSpec((tm, tn), lambda i, j: (i, j)),
        compiler_params=_params(2),
        name="out_proj",
    )(ret, att, w_bf16, w_bf16, x2d)


def _ffn_kernel(x_ref, g_ref, w1_ref, w2_ref, o_ref, h_ref, r_ref, hid_ref, *, n_chunks):
    f = pl.program_id(1)

    def up(slot):
        a = _dot(h_ref[...], w1_ref[...])
        for c in range(a.shape[1] // HEAD_DIM):
            lanes = slice(c * HEAD_DIM, (c + 1) * HEAD_DIM)
            act = jnp.maximum(a[:, lanes] * r_ref[...], 0.0)
            hid_ref[slot, :, lanes] = (act * act).astype(BF16)

    def down(slot):
        o_ref[...] += _dot(hid_ref[slot], w2_ref[...])

    @pl.when(f == 0)
    def _():
        x = x_ref[...]
        ms = jnp.mean(x * x, axis=-1, keepdims=True)
        r_ref[...] = jnp.broadcast_to(lax.rsqrt(ms + EPS), r_ref.shape)
        h_ref[...] = (x * g_ref[...]).astype(BF16)
        o_ref[...] = x
        up(0)

    for parity in range(2):
        @pl.when((f > 0) & (f < n_chunks) & (f % 2 == parity))
        def _():
            down(1 - parity)
            up(parity)

    pl.when(f == n_chunks)(lambda: down((n_chunks - 1) % 2))


def _ffn(x2d, g2, w1_bf16, w2_bf16, *, tm, tf):
    m, d = x2d.shape
    d_ff = w1_bf16.shape[1]
    assert m % tm == 0 and d_ff % tf == 0 and tf % HEAD_DIM == 0
    n_chunks = d_ff // tf
    return pl.pallas_call(
        functools.partial(_ffn_kernel, n_chunks=n_chunks),
        out_shape=jax.ShapeDtypeStruct((m, d), F32),
        grid=(m // tm, n_chunks + 1),
        in_specs=[
            pl.BlockSpec((tm, d), lambda i, f: (i, 0)),
            pl.BlockSpec((1, d), lambda i, f: (0, 0)),
            pl.BlockSpec((d, tf), lambda i, f: (0, jnp.minimum(f, n_chunks - 1))),
            pl.BlockSpec((tf, d), lambda i, f: (jnp.maximum(f - 1, 0), 0)),
        ],
        out_specs=pl.BlockSpec((tm, d), lambda i, f: (i, 0)),
        scratch_shapes=[pltpu.VMEM((tm, d), BF16), pltpu.VMEM((tm, HEAD_DIM), F32),
                        pltpu.VMEM((2, tm, tf), BF16)],
        compiler_params=_params(2),
        name="ffn",
    )(x2d, g2, w1_bf16, w2_bf16)


def _rotary_tables(seq):
    half = HEAD_DIM // 2
    inv_freq = ROPE_BASE ** (-np.arange(half, dtype=np.float64) / half)
    ang = np.arange(seq, dtype=np.float64)[:, None] * inv_freq[None, :]
    cos, sin = np.cos(ang), np.sin(ang)
    cos2 = np.concatenate([cos, cos], axis=1).astype(np.float32)
    sin2 = np.concatenate([-sin, sin], axis=1).astype(np.float32)
    return cos2, sin2


def kernel(x, norm1_g, w_in, ret_norm_g, q_norm_g, k_norm_g, rel_bias, w_out, norm2_g, w_ff1, w_ff2):
    b, s, d = x.shape
    depth = w_in.shape[0]
    assert s % SEQ_BLOCK == 0 and SEQ_BLOCK % CHUNK == 0
    assert w_in.shape[2] == 7 * RET_HEADS * HEAD_DIM and RET_HEADS == ATT_HEADS
    cos2, sin2 = _rotary_tables(s)
    row = lambda v: v.astype(F32).reshape(1, -1)
    x2d = x.reshape(b * s, d)
    for layer in range(depth):
        proj, (w1, w2) = _in_proj(x2d, row(norm1_g[layer]), w_in[layer].astype(BF16), cos2, sin2,
                                  row(q_norm_g[layer]), row(k_norm_g[layer]),
                                  (w_ff1[layer], w_ff2[layer]), ((0, 1, 4, 5), (0, 2, 3, 6)),
                                  seq=s, tm=1024)
        ret = _retention(proj, row(ret_norm_g[layer]), seq=s)
        att, (wo,) = _chunk_attn(proj, _bias_tile(rel_bias[layer]), (w_out[layer],), seq=s)
        x2d = _out_proj(ret, att, wo, x2d, tm=1024, tn=1024)
        x2d = _ffn(x2d, row(norm2_g[layer]), w1, w2, tm=512, tf=1024)
    return x2d.reshape(b, s, d)
```

```python
import functools
import math

import numpy as np
import jax
import jax.numpy as jnp
from jax import lax
from jax.experimental import pallas as pl
from jax.experimental.pallas import tpu as pltpu

CHUNK = 64
LEFT_CHUNKS = 8
REL_CLIP = 128
HEAD_DIM = 128
RET_HEADS = 8
ATT_HEADS = 8
ROPE_BASE = 10000.0
EPS = 1e-6
GN_EPS = 1e-5
NEG = -1e30
LOG2E = math.log2(math.e)

V7X_VMEM_LIMIT_BYTES = 56 * 1024 * 1024

SEQ_BLOCK = 256
LEFT_BLOCKS = LEFT_CHUNKS * CHUNK // SEQ_BLOCK
WINDOW = (LEFT_BLOCKS + 1) * SEQ_BLOCK
ATT_BLOCKS = 4
TOEPLITZ_WIDTH = 1024

F32 = jnp.float32
BF16 = jnp.bfloat16
BF16_SUBLANES = 16


def _params(n_axes, vmem_bytes=V7X_VMEM_LIMIT_BYTES):
    return pltpu.CompilerParams(
        dimension_semantics=("arbitrary",) * n_axes, vmem_limit_bytes=vmem_bytes)


def _dot(a, b):
    return jnp.dot(a, b, preferred_element_type=F32)


def _dot_nt(a, b):
    return lax.dot_general(a, b, (((1,), (1,)), ((), ())), preferred_element_type=F32)


def _dot_tn(a, b):
    return lax.dot_general(a, b, (((0,), (0,)), ((), ())), preferred_element_type=F32)


def _in_proj_kernel(x_ref, g_ref, w_ref, cos_ref, sin_ref, qg_ref, kg_ref, *rest,
                    n_heads_per_block, cast_sections):
    n_cast = len(cast_sections)
    w_refs, o_ref, wb_refs = rest[:n_cast], rest[n_cast], rest[n_cast + 1:2 * n_cast + 1]
    h_ref, r_ref = rest[2 * n_cast + 1:]
    j = pl.program_id(1)

    def cast_slabs(*sections):
        for src_ref, dst_ref, cast_at in zip(w_refs, wb_refs, cast_sections):
            assert all(s in cast_at for s in sections) or not any(s in cast_at for s in sections)
            if sections[0] in cast_at:
                dst_ref[...] = src_ref[...].astype(BF16)

    heads = [slice(h * HEAD_DIM, (h + 1) * HEAD_DIM) for h in range(n_heads_per_block)]
    scale = HEAD_DIM ** -0.5

    def project():
        return _dot(h_ref[...], w_ref[...])

    def rotary(acc, post_scale):
        cos = cos_ref[...]
        sin = sin_ref[...]
        for sl in heads:
            a = acc[:, sl] * r_ref[...]
            y = a * cos + pltpu.roll(a, HEAD_DIM // 2, 1) * sin
            if post_scale is not None:
                y = y * post_scale
            o_ref[:, sl] = y.astype(BF16)

    def head_rms(acc, gain_ref, post_scale):
        gain = gain_ref[...]
        for sl in heads:
            a = acc[:, sl] * r_ref[...]
            ms = jnp.mean(a * a, axis=-1, keepdims=True)
            y = a * lax.rsqrt(ms + EPS) * gain
            if post_scale is not None:
                y = y * post_scale
            o_ref[:, sl] = y.astype(BF16)

    @pl.when(j == 0)
    def _():
        x = x_ref[...]
        ms = jnp.mean(x * x, axis=-1, keepdims=True)
        r_ref[...] = jnp.broadcast_to(lax.rsqrt(ms + EPS), r_ref.shape)
        h_ref[...] = (x * g_ref[...]).astype(BF16)
        cast_slabs(0)
        rotary(project(), None)

    @pl.when(j == 1)
    def _():
        cast_slabs(1)
        rotary(project(), scale)

    @pl.when(j == 4)
    def _():
        cast_slabs(4)
        head_rms(project(), qg_ref, scale * LOG2E)

    @pl.when(j == 5)
    def _():
        cast_slabs(5)
        head_rms(project(), kg_ref, None)

    @pl.when((j == 2) | (j == 3) | (j == 6))
    def _():
        cast_slabs(2, 3, 6)
        acc = project()
        for sl in heads:
            o_ref[:, sl] = (acc[:, sl] * r_ref[...]).astype(BF16)


def _in_proj(x2d, g1, w_bf16, cos2, sin2, qg, kg, weights, cast_sections, *, seq, tm):
    m, d = x2d.shape
    n = w_bf16.shape[1]
    tn = n // 7
    assert tn % HEAD_DIM == 0 and m % tm == 0 and seq % tm == 0
    blocks_per_seq = seq // tm

    def slab(w, cast_at):
        rows, cols = w.shape
        n_slabs = (m // tm) * len(cast_at)
        assert rows % (n_slabs * BF16_SUBLANES) == 0 and cast_at[0] == 0 and list(cast_at) == sorted(cast_at)

        def index(i, j):
            done = sum((j >= s).astype(jnp.int32) for s in cast_at[1:])
            return (i * len(cast_at) + done, 0)
        return pl.BlockSpec((rows // n_slabs, cols), index)

    slabs = [slab(w, cast_at) for w, cast_at in zip(weights, cast_sections)]
    proj, *weights_bf16 = pl.pallas_call(
        functools.partial(_in_proj_kernel, n_heads_per_block=tn // HEAD_DIM,
                          cast_sections=tuple(cast_sections)),
        out_shape=[jax.ShapeDtypeStruct((m, n), BF16)]
                  + [jax.ShapeDtypeStruct(w.shape, BF16) for w in weights],
        grid=(m // tm, 7),
        in_specs=[
            pl.BlockSpec((tm, d), lambda i, j: (i, 0)),
            pl.BlockSpec((1, d), lambda i, j: (0, 0)),
            pl.BlockSpec((d, tn), lambda i, j: (0, j)),
            pl.BlockSpec((tm, HEAD_DIM), lambda i, j: (i % blocks_per_seq, 0)),
            pl.BlockSpec((tm, HEAD_DIM), lambda i, j: (i % blocks_per_seq, 0)),
            pl.BlockSpec((1, HEAD_DIM), lambda i, j: (0, 0)),
            pl.BlockSpec((1, HEAD_DIM), lambda i, j: (0, 0)),
        ] + slabs,
        out_specs=[pl.BlockSpec((tm, tn), lambda i, j: (i, j))] + slabs,
        scratch_shapes=[pltpu.VMEM((tm, d), BF16), pltpu.VMEM((tm, HEAD_DIM), F32)],
        compiler_params=_params(2),
        name="in_proj",
    )(x2d, g1, w_bf16, cos2, sin2, qg, kg, *weights)
    return proj, weights_bf16


def _bias_tile_kernel(g_ref, o_ref):
    row = jnp.broadcast_to(g_ref[0], (SEQ_BLOCK, TOEPLITZ_WIDTH))
    t = pltpu.roll(row, 0, 1, stride=1, stride_axis=0)[:, :WINDOW]
    qc = lax.broadcasted_iota(jnp.int32, (SEQ_BLOCK, WINDOW), 0) // CHUNK
    kc = lax.broadcasted_iota(jnp.int32, (SEQ_BLOCK, WINDOW), 1) // CHUNK
    visible = (kc >= qc) & (kc <= qc + LEFT_CHUNKS)
    o_ref[0] = jnp.where(visible, t * LOG2E, NEG)


def _bias_tile(rel_bias):
    h, rel_size = rel_bias.shape
    assert rel_size == CHUNK + REL_CLIP
    far = rel_bias[:, rel_size - 1:]
    near = rel_bias[:, :1]
    left = LEFT_CHUNKS * CHUNK - REL_CLIP
    g = jnp.concatenate([
        jnp.broadcast_to(far, (h, left)),
        jnp.flip(rel_bias, axis=-1),
        jnp.broadcast_to(near, (h, WINDOW - left - rel_size)),
        jnp.broadcast_to(far, (h, TOEPLITZ_WIDTH - WINDOW)),
    ], axis=-1).astype(F32).reshape(h, 1, TOEPLITZ_WIDTH)
    return pl.pallas_call(
        _bias_tile_kernel,
        out_shape=jax.ShapeDtypeStruct((h, SEQ_BLOCK, WINDOW), F32),
        grid=(h,),
        in_specs=[pl.BlockSpec((1, 1, TOEPLITZ_WIDTH), lambda i: (i, 0, 0))],
        out_specs=pl.BlockSpec((1, SEQ_BLOCK, WINDOW), lambda i: (i, 0, 0)),
        compiler_params=_params(1),
        name="rel_bias_tile",
    )(g)


def _retention_tables():
    h = np.arange(RET_HEADS, dtype=np.float64)
    log_g = np.log1p(-np.exp2(-(5.0 + h)))
    pos = np.arange(SEQ_BLOCK, dtype=np.float64)
    dist = np.abs(pos[:, None] - pos[None, :])
    vis = (pos[None, :] // CHUNK) <= (pos[:, None] // CHUNK)
    decay = np.exp(log_g[:, None, None] * dist) * vis
    q_dec = np.exp(log_g[None, :] * (pos + 1.0)[:, None])
    k_dec = np.exp(log_g[None, :] * (SEQ_BLOCK - 1.0 - pos)[:, None])
    blk_dec = np.exp(log_g * SEQ_BLOCK)
    lanes = lambda t: np.repeat(t, HEAD_DIM, axis=1).astype(np.float32)
    return decay.astype(np.float32), lanes(q_dec), lanes(k_dec), [float(v) for v in blk_dec]


def _retention_kernel(q_ref, k_ref, v_ref, gate_ref, dec_ref, qdec_ref, kdec_ref, gn_ref,
                      o_ref, st_ref, *, blocks_per_seq, blk_dec):
    @pl.when(pl.program_id(0) % blocks_per_seq == 0)
    def _():
        st_ref[...] = jnp.zeros_like(st_ref)

    for h in range(RET_HEADS):
        sl = slice(h * HEAD_DIM, (h + 1) * HEAD_DIM)
        q = q_ref[:, sl]
        k = k_ref[:, sl]
        v = v_ref[:, sl]
        p = (_dot_nt(q, k) * dec_ref[h]).astype(BF16)
        qd = (q.astype(F32) * qdec_ref[:, sl]).astype(BF16)
        state = st_ref[h]
        y = _dot(p, v) + _dot(qd, state.astype(BF16))
        kd = (k.astype(F32) * kdec_ref[:, sl]).astype(BF16)
        st_ref[h] = blk_dec[h] * state + _dot_tn(kd, v)
        mu = jnp.mean(y, axis=-1, keepdims=True)
        yc = y - mu
        var = jnp.mean(yc * yc, axis=-1, keepdims=True)
        yn = yc * lax.rsqrt(var + GN_EPS) * gn_ref[:, sl]
        gate = gate_ref[:, sl].astype(F32)
        swish = gate / (1.0 + jnp.exp(-gate))
        o_ref[:, sl] = (swish * yn).astype(BF16)


def _retention(proj, gn, *, seq):
    m = proj.shape[0]
    width = RET_HEADS * HEAD_DIM
    decay, q_dec, k_dec, blk_dec = _retention_tables()
    blocks_per_seq = seq // SEQ_BLOCK
    section = lambda s: pl.BlockSpec((SEQ_BLOCK, width), lambda i: (i, s))
    whole = lambda shape: pl.BlockSpec(shape, lambda i: (0,) * len(shape))
    return pl.pallas_call(
        functools.partial(_retention_kernel, blocks_per_seq=blocks_per_seq, blk_dec=blk_dec),
        out_shape=jax.ShapeDtypeStruct((m, width), BF16),
        grid=(m // SEQ_BLOCK,),
        in_specs=[section(0), section(1), section(2), section(3),
                  whole(decay.shape), whole(q_dec.shape), whole(k_dec.shape), whole((1, width))],
        out_specs=pl.BlockSpec((SEQ_BLOCK, width), lambda i: (i, 0)),
        scratch_shapes=[pltpu.VMEM((RET_HEADS, HEAD_DIM, HEAD_DIM), F32)],
        compiler_params=_params(1),
        name="retention",
    )(proj, proj, proj, proj, decay, q_dec, k_dec, gn)


def _chunk_attn_kernel(q_ref, kp0_ref, kp1_ref, k_ref, vp0_ref, vp1_ref, v_ref, bias_ref, *rest,
                       steps_per_seq, n_cast):
    w_refs, o_ref, wb_refs = rest[:n_cast], rest[n_cast], rest[n_cast + 1:]
    first_step = pl.program_id(0) % steps_per_seq == 0
    add = lambda a, b: a + b
    lane_tiles = SEQ_BLOCK // HEAD_DIM
    rows = lambda i0, i1: slice(i0 * SEQ_BLOCK, (i1 + 1) * SEQ_BLOCK)

    def key_block(prev_refs, main_ref, b, sl):
        if b < LEFT_BLOCKS:
            return prev_refs[b][:, sl]
        return main_ref[rows(b - LEFT_BLOCKS, b - LEFT_BLOCKS), sl]

    def attend(first_block):
        for w_ref, wb_ref in zip(w_refs, wb_refs):
            wb_ref[...] = w_ref[...].astype(BF16)
        key_blocks = range(first_block, ATT_BLOCKS + LEFT_BLOCKS)
        users = {b: (max(0, b - LEFT_BLOCKS), min(ATT_BLOCKS - 1, b)) for b in key_blocks}
        for h in range(ATT_HEADS):
            sl = slice(h * HEAD_DIM, (h + 1) * HEAD_DIM)
            tiles = {i: [] for i in range(ATT_BLOCKS)}
            for b in key_blocks:
                i0, i1 = users[b]
                s = _dot_nt(q_ref[rows(i0, i1), sl], key_block((kp0_ref, kp1_ref), k_ref, b, sl))
                for i in range(i0, i1 + 1):
                    w = b - i
                    sb = (s[rows(i - i0, i - i0), :]
                          + bias_ref[h, :, w * SEQ_BLOCK:(w + 1) * SEQ_BLOCK])
                    tiles[i] += [sb[:, c * HEAD_DIM:(c + 1) * HEAD_DIM] for c in range(lane_tiles)]
            probs, denom = {}, {}
            for i in range(ATT_BLOCKS):
                mx = jnp.max(functools.reduce(jnp.maximum, tiles[i]), axis=-1, keepdims=True)
                p = [jnp.exp2(tile - mx) for tile in tiles[i]]
                denom[i] = jnp.sum(functools.reduce(add, p), axis=-1, keepdims=True)
                visible = [b for b in key_blocks if users[b][0] <= i <= users[b][1]]
                for n, b in enumerate(visible):
                    probs[i, b] = jnp.concatenate(
                        p[n * lane_tiles:(n + 1) * lane_tiles], axis=1).astype(BF16)
            out = {i: [] for i in range(ATT_BLOCKS)}
            for b in key_blocks:
                i0, i1 = users[b]
                lhs = jnp.concatenate([probs[i, b] for i in range(i0, i1 + 1)], axis=0)
                pv = _dot(lhs, key_block((vp0_ref, vp1_ref), v_ref, b, sl))
                for i in range(i0, i1 + 1):
                    out[i].append(pv[rows(i - i0, i - i0), :])
            for i in range(ATT_BLOCKS):
                o_ref[rows(i, i), sl] = (functools.reduce(add, out[i]) / denom[i]).astype(BF16)

    pl.when(first_step)(functools.partial(attend, LEFT_BLOCKS))
    pl.when(jnp.logical_not(first_step))(functools.partial(attend, 0))


def _chunk_attn(proj, bias_tile, weights, *, seq):
    m = proj.shape[0]
    width = ATT_HEADS * HEAD_DIM
    step_rows = ATT_BLOCKS * SEQ_BLOCK
    assert seq % step_rows == 0 and ATT_BLOCKS >= LEFT_BLOCKS
    steps_per_seq = seq // step_rows
    steps = m // step_rows

    def previous(section, b):
        def index(i):
            seq_start = (i - i % steps_per_seq) * ATT_BLOCKS
            return (jnp.maximum(i * ATT_BLOCKS - (LEFT_BLOCKS - b), seq_start), section)
        return pl.BlockSpec((SEQ_BLOCK, width), index)

    main = lambda section: pl.BlockSpec((step_rows, width), lambda i: (i, section))

    def slab(w):
        rows, cols = w.shape
        assert rows % (steps * BF16_SUBLANES) == 0
        return pl.BlockSpec((rows // steps, cols), lambda i: (i, 0))

    att, *weights_bf16 = pl.pallas_call(
        functools.partial(_chunk_attn_kernel, steps_per_seq=steps_per_seq, n_cast=len(weights)),
        out_shape=[jax.ShapeDtypeStruct((m, width), BF16)]
                  + [jax.ShapeDtypeStruct(w.shape, BF16) for w in weights],
        grid=(steps,),
        in_specs=[main(4)]
                 + [previous(5, b) for b in range(LEFT_BLOCKS)] + [main(5)]
                 + [previous(6, b) for b in range(LEFT_BLOCKS)] + [main(6)]
                 + [pl.BlockSpec(bias_tile.shape, lambda i: (0, 0, 0))]
                 + [slab(w) for w in weights],
        out_specs=[pl.BlockSpec((step_rows, width), lambda i: (i, 0))] + [slab(w) for w in weights],
        compiler_params=_params(1),
        name="chunk_attn",
    )(proj, proj, proj, proj, proj, proj, proj, bias_tile, *weights)
    return att, weights_bf16


def _out_proj_kernel(ret_ref, att_ref, wr_ref, wa_ref, x_ref, o_ref):
    o_ref[...] = x_ref[...] + (_dot(ret_ref[...], wr_ref[...]) + _dot(att_ref[...], wa_ref[...]))


def _out_proj(ret, att, w_bf16, x2d, *, tm, tn):
    m, d = x2d.shape
    kr, ka = ret.shape[1], att.shape[1]
    assert kr == ka and kr + ka == w_bf16.shape[0] and m % tm == 0 and d % tn == 0
    return pl.pallas_call(
        _out_proj_kernel,
        out_shape=jax.ShapeDtypeStruct((m, d), F32),
        grid=(m // tm, d // tn),
        in_specs=[
            pl.BlockSpec((tm, kr), lambda i, j: (i, 0)),
            pl.BlockSpec((tm, ka), lambda i, j: (i, 0)),
            pl.BlockSpec((kr, tn), lambda i, j: (0, j)),
            pl.BlockSpec((ka, tn), lambda i, j: (1, j)),
            pl.BlockSpec((tm, tn), lambda i, j: (i, j)),
        ],
        out_specs=pl.BlockSpec((tm, tn), lambda i, j: (i, j)),
        compiler_params=_params(2),
        name="out_proj",
    )(ret, att, w_bf16, w_bf16, x2d)


def _ffn_kernel(x_ref, g_ref, w1_ref, w2_ref, o_ref, h_ref):
    f = pl.program_id(1)

    @pl.when(f == 0)
    def _():
        x = x_ref[...]
        ms = jnp.mean(x * x, axis=-1, keepdims=True)
        h_ref[...] = (x * lax.rsqrt(ms + EPS) * g_ref[...]).astype(BF16)
        o_ref[...] = x

    a = jnp.maximum(_dot(h_ref[...], w1_ref[...]), 0.0)
    o_ref[...] += _dot((a * a).astype(BF16), w2_ref[...])


def _ffn(x2d, g2, w1_bf16, w2_bf16, *, tm, tf):
    m, d = x2d.shape
    d_ff = w1_bf16.shape[1]
    assert m % tm == 0 and d_ff % tf == 0
    return pl.pallas_call(
        _ffn_kernel,
        out_shape=jax.ShapeDtypeStruct((m, d), F32),
        grid=(m // tm, d_ff // tf),
        in_specs=[
            pl.BlockSpec((tm, d), lambda i, f: (i, 0)),
            pl.BlockSpec((1, d), lambda i, f: (0, 0)),
            pl.BlockSpec((d, tf), lambda i, f: (0, f)),
            pl.BlockSpec((tf, d), lambda i, f: (f, 0)),
        ],
        out_specs=pl.BlockSpec((tm, d), lambda i, f: (i, 0)),
        scratch_shapes=[pltpu.VMEM((tm, d), BF16)],
        compiler_params=_params(2),
        name="ffn",
    )(x2d, g2, w1_bf16, w2_bf16)


def _rotary_tables(seq):
    half = HEAD_DIM // 2
    inv_freq = ROPE_BASE ** (-np.arange(half, dtype=np.float64) / half)
    ang = np.arange(seq, dtype=np.float64)[:, None] * inv_freq[None, :]
    cos, sin = np.cos(ang), np.sin(ang)
    cos2 = np.concatenate([cos, cos], axis=1).astype(np.float32)
    sin2 = np.concatenate([-sin, sin], axis=1).astype(np.float32)
    return cos2, sin2


def kernel(x, norm1_g, w_in, ret_norm_g, q_norm_g, k_norm_g, rel_bias, w_out, norm2_g, w_ff1, w_ff2):
    b, s, d = x.shape
    depth = w_in.shape[0]
    assert s % SEQ_BLOCK == 0 and SEQ_BLOCK % CHUNK == 0
    assert w_in.shape[2] == 7 * RET_HEADS * HEAD_DIM and RET_HEADS == ATT_HEADS
    cos2, sin2 = _rotary_tables(s)
    row = lambda v: v.astype(F32).reshape(1, -1)
    x2d = x.reshape(b * s, d)
    for layer in range(depth):
        proj, (w1, w2) = _in_proj(x2d, row(norm1_g[layer]), w_in[layer].astype(BF16), cos2, sin2,
                                  row(q_norm_g[layer]), row(k_norm_g[layer]),
                                  (w_ff1[layer], w_ff2[layer]), ((0, 1, 4, 5), (0, 2, 3, 6)),
                                  seq=s, tm=1024)
        ret = _retention(proj, row(ret_norm_g[layer]), seq=s)
        att, (wo,) = _chunk_attn(proj, _bias_tile(rel_bias[layer]), (w_out[layer],), seq=s)
        x2d = _out_proj(ret, att, wo, x2d, tm=512, tn=2048)
        x2d = _ffn(x2d, row(norm2_g[layer]), w1, w2, tm=1024, tf=512)
    return x2d.reshape(b, s, d)
```

```python
import functools
import math

import numpy as np
import jax
import jax.numpy as jnp
from jax import lax
from jax.experimental import pallas as pl
from jax.experimental.pallas import tpu as pltpu

CHUNK = 64
LEFT_CHUNKS = 8
REL_CLIP = 128
HEAD_DIM = 128
RET_HEADS = 8
ATT_HEADS = 8
ROPE_BASE = 10000.0
EPS = 1e-6
GN_EPS = 1e-5
NEG = -1e30
LOG2E = math.log2(math.e)

V7X_VMEM_BYTES = 64 * 1024 * 1024
V7X_VMEM_LIMIT_BYTES = 56 * 1024 * 1024

SEQ_BLOCK = 256
LEFT_BLOCKS = LEFT_CHUNKS * CHUNK // SEQ_BLOCK
WINDOW = (LEFT_BLOCKS + 1) * SEQ_BLOCK
ATT_BLOCKS = 4
TOEPLITZ_WIDTH = 1024

F32 = jnp.float32
BF16 = jnp.bfloat16
BF16_SUBLANES = 16


def _params(n_axes, vmem_bytes=V7X_VMEM_LIMIT_BYTES):
    return pltpu.CompilerParams(
        dimension_semantics=("arbitrary",) * n_axes, vmem_limit_bytes=vmem_bytes)


def _dot(a, b):
    return jnp.dot(a, b, preferred_element_type=F32)


def _dot_nt(a, b):
    return lax.dot_general(a, b, (((1,), (1,)), ((), ())), preferred_element_type=F32)


def _dot_tn(a, b):
    return lax.dot_general(a, b, (((0,), (0,)), ((), ())), preferred_element_type=F32)


def _in_proj_kernel(x_ref, g_ref, w_ref, cos_ref, sin_ref, qg_ref, kg_ref, *rest,
                    n_heads_per_block, cast_sections):
    n_cast = len(cast_sections)
    w_refs, o_ref, wb_refs = rest[:n_cast], rest[n_cast], rest[n_cast + 1:2 * n_cast + 1]
    h_ref, r_ref = rest[2 * n_cast + 1:]
    j = pl.program_id(1)

    def cast_slabs(*sections):
        for src_ref, dst_ref, cast_at in zip(w_refs, wb_refs, cast_sections):
            assert all(s in cast_at for s in sections) or not any(s in cast_at for s in sections)
            if sections[0] in cast_at:
                dst_ref[...] = src_ref[...].astype(BF16)

    heads = [slice(h * HEAD_DIM, (h + 1) * HEAD_DIM) for h in range(n_heads_per_block)]
    scale = HEAD_DIM ** -0.5

    def project():
        return _dot(h_ref[...], w_ref[...])

    def rotary(acc, post_scale):
        cos = cos_ref[...]
        sin = sin_ref[...]
        for sl in heads:
            a = acc[:, sl] * r_ref[...]
            y = a * cos + pltpu.roll(a, HEAD_DIM // 2, 1) * sin
            if post_scale is not None:
                y = y * post_scale
            o_ref[:, sl] = y.astype(BF16)

    def head_rms(acc, gain_ref, post_scale):
        gain = gain_ref[...]
        for sl in heads:
            a = acc[:, sl] * r_ref[...]
            ms = jnp.mean(a * a, axis=-1, keepdims=True)
            y = a * lax.rsqrt(ms + EPS) * gain
            if post_scale is not None:
                y = y * post_scale
            o_ref[:, sl] = y.astype(BF16)

    @pl.when(j == 0)
    def _():
        x = x_ref[...]
        ms = jnp.mean(x * x, axis=-1, keepdims=True)
        r_ref[...] = jnp.broadcast_to(lax.rsqrt(ms + EPS), r_ref.shape)
        h_ref[...] = (x * g_ref[...]).astype(BF16)
        cast_slabs(0)
        rotary(project(), None)

    @pl.when(j == 1)
    def _():
        cast_slabs(1)
        rotary(project(), scale)

    @pl.when(j == 4)
    def _():
        cast_slabs(4)
        head_rms(project(), qg_ref, scale * LOG2E)

    @pl.when(j == 5)
    def _():
        cast_slabs(5)
        head_rms(project(), kg_ref, None)

    @pl.when((j == 2) | (j == 3) | (j == 6))
    def _():
        cast_slabs(2, 3, 6)
        acc = project()
        for sl in heads:
            o_ref[:, sl] = (acc[:, sl] * r_ref[...]).astype(BF16)


def _in_proj(x2d, g1, w_bf16, cos2, sin2, qg, kg, weights, cast_sections, *, seq, tm):
    m, d = x2d.shape
    n = w_bf16.shape[1]
    tn = n // 7
    assert tn % HEAD_DIM == 0 and m % tm == 0 and seq % tm == 0
    blocks_per_seq = seq // tm

    def slab(w, cast_at):
        rows, cols = w.shape
        n_slabs = (m // tm) * len(cast_at)
        assert rows % (n_slabs * BF16_SUBLANES) == 0 and cast_at[0] == 0 and list(cast_at) == sorted(cast_at)

        def index(i, j):
            done = sum((j >= s).astype(jnp.int32) for s in cast_at[1:])
            return (i * len(cast_at) + done, 0)
        return pl.BlockSpec((rows // n_slabs, cols), index)

    slabs = [slab(w, cast_at) for w, cast_at in zip(weights, cast_sections)]
    proj, *weights_bf16 = pl.pallas_call(
        functools.partial(_in_proj_kernel, n_heads_per_block=tn // HEAD_DIM,
                          cast_sections=tuple(cast_sections)),
        out_shape=[jax.ShapeDtypeStruct((m, n), BF16)]
                  + [jax.ShapeDtypeStruct(w.shape, BF16) for w in weights],
        grid=(m // tm, 7),
        in_specs=[
            pl.BlockSpec((tm, d), lambda i, j: (i, 0)),
            pl.BlockSpec((1, d), lambda i, j: (0, 0)),
            pl.BlockSpec((d, tn), lambda i, j: (0, j)),
            pl.BlockSpec((tm, HEAD_DIM), lambda i, j: (i % blocks_per_seq, 0)),
            pl.BlockSpec((tm, HEAD_DIM), lambda i, j: (i % blocks_per_seq, 0)),
            pl.BlockSpec((1, HEAD_DIM), lambda i, j: (0, 0)),
            pl.BlockSpec((1, HEAD_DIM), lambda i, j: (0, 0)),
        ] + slabs,
        out_specs=[pl.BlockSpec((tm, tn), lambda i, j: (i, j))] + slabs,
        scratch_shapes=[pltpu.VMEM((tm, d), BF16), pltpu.VMEM((tm, HEAD_DIM), F32)],
        compiler_params=_params(2),
        name="in_proj",
    )(x2d, g1, w_bf16, cos2, sin2, qg, kg, *weights)
    return proj, weights_bf16


def _bias_tile_kernel(g_ref, o_ref):
    row = jnp.broadcast_to(g_ref[0], (SEQ_BLOCK, TOEPLITZ_WIDTH))
    t = pltpu.roll(row, 0, 1, stride=1, stride_axis=0)[:, :WINDOW]
    qc = lax.broadcasted_iota(jnp.int32, (SEQ_BLOCK, WINDOW), 0) // CHUNK
    kc = lax.broadcasted_iota(jnp.int32, (SEQ_BLOCK, WINDOW), 1) // CHUNK
    visible = (kc >= qc) & (kc <= qc + LEFT_CHUNKS)
    o_ref[0] = jnp.where(visible, t * LOG2E, NEG)


def _bias_tile(rel_bias):
    h, rel_size = rel_bias.shape
    assert rel_size == CHUNK + REL_CLIP
    far = rel_bias[:, rel_size - 1:]
    near = rel_bias[:, :1]
    left = LEFT_CHUNKS * CHUNK - REL_CLIP
    g = jnp.concatenate([
        jnp.broadcast_to(far, (h, left)),
        jnp.flip(rel_bias, axis=-1),
        jnp.broadcast_to(near, (h, WINDOW - left - rel_size)),
        jnp.broadcast_to(far, (h, TOEPLITZ_WIDTH - WINDOW)),
    ], axis=-1).astype(F32).reshape(h, 1, TOEPLITZ_WIDTH)
    return pl.pallas_call(
        _bias_tile_kernel,
        out_shape=jax.ShapeDtypeStruct((h, SEQ_BLOCK, WINDOW), F32),
        grid=(h,),
        in_specs=[pl.BlockSpec((1, 1, TOEPLITZ_WIDTH), lambda i: (i, 0, 0))],
        out_specs=pl.BlockSpec((1, SEQ_BLOCK, WINDOW), lambda i: (i, 0, 0)),
        compiler_params=_params(1),
        name="rel_bias_tile",
    )(g)


def _retention_tables():
    h = np.arange(RET_HEADS, dtype=np.float64)
    log_g = np.log1p(-np.exp2(-(5.0 + h)))
    pos = np.arange(SEQ_BLOCK, dtype=np.float64)
    dist = np.abs(pos[:, None] - pos[None, :])
    vis = (pos[None, :] // CHUNK) <= (pos[:, None] // CHUNK)
    decay = np.exp(log_g[:, None, None] * dist) * vis
    q_dec = np.exp(log_g[None, :] * (pos + 1.0)[:, None])
    k_dec = np.exp(log_g[None, :] * (SEQ_BLOCK - 1.0 - pos)[:, None])
    blk_dec = np.exp(log_g * SEQ_BLOCK)
    lanes = lambda t: np.repeat(t, HEAD_DIM, axis=1).astype(np.float32)
    return decay.astype(np.float32), lanes(q_dec), lanes(k_dec), [float(v) for v in blk_dec]


def _retention_kernel(q_ref, k_ref, v_ref, gate_ref, dec_ref, qdec_ref, kdec_ref, gn_ref,
                      o_ref, st_ref, *, blocks_per_seq, blk_dec):
    @pl.when(pl.program_id(0) % blocks_per_seq == 0)
    def _():
        st_ref[...] = jnp.zeros_like(st_ref)

    for h in range(RET_HEADS):
        sl = slice(h * HEAD_DIM, (h + 1) * HEAD_DIM)
        q = q_ref[:, sl]
        k = k_ref[:, sl]
        v = v_ref[:, sl]
        p = (_dot_nt(q, k) * dec_ref[h]).astype(BF16)
        qd = (q.astype(F32) * qdec_ref[:, sl]).astype(BF16)
        state = st_ref[h]
        y = _dot(p, v) + _dot(qd, state.astype(BF16))
        kd = (k.astype(F32) * kdec_ref[:, sl]).astype(BF16)
        st_ref[h] = blk_dec[h] * state + _dot_tn(kd, v)
        mu = jnp.mean(y, axis=-1, keepdims=True)
        yc = y - mu
        var = jnp.mean(yc * yc, axis=-1, keepdims=True)
        yn = yc * lax.rsqrt(var + GN_EPS) * gn_ref[:, sl]
        gate = gate_ref[:, sl].astype(F32)
        swish = gate / (1.0 + jnp.exp(-gate))
        o_ref[:, sl] = (swish * yn).astype(BF16)


def _retention(proj, gn, *, seq):
    m = proj.shape[0]
    width = RET_HEADS * HEAD_DIM
    decay, q_dec, k_dec, blk_dec = _retention_tables()
    blocks_per_seq = seq // SEQ_BLOCK
    section = lambda s: pl.BlockSpec((SEQ_BLOCK, width), lambda i: (i, s))
    whole = lambda shape: pl.BlockSpec(shape, lambda i: (0,) * len(shape))
    return pl.pallas_call(
        functools.partial(_retention_kernel, blocks_per_seq=blocks_per_seq, blk_dec=blk_dec),
        out_shape=jax.ShapeDtypeStruct((m, width), BF16),
        grid=(m // SEQ_BLOCK,),
        in_specs=[section(0), section(1), section(2), section(3),
                  whole(decay.shape), whole(q_dec.shape), whole(k_dec.shape), whole((1, width))],
        out_specs=pl.BlockSpec((SEQ_BLOCK, width), lambda i: (i, 0)),
        scratch_shapes=[pltpu.VMEM((RET_HEADS, HEAD_DIM, HEAD_DIM), F32)],
        compiler_params=_params(1),
        name="retention",
    )(proj, proj, proj, proj, decay, q_dec, k_dec, gn)


def _chunk_attn_kernel(q_ref, kp0_ref, kp1_ref, k_ref, vp0_ref, vp1_ref, v_ref, bias_ref, *rest,
                       steps_per_seq, n_cast):
    w_refs, o_ref, wb_refs = rest[:n_cast], rest[n_cast], rest[n_cast + 1:]
    first_step = pl.program_id(0) % steps_per_seq == 0
    add = lambda a, b: a + b
    lane_tiles = SEQ_BLOCK // HEAD_DIM
    rows = lambda i0, i1: slice(i0 * SEQ_BLOCK, (i1 + 1) * SEQ_BLOCK)

    def key_block(prev_refs, main_ref, b, sl):
        if b < LEFT_BLOCKS:
            return prev_refs[b][:, sl]
        return main_ref[rows(b - LEFT_BLOCKS, b - LEFT_BLOCKS), sl]

    def attend(first_block):
        for w_ref, wb_ref in zip(w_refs, wb_refs):
            wb_ref[...] = w_ref[...].astype(BF16)
        key_blocks = range(first_block, ATT_BLOCKS + LEFT_BLOCKS)
        users = {b: (max(0, b - LEFT_BLOCKS), min(ATT_BLOCKS - 1, b)) for b in key_blocks}
        for h in range(ATT_HEADS):
            sl = slice(h * HEAD_DIM, (h + 1) * HEAD_DIM)
            tiles = {i: [] for i in range(ATT_BLOCKS)}
            for b in key_blocks:
                i0, i1 = users[b]
                s = _dot_nt(q_ref[rows(i0, i1), sl], key_block((kp0_ref, kp1_ref), k_ref, b, sl))
                for i in range(i0, i1 + 1):
                    w = b - i
                    sb = (s[rows(i - i0, i - i0), :]
                          + bias_ref[h, :, w * SEQ_BLOCK:(w + 1) * SEQ_BLOCK])
                    tiles[i] += [sb[:, c * HEAD_DIM:(c + 1) * HEAD_DIM] for c in range(lane_tiles)]
            probs, denom = {}, {}
            for i in range(ATT_BLOCKS):
                mx = jnp.max(functools.reduce(jnp.maximum, tiles[i]), axis=-1, keepdims=True)
                p = [jnp.exp2(tile - mx) for tile in tiles[i]]
                denom[i] = jnp.sum(functools.reduce(add, p), axis=-1, keepdims=True)
                visible = [b for b in key_blocks if users[b][0] <= i <= users[b][1]]
                for n, b in enumerate(visible):
                    probs[i, b] = jnp.concatenate(
                        p[n * lane_tiles:(n + 1) * lane_tiles], axis=1).astype(BF16)
            out = {i: [] for i in range(ATT_BLOCKS)}
            for b in key_blocks:
                i0, i1 = users[b]
                lhs = jnp.concatenate([probs[i, b] for i in range(i0, i1 + 1)], axis=0)
                pv = _dot(lhs, key_block((vp0_ref, vp1_ref), v_ref, b, sl))
                for i in range(i0, i1 + 1):
                    out[i].append(pv[rows(i - i0, i - i0), :])
            for i in range(ATT_BLOCKS):
                o_ref[rows(i, i), sl] = (functools.reduce(add, out[i]) / denom[i]).astype(BF16)

    pl.when(first_step)(functools.partial(attend, LEFT_BLOCKS))
    pl.when(jnp.logical_not(first_step))(functools.partial(attend, 0))


def _chunk_attn(proj, bias_tile, weights, *, seq):
    m = proj.shape[0]
    width = ATT_HEADS * HEAD_DIM
    step_rows = ATT_BLOCKS * SEQ_BLOCK
    assert seq % step_rows == 0 and ATT_BLOCKS >= LEFT_BLOCKS
    steps_per_seq = seq // step_rows
    steps = m // step_rows

    def previous(section, b):
        def index(i):
            seq_start = (i - i % steps_per_seq) * ATT_BLOCKS
            return (jnp.maximum(i * ATT_BLOCKS - (LEFT_BLOCKS - b), seq_start), section)
        return pl.BlockSpec((SEQ_BLOCK, width), index)

    main = lambda section: pl.BlockSpec((step_rows, width), lambda i: (i, section))

    def slab(w):
        rows, cols = w.shape
        assert rows % (steps * BF16_SUBLANES) == 0
        return pl.BlockSpec((rows // steps, cols), lambda i: (i, 0))

    att, *weights_bf16 = pl.pallas_call(
        functools.partial(_chunk_attn_kernel, steps_per_seq=steps_per_seq, n_cast=len(weights)),
        out_shape=[jax.ShapeDtypeStruct((m, width), BF16)]
                  + [jax.ShapeDtypeStruct(w.shape, BF16) for w in weights],
        grid=(steps,),
        in_specs=[main(4)]
                 + [previous(5, b) for b in range(LEFT_BLOCKS)] + [main(5)]
                 + [previous(6, b) for b in range(LEFT_BLOCKS)] + [main(6)]
                 + [pl.BlockSpec(bias_tile.shape, lambda i: (0, 0, 0))]
                 + [slab(w) for w in weights],
        out_specs=[pl.BlockSpec((step_rows, width), lambda i: (i, 0))] + [slab(w) for w in weights],
        compiler_params=_params(1),
        name="chunk_attn",
    )(proj, proj, proj, proj, proj, proj, proj, bias_tile, *weights)
    return att, weights_bf16


def _out_proj_kernel(ret_ref, att_ref, wr_ref, wa_ref, x_ref, o_ref):
    o_ref[...] = x_ref[...] + (_dot(ret_ref[...], wr_ref[...]) + _dot(att_ref[...], wa_ref[...]))


def _out_proj(ret, att, w_bf16, x2d, *, tm, tn):
    m, d = x2d.shape
    kr, ka = ret.shape[1], att.shape[1]
    assert kr == ka and kr + ka == w_bf16.shape[0] and m % tm == 0 and d % tn == 0
    return pl.pallas_call(
        _out_proj_kernel,
        out_shape=jax.ShapeDtypeStruct((m, d), F32),
        grid=(m // tm, d // tn),
        in_specs=[
            pl.BlockSpec((tm, kr), lambda i, j: (i, 0)),
            pl.BlockSpec((tm, ka), lambda i, j: (i, 0)),
            pl.BlockSpec((kr, tn), lambda i, j: (0, j)),
            pl.BlockSpec((ka, tn), lambda i, j: (1, j)),
            pl.BlockSpec((tm, tn), lambda i, j: (i, j)),
        ],
        out_specs=pl.BlockSpec((tm, tn), lambda i, j: (i, j)),
        compiler_params=_params(2),
        name="out_proj",
    )(ret, att, w_bf16, w_bf16, x2d)


def _ffn_kernel(x_ref, g_ref, w1_ref, w2_ref, o_ref, h_ref):
    f = pl.program_id(1)

    @pl.when(f == 0)
    def _():
        x = x_ref[...]
        ms = jnp.mean(x * x, axis=-1, keepdims=True)
        h_ref[...] = (x * lax.rsqrt(ms + EPS) * g_ref[...]).astype(BF16)
        o_ref[...] = x

    a = jnp.maximum(_dot(h_ref[...], w1_ref[...]), 0.0)
    o_ref[...] += _dot((a * a).astype(BF16), w2_ref[...])


def _ffn(x2d, g2, w1_bf16, w2_bf16, *, tm, tf):
    m, d = x2d.shape
    d_ff = w1_bf16.shape[1]
    assert m % tm == 0 and d_ff % tf == 0
    return pl.pallas_call(
        _ffn_kernel,
        out_shape=jax.ShapeDtypeStruct((m, d), F32),
        grid=(m // tm, d_ff // tf),
        in_specs=[
            pl.BlockSpec((tm, d), lambda i, f: (i, 0)),
            pl.BlockSpec((1, d), lambda i, f: (0, 0)),
            pl.BlockSpec((d, tf), lambda i, f: (0, f)),
            pl.BlockSpec((tf, d), lambda i, f: (f, 0)),
        ],
        out_specs=pl.BlockSpec((tm, d), lambda i, f: (i, 0)),
        scratch_shapes=[pltpu.VMEM((tm, d), BF16)],
        compiler_params=_params(2, V7X_VMEM_BYTES - 4 * 1024 * 1024),
        name="ffn",
    )(x2d, g2, w1_bf16, w2_bf16)


def _rotary_tables(seq):
    half = HEAD_DIM // 2
    inv_freq = ROPE_BASE ** (-np.arange(half, dtype=np.float64) / half)
    ang = np.arange(seq, dtype=np.float64)[:, None] * inv_freq[None, :]
    cos, sin = np.cos(ang), np.sin(ang)
    cos2 = np.concatenate([cos, cos], axis=1).astype(np.float32)
    sin2 = np.concatenate([-sin, sin], axis=1).astype(np.float32)
    return cos2, sin2


def kernel(x, norm1_g, w_in, ret_norm_g, q_norm_g, k_norm_g, rel_bias, w_out, norm2_g, w_ff1, w_ff2):
    b, s, d = x.shape
    depth = w_in.shape[0]
    assert s % SEQ_BLOCK == 0 and SEQ_BLOCK % CHUNK == 0
    assert w_in.shape[2] == 7 * RET_HEADS * HEAD_DIM and RET_HEADS == ATT_HEADS
    cos2, sin2 = _rotary_tables(s)
    row = lambda v: v.astype(F32).reshape(1, -1)
    x2d = x.reshape(b * s, d)
    for layer in range(depth):
        proj, (w1, w2) = _in_proj(x2d, row(norm1_g[layer]), w_in[layer].astype(BF16), cos2, sin2,
                                  row(q_norm_g[layer]), row(k_norm_g[layer]),
                                  (w_ff1[layer], w_ff2[layer]), ((0, 1, 4, 5), (0, 2, 3, 6)),
                                  seq=s, tm=1024)
        ret = _retention(proj, row(ret_norm_g[layer]), seq=s)
        att, (wo,) = _chunk_attn(proj, _bias_tile(rel_bias[layer]), (w_out[layer],), seq=s)
        x2d = _out_proj(ret, att, wo, x2d, tm=512, tn=2048)
        x2d = _ffn(x2d, row(norm2_g[layer]), w1, w2, tm=1024, tf=1024)
    return x2d.reshape(b, s, d)
```

```python
import functools
import math

import numpy as np
import jax
import jax.numpy as jnp
from jax import lax
from jax.experimental import pallas as pl
from jax.experimental.pallas import tpu as pltpu

CHUNK = 64
LEFT_CHUNKS = 8
REL_CLIP = 128
HEAD_DIM = 128
RET_HEADS = 8
ATT_HEADS = 8
ROPE_BASE = 10000.0
EPS = 1e-6
GN_EPS = 1e-5
NEG = -1e30
LOG2E = math.log2(math.e)

V7X_VMEM_BYTES = 64 * 1024 * 1024
V7X_VMEM_LIMIT_BYTES = 56 * 1024 * 1024

SEQ_BLOCK = 256
LEFT_BLOCKS = LEFT_CHUNKS * CHUNK // SEQ_BLOCK
WINDOW = (LEFT_BLOCKS + 1) * SEQ_BLOCK
ATT_BLOCKS = 4
FFN_CHUNK = 1024
TOEPLITZ_WIDTH = 1024

F32 = jnp.float32
BF16 = jnp.bfloat16
BF16_SUBLANES = 16


def _params(n_axes, vmem_bytes=V7X_VMEM_LIMIT_BYTES):
    return pltpu.CompilerParams(
        dimension_semantics=("arbitrary",) * n_axes, vmem_limit_bytes=vmem_bytes)


def _dot(a, b):
    return jnp.dot(a, b, preferred_element_type=F32)


def _dot_nt(a, b):
    return lax.dot_general(a, b, (((1,), (1,)), ((), ())), preferred_element_type=F32)


def _dot_tn(a, b):
    return lax.dot_general(a, b, (((0,), (0,)), ((), ())), preferred_element_type=F32)


def _in_proj_kernel(x_ref, g_ref, w_ref, cos_ref, sin_ref, qg_ref, kg_ref, *rest,
                    n_heads_per_block, cast_sections):
    n_cast = len(cast_sections)
    w_refs, o_ref, wb_refs = rest[:n_cast], rest[n_cast], rest[n_cast + 1:2 * n_cast + 1]
    h_ref, r_ref = rest[2 * n_cast + 1:]
    j = pl.program_id(1)

    def cast_slabs(*sections):
        for src_ref, dst_ref, cast_at in zip(w_refs, wb_refs, cast_sections):
            assert all(s in cast_at for s in sections) or not any(s in cast_at for s in sections)
            if sections[0] not in cast_at:
                continue
            if len(dst_ref.shape) == 2:
                dst_ref[...] = src_ref[...].astype(BF16)
            else:
                n_chunks, _, cols = dst_ref.shape
                for c in range(n_chunks):
                    dst_ref[c] = src_ref[:, c * cols:(c + 1) * cols].astype(BF16)

    heads = [slice(h * HEAD_DIM, (h + 1) * HEAD_DIM) for h in range(n_heads_per_block)]
    scale = HEAD_DIM ** -0.5

    def project():
        return _dot(h_ref[...], w_ref[...])

    def rotary(acc, post_scale):
        cos = cos_ref[...]
        sin = sin_ref[...]
        for sl in heads:
            a = acc[:, sl] * r_ref[...]
            y = a * cos + pltpu.roll(a, HEAD_DIM // 2, 1) * sin
            if post_scale is not None:
                y = y * post_scale
            o_ref[:, sl] = y.astype(BF16)

    def head_rms(acc, gain_ref, post_scale):
        gain = gain_ref[...]
        for sl in heads:
            a = acc[:, sl] * r_ref[...]
            ms = jnp.mean(a * a, axis=-1, keepdims=True)
            y = a * lax.rsqrt(ms + EPS) * gain
            if post_scale is not None:
                y = y * post_scale
            o_ref[:, sl] = y.astype(BF16)

    @pl.when(j == 0)
    def _():
        x = x_ref[...]
        ms = jnp.mean(x * x, axis=-1, keepdims=True)
        r_ref[...] = jnp.broadcast_to(lax.rsqrt(ms + EPS), r_ref.shape)
        h_ref[...] = (x * g_ref[...]).astype(BF16)
        cast_slabs(0)
        rotary(project(), None)

    @pl.when(j == 1)
    def _():
        cast_slabs(1)
        rotary(project(), scale)

    @pl.when(j == 4)
    def _():
        cast_slabs(4)
        head_rms(project(), qg_ref, scale * LOG2E)

    @pl.when(j == 5)
    def _():
        cast_slabs(5)
        head_rms(project(), kg_ref, None)

    @pl.when((j == 2) | (j == 3) | (j == 6))
    def _():
        cast_slabs(2, 3, 6)
        acc = project()
        for sl in heads:
            o_ref[:, sl] = (acc[:, sl] * r_ref[...]).astype(BF16)


def _in_proj(x2d, g1, w_sections, cos2, sin2, qg, kg, weights, cast_sections, col_chunks, *, seq, tm):
    m, d = x2d.shape
    n_sections, _, tn = w_sections.shape
    assert n_sections == 7 and tn % HEAD_DIM == 0 and m % tm == 0 and seq % tm == 0
    blocks_per_seq = seq // tm

    def slab(w, cast_at, chunks):
        rows, cols = w.shape
        n_slabs = (m // tm) * len(cast_at)
        assert rows % (n_slabs * BF16_SUBLANES) == 0 and cast_at[0] == 0 and list(cast_at) == sorted(cast_at)
        assert cols % (chunks * HEAD_DIM) == 0

        def index(i, j):
            done = sum((j >= s).astype(jnp.int32) for s in cast_at[1:])
            return i * len(cast_at) + done

        src = pl.BlockSpec((rows // n_slabs, cols), lambda i, j: (index(i, j), 0))
        if chunks == 1:
            return src, src, jax.ShapeDtypeStruct((rows, cols), BF16)
        dst = pl.BlockSpec((chunks, rows // n_slabs, cols // chunks), lambda i, j: (0, index(i, j), 0))
        return src, dst, jax.ShapeDtypeStruct((chunks, rows, cols // chunks), BF16)

    slabs = [slab(w, cast_at, c) for w, cast_at, c in zip(weights, cast_sections, col_chunks)]
    proj, *weights_bf16 = pl.pallas_call(
        functools.partial(_in_proj_kernel, n_heads_per_block=tn // HEAD_DIM,
                          cast_sections=tuple(cast_sections)),
        out_shape=[jax.ShapeDtypeStruct((n_sections, m, tn), BF16)] + [shape for _, _, shape in slabs],
        grid=(m // tm, n_sections),
        in_specs=[
            pl.BlockSpec((tm, d), lambda i, j: (i, 0)),
            pl.BlockSpec((1, d), lambda i, j: (0, 0)),
            pl.BlockSpec((None, d, tn), lambda i, j: (j, 0, 0)),
            pl.BlockSpec((tm, HEAD_DIM), lambda i, j: (i % blocks_per_seq, 0)),
            pl.BlockSpec((tm, HEAD_DIM), lambda i, j: (i % blocks_per_seq, 0)),
            pl.BlockSpec((1, HEAD_DIM), lambda i, j: (0, 0)),
            pl.BlockSpec((1, HEAD_DIM), lambda i, j: (0, 0)),
        ] + [src for src, _, _ in slabs],
        out_specs=[pl.BlockSpec((None, tm, tn), lambda i, j: (j, i, 0))] + [dst for _, dst, _ in slabs],
        scratch_shapes=[pltpu.VMEM((tm, d), BF16), pltpu.VMEM((tm, HEAD_DIM), F32)],
        compiler_params=_params(2),
        name="in_proj",
    )(x2d, g1, w_sections, cos2, sin2, qg, kg, *weights)
    return proj, weights_bf16


def _bias_tile_kernel(g_ref, o_ref):
    row = jnp.broadcast_to(g_ref[0], (SEQ_BLOCK, TOEPLITZ_WIDTH))
    t = pltpu.roll(row, 0, 1, stride=1, stride_axis=0)[:, :WINDOW]
    qc = lax.broadcasted_iota(jnp.int32, (SEQ_BLOCK, WINDOW), 0) // CHUNK
    kc = lax.broadcasted_iota(jnp.int32, (SEQ_BLOCK, WINDOW), 1) // CHUNK
    visible = (kc >= qc) & (kc <= qc + LEFT_CHUNKS)
    o_ref[0] = jnp.where(visible, t * LOG2E, NEG)


def _bias_tile(rel_bias):
    h, rel_size = rel_bias.shape
    assert rel_size == CHUNK + REL_CLIP
    far = rel_bias[:, rel_size - 1:]
    near = rel_bias[:, :1]
    left = LEFT_CHUNKS * CHUNK - REL_CLIP
    g = jnp.concatenate([
        jnp.broadcast_to(far, (h, left)),
        jnp.flip(rel_bias, axis=-1),
        jnp.broadcast_to(near, (h, WINDOW - left - rel_size)),
        jnp.broadcast_to(far, (h, TOEPLITZ_WIDTH - WINDOW)),
    ], axis=-1).astype(F32).reshape(h, 1, TOEPLITZ_WIDTH)
    return pl.pallas_call(
        _bias_tile_kernel,
        out_shape=jax.ShapeDtypeStruct((h, SEQ_BLOCK, WINDOW), F32),
        grid=(h,),
        in_specs=[pl.BlockSpec((1, 1, TOEPLITZ_WIDTH), lambda i: (i, 0, 0))],
        out_specs=pl.BlockSpec((1, SEQ_BLOCK, WINDOW), lambda i: (i, 0, 0)),
        compiler_params=_params(1),
        name="rel_bias_tile",
    )(g)


def _retention_tables():
    h = np.arange(RET_HEADS, dtype=np.float64)
    log_g = np.log1p(-np.exp2(-(5.0 + h)))
    pos = np.arange(SEQ_BLOCK, dtype=np.float64)
    dist = np.abs(pos[:, None] - pos[None, :])
    vis = (pos[None, :] // CHUNK) <= (pos[:, None] // CHUNK)
    decay = np.exp(log_g[:, None, None] * dist) * vis
    q_dec = np.exp(log_g[None, :] * (pos + 1.0)[:, None])
    k_dec = np.exp(log_g[None, :] * (SEQ_BLOCK - 1.0 - pos)[:, None])
    blk_dec = np.exp(log_g * SEQ_BLOCK)
    lanes = lambda t: np.repeat(t, HEAD_DIM, axis=1).astype(np.float32)
    return decay.astype(np.float32), lanes(q_dec), lanes(k_dec), [float(v) for v in blk_dec]


def _retention_kernel(q_ref, k_ref, v_ref, gate_ref, dec_ref, qdec_ref, kdec_ref, gn_ref,
                      o_ref, st_ref, *, blocks_per_seq, blk_dec):
    @pl.when(pl.program_id(0) % blocks_per_seq == 0)
    def _():
        st_ref[...] = jnp.zeros_like(st_ref)

    for h in range(RET_HEADS):
        sl = slice(h * HEAD_DIM, (h + 1) * HEAD_DIM)
        q = q_ref[:, sl]
        k = k_ref[:, sl]
        v = v_ref[:, sl]
        p = (_dot_nt(q, k) * dec_ref[h]).astype(BF16)
        qd = (q.astype(F32) * qdec_ref[:, sl]).astype(BF16)
        state = st_ref[h]
        y = _dot(p, v) + _dot(qd, state.astype(BF16))
        kd = (k.astype(F32) * kdec_ref[:, sl]).astype(BF16)
        st_ref[h] = blk_dec[h] * state + _dot_tn(kd, v)
        mu = jnp.mean(y, axis=-1, keepdims=True)
        yc = y - mu
        var = jnp.mean(yc * yc, axis=-1, keepdims=True)
        yn = yc * lax.rsqrt(var + GN_EPS) * gn_ref[:, sl]
        gate = gate_ref[:, sl].astype(F32)
        swish = gate / (1.0 + jnp.exp(-gate))
        o_ref[:, sl] = (swish * yn).astype(BF16)


def _retention(proj, gn, *, seq):
    _, m, width = proj.shape
    assert width == RET_HEADS * HEAD_DIM
    decay, q_dec, k_dec, blk_dec = _retention_tables()
    blocks_per_seq = seq // SEQ_BLOCK
    section = lambda s: pl.BlockSpec((None, SEQ_BLOCK, width), lambda i: (s, i, 0))
    whole = lambda shape: pl.BlockSpec(shape, lambda i: (0,) * len(shape))
    return pl.pallas_call(
        functools.partial(_retention_kernel, blocks_per_seq=blocks_per_seq, blk_dec=blk_dec),
        out_shape=jax.ShapeDtypeStruct((m, width), BF16),
        grid=(m // SEQ_BLOCK,),
        in_specs=[section(0), section(1), section(2), section(3),
                  whole(decay.shape), whole(q_dec.shape), whole(k_dec.shape), whole((1, width))],
        out_specs=pl.BlockSpec((SEQ_BLOCK, width), lambda i: (i, 0)),
        scratch_shapes=[pltpu.VMEM((RET_HEADS, HEAD_DIM, HEAD_DIM), F32)],
        compiler_params=_params(1),
        name="retention",
    )(proj, proj, proj, proj, decay, q_dec, k_dec, gn)


def _chunk_attn_kernel(q_ref, kp0_ref, kp1_ref, k_ref, vp0_ref, vp1_ref, v_ref, bias_ref, *rest,
                       steps_per_seq, n_cast):
    w_refs, o_ref, wb_refs = rest[:n_cast], rest[n_cast], rest[n_cast + 1:]
    first_step = pl.program_id(0) % steps_per_seq == 0
    add = lambda a, b: a + b
    lane_tiles = SEQ_BLOCK // HEAD_DIM
    rows = lambda i0, i1: slice(i0 * SEQ_BLOCK, (i1 + 1) * SEQ_BLOCK)

    def key_block(prev_refs, main_ref, b, sl):
        if b < LEFT_BLOCKS:
            return prev_refs[b][:, sl]
        return main_ref[rows(b - LEFT_BLOCKS, b - LEFT_BLOCKS), sl]

    def attend(first_block):
        for w_ref, wb_ref in zip(w_refs, wb_refs):
            wb_ref[...] = w_ref[...].astype(BF16)
        key_blocks = range(first_block, ATT_BLOCKS + LEFT_BLOCKS)
        users = {b: (max(0, b - LEFT_BLOCKS), min(ATT_BLOCKS - 1, b)) for b in key_blocks}
        for h in range(ATT_HEADS):
            sl = slice(h * HEAD_DIM, (h + 1) * HEAD_DIM)
            tiles = {i: [] for i in range(ATT_BLOCKS)}
            for b in key_blocks:
                i0, i1 = users[b]
                s = _dot_nt(q_ref[rows(i0, i1), sl], key_block((kp0_ref, kp1_ref), k_ref, b, sl))
                for i in range(i0, i1 + 1):
                    w = b - i
                    sb = (s[rows(i - i0, i - i0), :]
                          + bias_ref[h, :, w * SEQ_BLOCK:(w + 1) * SEQ_BLOCK])
                    tiles[i] += [sb[:, c * HEAD_DIM:(c + 1) * HEAD_DIM] for c in range(lane_tiles)]
            probs, denom = {}, {}
            for i in range(ATT_BLOCKS):
                mx = jnp.max(functools.reduce(jnp.maximum, tiles[i]), axis=-1, keepdims=True)
                p = [jnp.exp2(tile - mx) for tile in tiles[i]]
                denom[i] = jnp.sum(functools.reduce(add, p), axis=-1, keepdims=True)
                visible = [b for b in key_blocks if users[b][0] <= i <= users[b][1]]
                for n, b in enumerate(visible):
                    probs[i, b] = jnp.concatenate(
                        p[n * lane_tiles:(n + 1) * lane_tiles], axis=1).astype(BF16)
            out = {i: [] for i in range(ATT_BLOCKS)}
            for b in key_blocks:
                i0, i1 = users[b]
                lhs = jnp.concatenate([probs[i, b] for i in range(i0, i1 + 1)], axis=0)
                pv = _dot(lhs, key_block((vp0_ref, vp1_ref), v_ref, b, sl))
                for i in range(i0, i1 + 1):
                    out[i].append(pv[rows(i - i0, i - i0), :])
            for i in range(ATT_BLOCKS):
                o_ref[rows(i, i), sl] = (functools.reduce(add, out[i]) / denom[i]).astype(BF16)

    pl.when(first_step)(functools.partial(attend, LEFT_BLOCKS))
    pl.when(jnp.logical_not(first_step))(functools.partial(attend, 0))


def _chunk_attn(proj, bias_tile, weights, *, seq):
    _, m, width = proj.shape
    assert width == ATT_HEADS * HEAD_DIM
    step_rows = ATT_BLOCKS * SEQ_BLOCK
    assert seq % step_rows == 0 and ATT_BLOCKS >= LEFT_BLOCKS
    steps_per_seq = seq // step_rows
    steps = m // step_rows

    def previous(section, b):
        def index(i):
            seq_start = (i - i % steps_per_seq) * ATT_BLOCKS
            return (section, jnp.maximum(i * ATT_BLOCKS - (LEFT_BLOCKS - b), seq_start), 0)
        return pl.BlockSpec((None, SEQ_BLOCK, width), index)

    main = lambda section: pl.BlockSpec((None, step_rows, width), lambda i: (section, i, 0))

    def slab(w):
        rows, cols = w.shape
        assert rows % (steps * BF16_SUBLANES) == 0
        return pl.BlockSpec((rows // steps, cols), lambda i: (i, 0))

    att, *weights_bf16 = pl.pallas_call(
        functools.partial(_chunk_attn_kernel, steps_per_seq=steps_per_seq, n_cast=len(weights)),
        out_shape=[jax.ShapeDtypeStruct((m, width), BF16)]
                  + [jax.ShapeDtypeStruct(w.shape, BF16) for w in weights],
        grid=(steps,),
        in_specs=[main(4)]
                 + [previous(5, b) for b in range(LEFT_BLOCKS)] + [main(5)]
                 + [previous(6, b) for b in range(LEFT_BLOCKS)] + [main(6)]
                 + [pl.BlockSpec(bias_tile.shape, lambda i: (0, 0, 0))]
                 + [slab(w) for w in weights],
        out_specs=[pl.BlockSpec((step_rows, width), lambda i: (i, 0))] + [slab(w) for w in weights],
        compiler_params=_params(1),
        name="chunk_attn",
    )(proj, proj, proj, proj, proj, proj, proj, bias_tile, *weights)
    return att, weights_bf16


def _out_proj_kernel(ret_ref, att_ref, wr_ref, wa_ref, x_ref, o_ref):
    o_ref[...] = x_ref[...] + (_dot(ret_ref[...], wr_ref[...]) + _dot(att_ref[...], wa_ref[...]))


def _out_proj(ret, att, w_bf16, x2d, *, tm, tn):
    m, d = x2d.shape
    kr, ka = ret.shape[1], att.shape[1]
    assert kr == ka and kr + ka == w_bf16.shape[0] and m % tm == 0 and d % tn == 0
    return pl.pallas_call(
        _out_proj_kernel,
        out_shape=jax.ShapeDtypeStruct((m, d), F32),
        grid=(m // tm, d // tn),
        in_specs=[
            pl.BlockSpec((tm, kr), lambda i, j: (i, 0)),
            pl.BlockSpec((tm, ka), lambda i, j: (i, 0)),
            pl.BlockSpec((kr, tn), lambda i, j: (0, j)),
            pl.BlockSpec((ka, tn), lambda i, j: (1, j)),
            pl.BlockSpec((tm, tn), lambda i, j: (i, j)),
        ],
        out_specs=pl.BlockSpec((tm, tn), lambda i, j: (i, j)),
        compiler_params=_params(2),
        name="out_proj",
    )(ret, att, w_bf16, w_bf16, x2d)


def _ffn_kernel(x_ref, g_ref, w1_ref, w2_ref, o_ref, h_ref):
    f = pl.program_id(1)

    @pl.when(f == 0)
    def _():
        x = x_ref[...]
        ms = jnp.mean(x * x, axis=-1, keepdims=True)
        h_ref[...] = (x * lax.rsqrt(ms + EPS) * g_ref[...]).astype(BF16)
        o_ref[...] = x

    a = jnp.maximum(_dot(h_ref[...], w1_ref[...]), 0.0)
    o_ref[...] += _dot((a * a).astype(BF16), w2_ref[...])


def _ffn(x2d, g2, w1_chunks, w2_bf16, *, tm):
    m, d = x2d.shape
    n_chunks, _, tf = w1_chunks.shape
    assert m % tm == 0 and w2_bf16.shape == (n_chunks * tf, d)
    return pl.pallas_call(
        _ffn_kernel,
        out_shape=jax.ShapeDtypeStruct((m, d), F32),
        grid=(m // tm, n_chunks),
        in_specs=[
            pl.BlockSpec((tm, d), lambda i, f: (i, 0)),
            pl.BlockSpec((1, d), lambda i, f: (0, 0)),
            pl.BlockSpec((None, d, tf), lambda i, f: (f, 0, 0)),
            pl.BlockSpec((tf, d), lambda i, f: (f, 0)),
        ],
        out_specs=pl.BlockSpec((tm, d), lambda i, f: (i, 0)),
        scratch_shapes=[pltpu.VMEM((tm, d), BF16)],
        compiler_params=_params(2, V7X_VMEM_BYTES - 4 * 1024 * 1024),
        name="ffn",
    )(x2d, g2, w1_chunks, w2_bf16)


def _rotary_tables(seq):
    half = HEAD_DIM // 2
    inv_freq = ROPE_BASE ** (-np.arange(half, dtype=np.float64) / half)
    ang = np.arange(seq, dtype=np.float64)[:, None] * inv_freq[None, :]
    cos, sin = np.cos(ang), np.sin(ang)
    cos2 = np.concatenate([cos, cos], axis=1).astype(np.float32)
    sin2 = np.concatenate([-sin, sin], axis=1).astype(np.float32)
    return cos2, sin2


def kernel(x, norm1_g, w_in, ret_norm_g, q_norm_g, k_norm_g, rel_bias, w_out, norm2_g, w_ff1, w_ff2):
    b, s, d = x.shape
    depth = w_in.shape[0]
    assert s % SEQ_BLOCK == 0 and SEQ_BLOCK % CHUNK == 0
    assert w_in.shape[2] == 7 * RET_HEADS * HEAD_DIM and RET_HEADS == ATT_HEADS
    cos2, sin2 = _rotary_tables(s)
    row = lambda v: v.astype(F32).reshape(1, -1)
    x2d = x.reshape(b * s, d)
    for layer in range(depth):
        d_ff = w_ff1.shape[2]
        w_sections = w_in[layer].astype(BF16).reshape(d, 7, -1).transpose(1, 0, 2)
        proj, (w1, w2) = _in_proj(x2d, row(norm1_g[layer]), w_sections, cos2, sin2,
                                  row(q_norm_g[layer]), row(k_norm_g[layer]),
                                  (w_ff1[layer], w_ff2[layer]), ((0, 1, 4, 5), (0, 2, 3, 6)),
                                  (d_ff // FFN_CHUNK, 1), seq=s, tm=1024)
        ret = _retention(proj, row(ret_norm_g[layer]), seq=s)
        att, (wo,) = _chunk_attn(proj, _bias_tile(rel_bias[layer]), (w_out[layer],), seq=s)
        x2d = _out_proj(ret, att, wo, x2d, tm=512, tn=2048)
        x2d = _ffn(x2d, row(norm2_g[layer]), w1, w2, tm=1024)
    return x2d.reshape(b, s, d)
```

```python
import functools
import math

import numpy as np
import jax
import jax.numpy as jnp
from jax import lax
from jax.experimental import pallas as pl
from jax.experimental.pallas import tpu as pltpu

CHUNK = 64
LEFT_CHUNKS = 8
REL_CLIP = 128
HEAD_DIM = 128
RET_HEADS = 8
ATT_HEADS = 8
ROPE_BASE = 10000.0
EPS = 1e-6
GN_EPS = 1e-5
NEG = -1e30
LOG2E = math.log2(math.e)

V7X_VMEM_BYTES = 64 * 1024 * 1024
V7X_VMEM_LIMIT_BYTES = 56 * 1024 * 1024

SEQ_BLOCK = 256
LEFT_BLOCKS = LEFT_CHUNKS * CHUNK // SEQ_BLOCK
WINDOW = (LEFT_BLOCKS + 1) * SEQ_BLOCK
ATT_BLOCKS = 4
TOEPLITZ_WIDTH = 1024

F32 = jnp.float32
BF16 = jnp.bfloat16
BF16_SUBLANES = 16


def _params(n_axes, vmem_bytes=V7X_VMEM_LIMIT_BYTES):
    return pltpu.CompilerParams(
        dimension_semantics=("arbitrary",) * n_axes, vmem_limit_bytes=vmem_bytes)


def _dot(a, b):
    return jnp.dot(a, b, preferred_element_type=F32)


def _dot_nt(a, b):
    return lax.dot_general(a, b, (((1,), (1,)), ((), ())), preferred_element_type=F32)


def _dot_tn(a, b):
    return lax.dot_general(a, b, (((0,), (0,)), ((), ())), preferred_element_type=F32)


N_SECTIONS = 7
HEADS_PER_SECTION = 8
IN_PROJ_TILES = 4
MXU_COLS = 256


def _in_proj_kernel(x_ref, g_ref, w_ref, cos_ref, sin_ref, qg_ref, kg_ref, *rest, n_cast):
    w_refs, o_ref, wb_refs = rest[:n_cast], rest[n_cast], rest[n_cast + 1:2 * n_cast + 1]
    h_ref, r_ref = rest[2 * n_cast + 1:]
    j = pl.program_id(1)
    heads_per_tile = o_ref.shape[1] // HEAD_DIM
    heads_per_dot = MXU_COLS // HEAD_DIM
    scale = HEAD_DIM ** -0.5

    def epilogue(section, a):
        if section in (0, 1):
            y = a * cos_ref[...] + pltpu.roll(a, HEAD_DIM // 2, 1) * sin_ref[...]
            return y * scale if section == 1 else y
        if section in (4, 5):
            ms = jnp.mean(a * a, axis=-1, keepdims=True)
            y = a * lax.rsqrt(ms + EPS) * (qg_ref if section == 4 else kg_ref)[...]
            return y * (scale * LOG2E) if section == 4 else y
        return a

    def tile(tile_index):
        if tile_index == 0:
            x = x_ref[...]
            ms = jnp.mean(x * x, axis=-1, keepdims=True)
            r_ref[...] = jnp.broadcast_to(lax.rsqrt(ms + EPS), r_ref.shape)
            h_ref[...] = (x * g_ref[...]).astype(BF16)
        for src_ref, dst_ref in zip(w_refs, wb_refs):
            dst_ref[...] = src_ref[...].astype(BF16)
        for first in range(0, heads_per_tile, heads_per_dot):
            cols = slice(first * HEAD_DIM, (first + heads_per_dot) * HEAD_DIM)
            acc = _dot(h_ref[...], w_ref[:, cols])
            for u in range(heads_per_dot):
                section = (tile_index * heads_per_tile + first + u) // HEADS_PER_SECTION
                a = acc[:, u * HEAD_DIM:(u + 1) * HEAD_DIM] * r_ref[...]
                lanes = slice((first + u) * HEAD_DIM, (first + u + 1) * HEAD_DIM)
                o_ref[:, lanes] = epilogue(section, a).astype(BF16)

    for tile_index in range(IN_PROJ_TILES):
        pl.when(j == tile_index)(functools.partial(tile, tile_index))


def _in_proj(x2d, g1, w_bf16, cos2, sin2, qg, kg, weights, *, seq, tm):
    m, d = x2d.shape
    n = w_bf16.shape[1]
    tn = n // IN_PROJ_TILES
    assert n == N_SECTIONS * HEADS_PER_SECTION * HEAD_DIM and tn % MXU_COLS == 0
    assert m % tm == 0 and seq % tm == 0
    blocks_per_seq = seq // tm
    n_slabs = (m // tm) * IN_PROJ_TILES

    def slab(w):
        rows, cols = w.shape
        assert rows % (n_slabs * BF16_SUBLANES) == 0
        return pl.BlockSpec((rows // n_slabs, cols), lambda i, j: (i * IN_PROJ_TILES + j, 0))

    slabs = [slab(w) for w in weights]
    proj, *weights_bf16 = pl.pallas_call(
        functools.partial(_in_proj_kernel, n_cast=len(weights)),
        out_shape=[jax.ShapeDtypeStruct((m, n), BF16)]
                  + [jax.ShapeDtypeStruct(w.shape, BF16) for w in weights],
        grid=(m // tm, IN_PROJ_TILES),
        in_specs=[
            pl.BlockSpec((tm, d), lambda i, j: (i, 0)),
            pl.BlockSpec((1, d), lambda i, j: (0, 0)),
            pl.BlockSpec((d, tn), lambda i, j: (0, j)),
            pl.BlockSpec((tm, HEAD_DIM), lambda i, j: (i % blocks_per_seq, 0)),
            pl.BlockSpec((tm, HEAD_DIM), lambda i, j: (i % blocks_per_seq, 0)),
            pl.BlockSpec((1, HEAD_DIM), lambda i, j: (0, 0)),
            pl.BlockSpec((1, HEAD_DIM), lambda i, j: (0, 0)),
        ] + slabs,
        out_specs=[pl.BlockSpec((tm, tn), lambda i, j: (i, j))] + slabs,
        scratch_shapes=[pltpu.VMEM((tm, d), BF16), pltpu.VMEM((tm, HEAD_DIM), F32)],
        compiler_params=_params(2, V7X_VMEM_BYTES - 4 * 1024 * 1024),
        name="in_proj",
    )(x2d, g1, w_bf16, cos2, sin2, qg, kg, *weights)
    return proj, weights_bf16


def _bias_tile_kernel(g_ref, o_ref):
    row = jnp.broadcast_to(g_ref[0], (SEQ_BLOCK, TOEPLITZ_WIDTH))
    t = pltpu.roll(row, 0, 1, stride=1, stride_axis=0)[:, :WINDOW]
    qc = lax.broadcasted_iota(jnp.int32, (SEQ_BLOCK, WINDOW), 0) // CHUNK
    kc = lax.broadcasted_iota(jnp.int32, (SEQ_BLOCK, WINDOW), 1) // CHUNK
    visible = (kc >= qc) & (kc <= qc + LEFT_CHUNKS)
    o_ref[0] = jnp.where(visible, t * LOG2E, NEG)


def _bias_tile(rel_bias):
    h, rel_size = rel_bias.shape
    assert rel_size == CHUNK + REL_CLIP
    far = rel_bias[:, rel_size - 1:]
    near = rel_bias[:, :1]
    left = LEFT_CHUNKS * CHUNK - REL_CLIP
    g = jnp.concatenate([
        jnp.broadcast_to(far, (h, left)),
        jnp.flip(rel_bias, axis=-1),
        jnp.broadcast_to(near, (h, WINDOW - left - rel_size)),
        jnp.broadcast_to(far, (h, TOEPLITZ_WIDTH - WINDOW)),
    ], axis=-1).astype(F32).reshape(h, 1, TOEPLITZ_WIDTH)
    return pl.pallas_call(
        _bias_tile_kernel,
        out_shape=jax.ShapeDtypeStruct((h, SEQ_BLOCK, WINDOW), F32),
        grid=(h,),
        in_specs=[pl.BlockSpec((1, 1, TOEPLITZ_WIDTH), lambda i: (i, 0, 0))],
        out_specs=pl.BlockSpec((1, SEQ_BLOCK, WINDOW), lambda i: (i, 0, 0)),
        compiler_params=_params(1),
        name="rel_bias_tile",
    )(g)


def _retention_tables():
    h = np.arange(RET_HEADS, dtype=np.float64)
    log_g = np.log1p(-np.exp2(-(5.0 + h)))
    pos = np.arange(SEQ_BLOCK, dtype=np.float64)
    dist = np.abs(pos[:, None] - pos[None, :])
    vis = (pos[None, :] // CHUNK) <= (pos[:, None] // CHUNK)
    decay = np.exp(log_g[:, None, None] * dist) * vis
    q_dec = np.exp(log_g[None, :] * (pos + 1.0)[:, None])
    k_dec = np.exp(log_g[None, :] * (SEQ_BLOCK - 1.0 - pos)[:, None])
    blk_dec = np.exp(log_g * SEQ_BLOCK)
    lanes = lambda t: np.repeat(t, HEAD_DIM, axis=1).astype(np.float32)
    return decay.astype(np.float32), lanes(q_dec), lanes(k_dec), [float(v) for v in blk_dec]


def _retention_kernel(q_ref, k_ref, v_ref, gate_ref, dec_ref, qdec_ref, kdec_ref, gn_ref,
                      o_ref, st_ref, *, blocks_per_seq, blk_dec):
    @pl.when(pl.program_id(0) % blocks_per_seq == 0)
    def _():
        st_ref[...] = jnp.zeros_like(st_ref)

    for h in range(RET_HEADS):
        sl = slice(h * HEAD_DIM, (h + 1) * HEAD_DIM)
        q = q_ref[:, sl]
        k = k_ref[:, sl]
        v = v_ref[:, sl]
        p = (_dot_nt(q, k) * dec_ref[h]).astype(BF16)
        qd = (q.astype(F32) * qdec_ref[:, sl]).astype(BF16)
        state = st_ref[h]
        y = _dot(p, v) + _dot(qd, state.astype(BF16))
        kd = (k.astype(F32) * kdec_ref[:, sl]).astype(BF16)
        st_ref[h] = blk_dec[h] * state + _dot_tn(kd, v)
        mu = jnp.mean(y, axis=-1, keepdims=True)
        yc = y - mu
        var = jnp.mean(yc * yc, axis=-1, keepdims=True)
        yn = yc * lax.rsqrt(var + GN_EPS) * gn_ref[:, sl]
        gate = gate_ref[:, sl].astype(F32)
        swish = gate / (1.0 + jnp.exp(-gate))
        o_ref[:, sl] = (swish * yn).astype(BF16)


def _retention(proj, gn, *, seq):
    m = proj.shape[0]
    width = RET_HEADS * HEAD_DIM
    decay, q_dec, k_dec, blk_dec = _retention_tables()
    blocks_per_seq = seq // SEQ_BLOCK
    section = lambda s: pl.BlockSpec((SEQ_BLOCK, width), lambda i: (i, s))
    whole = lambda shape: pl.BlockSpec(shape, lambda i: (0,) * len(shape))
    return pl.pallas_call(
        functools.partial(_retention_kernel, blocks_per_seq=blocks_per_seq, blk_dec=blk_dec),
        out_shape=jax.ShapeDtypeStruct((m, width), BF16),
        grid=(m // SEQ_BLOCK,),
        in_specs=[section(0), section(1), section(2), section(3),
                  whole(decay.shape), whole(q_dec.shape), whole(k_dec.shape), whole((1, width))],
        out_specs=pl.BlockSpec((SEQ_BLOCK, width), lambda i: (i, 0)),
        scratch_shapes=[pltpu.VMEM((RET_HEADS, HEAD_DIM, HEAD_DIM), F32)],
        compiler_params=_params(1),
        name="retention",
    )(proj, proj, proj, proj, decay, q_dec, k_dec, gn)


def _chunk_attn_kernel(q_ref, kp0_ref, kp1_ref, k_ref, vp0_ref, vp1_ref, v_ref, bias_ref, *rest,
                       steps_per_seq, n_cast):
    w_refs, o_ref, wb_refs = rest[:n_cast], rest[n_cast], rest[n_cast + 1:]
    first_step = pl.program_id(0) % steps_per_seq == 0
    add = lambda a, b: a + b
    lane_tiles = SEQ_BLOCK // HEAD_DIM
    rows = lambda i0, i1: slice(i0 * SEQ_BLOCK, (i1 + 1) * SEQ_BLOCK)

    def key_block(prev_refs, main_ref, b, sl):
        if b < LEFT_BLOCKS:
            return prev_refs[b][:, sl]
        return main_ref[rows(b - LEFT_BLOCKS, b - LEFT_BLOCKS), sl]

    def attend(first_block):
        for w_ref, wb_ref in zip(w_refs, wb_refs):
            wb_ref[...] = w_ref[...].astype(BF16)
        key_blocks = range(first_block, ATT_BLOCKS + LEFT_BLOCKS)
        users = {b: (max(0, b - LEFT_BLOCKS), min(ATT_BLOCKS - 1, b)) for b in key_blocks}
        for h in range(ATT_HEADS):
            sl = slice(h * HEAD_DIM, (h + 1) * HEAD_DIM)
            tiles = {i: [] for i in range(ATT_BLOCKS)}
            for b in key_blocks:
                i0, i1 = users[b]
                s = _dot_nt(q_ref[rows(i0, i1), sl], key_block((kp0_ref, kp1_ref), k_ref, b, sl))
                for i in range(i0, i1 + 1):
                    w = b - i
                    sb = (s[rows(i - i0, i - i0), :]
                          + bias_ref[h, :, w * SEQ_BLOCK:(w + 1) * SEQ_BLOCK])
                    tiles[i] += [sb[:, c * HEAD_DIM:(c + 1) * HEAD_DIM] for c in range(lane_tiles)]
            probs, denom = {}, {}
            for i in range(ATT_BLOCKS):
                mx = jnp.max(functools.reduce(jnp.maximum, tiles[i]), axis=-1, keepdims=True)
                p = [jnp.exp2(tile - mx) for tile in tiles[i]]
                denom[i] = jnp.sum(functools.reduce(add, p), axis=-1, keepdims=True)
                visible = [b for b in key_blocks if users[b][0] <= i <= users[b][1]]
                for n, b in enumerate(visible):
                    probs[i, b] = jnp.concatenate(
                        p[n * lane_tiles:(n + 1) * lane_tiles], axis=1).astype(BF16)
            out = {i: [] for i in range(ATT_BLOCKS)}
            for b in key_blocks:
                i0, i1 = users[b]
                lhs = jnp.concatenate([probs[i, b] for i in range(i0, i1 + 1)], axis=0)
                pv = _dot(lhs, key_block((vp0_ref, vp1_ref), v_ref, b, sl))
                for i in range(i0, i1 + 1):
                    out[i].append(pv[rows(i - i0, i - i0), :])
            for i in range(ATT_BLOCKS):
                o_ref[rows(i, i), sl] = (functools.reduce(add, out[i]) / denom[i]).astype(BF16)

    pl.when(first_step)(functools.partial(attend, LEFT_BLOCKS))
    pl.when(jnp.logical_not(first_step))(functools.partial(attend, 0))


def _chunk_attn(proj, bias_tile, weights, *, seq):
    m = proj.shape[0]
    width = ATT_HEADS * HEAD_DIM
    step_rows = ATT_BLOCKS * SEQ_BLOCK
    assert seq % step_rows == 0 and ATT_BLOCKS >= LEFT_BLOCKS
    steps_per_seq = seq // step_rows
    steps = m // step_rows

    def previous(section, b):
        def index(i):
            seq_start = (i - i % steps_per_seq) * ATT_BLOCKS
            return (jnp.maximum(i * ATT_BLOCKS - (LEFT_BLOCKS - b), seq_start), section)
        return pl.BlockSpec((SEQ_BLOCK, width), index)

    main = lambda section: pl.BlockSpec((step_rows, width), lambda i: (i, section))

    def slab(w):
        rows, cols = w.shape
        assert rows % (steps * BF16_SUBLANES) == 0
        return pl.BlockSpec((rows // steps, cols), lambda i: (i, 0))

    att, *weights_bf16 = pl.pallas_call(
        functools.partial(_chunk_attn_kernel, steps_per_seq=steps_per_seq, n_cast=len(weights)),
        out_shape=[jax.ShapeDtypeStruct((m, width), BF16)]
                  + [jax.ShapeDtypeStruct(w.shape, BF16) for w in weights],
        grid=(steps,),
        in_specs=[main(4)]
                 + [previous(5, b) for b in range(LEFT_BLOCKS)] + [main(5)]
                 + [previous(6, b) for b in range(LEFT_BLOCKS)] + [main(6)]
                 + [pl.BlockSpec(bias_tile.shape, lambda i: (0, 0, 0))]
                 + [slab(w) for w in weights],
        out_specs=[pl.BlockSpec((step_rows, width), lambda i: (i, 0))] + [slab(w) for w in weights],
        compiler_params=_params(1),
        name="chunk_attn",
    )(proj, proj, proj, proj, proj, proj, proj, bias_tile, *weights)
    return att, weights_bf16


def _out_proj_kernel(ret_ref, att_ref, wr_ref, wa_ref, x_ref, o_ref):
    o_ref[...] = x_ref[...] + (_dot(ret_ref[...], wr_ref[...]) + _dot(att_ref[...], wa_ref[...]))


def _out_proj(ret, att, w_bf16, x2d, *, tm, tn):
    m, d = x2d.shape
    kr, ka = ret.shape[1], att.shape[1]
    assert kr == ka and kr + ka == w_bf16.shape[0] and m % tm == 0 and d % tn == 0
    return pl.pallas_call(
        _out_proj_kernel,
        out_shape=jax.ShapeDtypeStruct((m, d), F32),
        grid=(m // tm, d // tn),
        in_specs=[
            pl.BlockSpec((tm, kr), lambda i, j: (i, 0)),
            pl.BlockSpec((tm, ka), lambda i, j: (i, 0)),
            pl.BlockSpec((kr, tn), lambda i, j: (0, j)),
            pl.BlockSpec((ka, tn), lambda i, j: (1, j)),
            pl.BlockSpec((tm, tn), lambda i, j: (i, j)),
        ],
        out_specs=pl.BlockSpec((tm, tn), lambda i, j: (i, j)),
        compiler_params=_params(2),
        name="out_proj",
    )(ret, att, w_bf16, w_bf16, x2d)


def _ffn_kernel(x_ref, g_ref, w1_ref, w2_ref, o_ref, h_ref):
    f = pl.program_id(1)

    @pl.when(f == 0)
    def _():
        x = x_ref[...]
        ms = jnp.mean(x * x, axis=-1, keepdims=True)
        h_ref[...] = (x * lax.rsqrt(ms + EPS) * g_ref[...]).astype(BF16)
        o_ref[...] = x

    a = jnp.maximum(_dot(h_ref[...], w1_ref[...]), 0.0)
    o_ref[...] += _dot((a * a).astype(BF16), w2_ref[...])


def _ffn(x2d, g2, w1_bf16, w2_bf16, *, tm, tf):
    m, d = x2d.shape
    d_ff = w1_bf16.shape[1]
    assert m % tm == 0 and d_ff % tf == 0
    return pl.pallas_call(
        _ffn_kernel,
        out_shape=jax.ShapeDtypeStruct((m, d), F32),
        grid=(m // tm, d_ff // tf),
        in_specs=[
            pl.BlockSpec((tm, d), lambda i, f: (i, 0)),
            pl.BlockSpec((1, d), lambda i, f: (0, 0)),
            pl.BlockSpec((d, tf), lambda i, f: (0, f)),
            pl.BlockSpec((tf, d), lambda i, f: (f, 0)),
        ],
        out_specs=pl.BlockSpec((tm, d), lambda i, f: (i, 0)),
        scratch_shapes=[pltpu.VMEM((tm, d), BF16)],
        compiler_params=_params(2, V7X_VMEM_BYTES - 4 * 1024 * 1024),
        name="ffn",
    )(x2d, g2, w1_bf16, w2_bf16)


def _rotary_tables(seq):
    half = HEAD_DIM // 2
    inv_freq = ROPE_BASE ** (-np.arange(half, dtype=np.float64) / half)
    ang = np.arange(seq, dtype=np.float64)[:, None] * inv_freq[None, :]
    cos, sin = np.cos(ang), np.sin(ang)
    cos2 = np.concatenate([cos, cos], axis=1).astype(np.float32)
    sin2 = np.concatenate([-sin, sin], axis=1).astype(np.float32)
    return cos2, sin2


def kernel(x, norm1_g, w_in, ret_norm_g, q_norm_g, k_norm_g, rel_bias, w_out, norm2_g, w_ff1, w_ff2):
    b, s, d = x.shape
    depth = w_in.shape[0]
    assert s % SEQ_BLOCK == 0 and SEQ_BLOCK % CHUNK == 0
    assert RET_HEADS == HEADS_PER_SECTION and ATT_HEADS == HEADS_PER_SECTION
    cos2, sin2 = _rotary_tables(s)
    row = lambda v: v.astype(F32).reshape(1, -1)
    x2d = x.reshape(b * s, d)
    for layer in range(depth):
        proj, (w1, w2) = _in_proj(x2d, row(norm1_g[layer]), w_in[layer].astype(BF16), cos2, sin2,
                                  row(q_norm_g[layer]), row(k_norm_g[layer]),
                                  (w_ff1[layer], w_ff2[layer]), seq=s, tm=1024)
        ret = _retention(proj, row(ret_norm_g[layer]), seq=s)
        att, (wo,) = _chunk_attn(proj, _bias_tile(rel_bias[layer]), (w_out[layer],), seq=s)
        x2d = _out_proj(ret, att, wo, x2d, tm=512, tn=2048)
        x2d = _ffn(x2d, row(norm2_g[layer]), w1, w2, tm=1024, tf=1024)
    return x2d.reshape(b, s, d)
```

```python
import functools
import math

import numpy as np
import jax
import jax.numpy as jnp
from jax import lax
from jax.experimental import pallas as pl
from jax.experimental.pallas import tpu as pltpu

CHUNK = 64
LEFT_CHUNKS = 8
REL_CLIP = 128
HEAD_DIM = 128
RET_HEADS = 8
ATT_HEADS = 8
ROPE_BASE = 10000.0
EPS = 1e-6
GN_EPS = 1e-5
NEG = -1e30
LOG2E = math.log2(math.e)

V7X_VMEM_BYTES = 64 * 1024 * 1024
V7X_VMEM_LIMIT_BYTES = 56 * 1024 * 1024

SEQ_BLOCK = 256
LEFT_BLOCKS = LEFT_CHUNKS * CHUNK // SEQ_BLOCK
WINDOW = (LEFT_BLOCKS + 1) * SEQ_BLOCK
ATT_BLOCKS = 4
TOEPLITZ_WIDTH = 1024

F32 = jnp.float32
BF16 = jnp.bfloat16
BF16_SUBLANES = 16


def _params(n_axes, vmem_bytes=V7X_VMEM_LIMIT_BYTES):
    return pltpu.CompilerParams(
        dimension_semantics=("arbitrary",) * n_axes, vmem_limit_bytes=vmem_bytes)


def _dot(a, b):
    return jnp.dot(a, b, preferred_element_type=F32)


def _dot_nt(a, b):
    return lax.dot_general(a, b, (((1,), (1,)), ((), ())), preferred_element_type=F32)


def _dot_tn(a, b):
    return lax.dot_general(a, b, (((0,), (0,)), ((), ())), preferred_element_type=F32)


N_SECTIONS = 7
HEADS_PER_SECTION = 8
IN_PROJ_TILES = 4
MXU_COLS = 256


def _in_proj_kernel(x_ref, g_ref, w_ref, cos_ref, sin_ref, qg_ref, kg_ref, *rest, n_cast):
    w_refs, o_ref, wb_refs = rest[:n_cast], rest[n_cast], rest[n_cast + 1:2 * n_cast + 1]
    h_ref, r_ref = rest[2 * n_cast + 1:]
    j = pl.program_id(1)
    heads_per_tile = o_ref.shape[1] // HEAD_DIM
    heads_per_dot = MXU_COLS // HEAD_DIM
    scale = HEAD_DIM ** -0.5

    def epilogue(section, a):
        if section in (0, 1):
            y = a * cos_ref[...] + pltpu.roll(a, HEAD_DIM // 2, 1) * sin_ref[...]
            return y * scale if section == 1 else y
        if section in (4, 5):
            ms = jnp.mean(a * a, axis=-1, keepdims=True)
            y = a * lax.rsqrt(ms + EPS) * (qg_ref if section == 4 else kg_ref)[...]
            return y * (scale * LOG2E) if section == 4 else y
        return a

    def tile(tile_index):
        if tile_index == 0:
            x = x_ref[...]
            ms = jnp.mean(x * x, axis=-1, keepdims=True)
            r_ref[...] = jnp.broadcast_to(lax.rsqrt(ms + EPS), r_ref.shape)
            h_ref[...] = (x * g_ref[...]).astype(BF16)
        for src_ref, dst_ref in zip(w_refs, wb_refs):
            dst_ref[...] = src_ref[...].astype(BF16)
        for first in range(0, heads_per_tile, heads_per_dot):
            cols = slice(first * HEAD_DIM, (first + heads_per_dot) * HEAD_DIM)
            acc = _dot(h_ref[...], w_ref[:, cols])
            for u in range(heads_per_dot):
                section = (tile_index * heads_per_tile + first + u) // HEADS_PER_SECTION
                a = acc[:, u * HEAD_DIM:(u + 1) * HEAD_DIM] * r_ref[...]
                lanes = slice((first + u) * HEAD_DIM, (first + u + 1) * HEAD_DIM)
                o_ref[:, lanes] = epilogue(section, a).astype(BF16)

    for tile_index in range(IN_PROJ_TILES):
        pl.when(j == tile_index)(functools.partial(tile, tile_index))


def _in_proj(x2d, g1, w_bf16, cos2, sin2, qg, kg, weights, *, seq, tm):
    m, d = x2d.shape
    n = w_bf16.shape[1]
    tn = n // IN_PROJ_TILES
    assert n == N_SECTIONS * HEADS_PER_SECTION * HEAD_DIM and tn % MXU_COLS == 0
    assert m % tm == 0 and seq % tm == 0
    blocks_per_seq = seq // tm
    n_slabs = (m // tm) * IN_PROJ_TILES

    def slab(w):
        rows, cols = w.shape
        assert rows % (n_slabs * BF16_SUBLANES) == 0
        return pl.BlockSpec((rows // n_slabs, cols), lambda i, j: (i * IN_PROJ_TILES + j, 0))

    slabs = [slab(w) for w in weights]
    proj, *weights_bf16 = pl.pallas_call(
        functools.partial(_in_proj_kernel, n_cast=len(weights)),
        out_shape=[jax.ShapeDtypeStruct((m, n), BF16)]
                  + [jax.ShapeDtypeStruct(w.shape, BF16) for w in weights],
        grid=(m // tm, IN_PROJ_TILES),
        in_specs=[
            pl.BlockSpec((tm, d), lambda i, j: (i, 0)),
            pl.BlockSpec((1, d), lambda i, j: (0, 0)),
            pl.BlockSpec((d, tn), lambda i, j: (0, j)),
            pl.BlockSpec((tm, HEAD_DIM), lambda i, j: (i % blocks_per_seq, 0)),
            pl.BlockSpec((tm, HEAD_DIM), lambda i, j: (i % blocks_per_seq, 0)),
            pl.BlockSpec((1, HEAD_DIM), lambda i, j: (0, 0)),
            pl.BlockSpec((1, HEAD_DIM), lambda i, j: (0, 0)),
        ] + slabs,
        out_specs=[pl.BlockSpec((tm, tn), lambda i, j: (i, j))] + slabs,
        scratch_shapes=[pltpu.VMEM((tm, d), BF16), pltpu.VMEM((tm, HEAD_DIM), F32)],
        compiler_params=_params(2, V7X_VMEM_BYTES - 4 * 1024 * 1024),
        name="in_proj",
    )(x2d, g1, w_bf16, cos2, sin2, qg, kg, *weights)
    return proj, weights_bf16


def _bias_tile_kernel(g_ref, o_ref):
    row = jnp.broadcast_to(g_ref[0], (SEQ_BLOCK, TOEPLITZ_WIDTH))
    t = pltpu.roll(row, 0, 1, stride=1, stride_axis=0)[:, :WINDOW]
    qc = lax.broadcasted_iota(jnp.int32, (SEQ_BLOCK, WINDOW), 0) // CHUNK
    kc = lax.broadcasted_iota(jnp.int32, (SEQ_BLOCK, WINDOW), 1) // CHUNK
    visible = (kc >= qc) & (kc <= qc + LEFT_CHUNKS)
    o_ref[0] = jnp.where(visible, t * LOG2E, NEG)


def _bias_tile(rel_bias):
    h, rel_size = rel_bias.shape
    assert rel_size == CHUNK + REL_CLIP
    far = rel_bias[:, rel_size - 1:]
    near = rel_bias[:, :1]
    left = LEFT_CHUNKS * CHUNK - REL_CLIP
    g = jnp.concatenate([
        jnp.broadcast_to(far, (h, left)),
        jnp.flip(rel_bias, axis=-1),
        jnp.broadcast_to(near, (h, WINDOW - left - rel_size)),
        jnp.broadcast_to(far, (h, TOEPLITZ_WIDTH - WINDOW)),
    ], axis=-1).astype(F32).reshape(h, 1, TOEPLITZ_WIDTH)
    return pl.pallas_call(
        _bias_tile_kernel,
        out_shape=jax.ShapeDtypeStruct((h, SEQ_BLOCK, WINDOW), F32),
        grid=(h,),
        in_specs=[pl.BlockSpec((1, 1, TOEPLITZ_WIDTH), lambda i: (i, 0, 0))],
        out_specs=pl.BlockSpec((1, SEQ_BLOCK, WINDOW), lambda i: (i, 0, 0)),
        compiler_params=_params(1),
        name="rel_bias_tile",
    )(g)


def _retention_tables():
    h = np.arange(RET_HEADS, dtype=np.float64)
    log_g = np.log1p(-np.exp2(-(5.0 + h)))
    pos = np.arange(SEQ_BLOCK, dtype=np.float64)
    dist = np.abs(pos[:, None] - pos[None, :])
    vis = (pos[None, :] // CHUNK) <= (pos[:, None] // CHUNK)
    decay = np.exp(log_g[:, None, None] * dist) * vis
    q_dec = np.exp(log_g[None, :] * (pos + 1.0)[:, None])
    k_dec = np.exp(log_g[None, :] * (SEQ_BLOCK - 1.0 - pos)[:, None])
    blk_dec = np.exp(log_g * SEQ_BLOCK)
    lanes = lambda t: np.repeat(t, HEAD_DIM, axis=1).astype(np.float32)
    return decay.astype(np.float32), lanes(q_dec), lanes(k_dec), [float(v) for v in blk_dec]


def _retention_kernel(q_ref, k_ref, v_ref, gate_ref, dec_ref, qdec_ref, kdec_ref, gn_ref,
                      o_ref, st_ref, *, blocks_per_seq, blk_dec):
    @pl.when(pl.program_id(0) % blocks_per_seq == 0)
    def _():
        st_ref[...] = jnp.zeros_like(st_ref)

    for h in range(RET_HEADS):
        sl = slice(h * HEAD_DIM, (h + 1) * HEAD_DIM)
        q = q_ref[:, sl]
        k = k_ref[:, sl]
        v = v_ref[:, sl]
        p = (_dot_nt(q, k) * dec_ref[h]).astype(BF16)
        state = st_ref[h]
        y = _dot(p, v) + _dot(q, state.astype(BF16)) * qdec_ref[:, sl]
        kd = (k.astype(F32) * kdec_ref[:, sl]).astype(BF16)
        st_ref[h] = blk_dec[h] * state + _dot_tn(kd, v)
        mu = jnp.mean(y, axis=-1, keepdims=True)
        yc = y - mu
        var = jnp.mean(yc * yc, axis=-1, keepdims=True)
        yn = yc * lax.rsqrt(var + GN_EPS) * gn_ref[:, sl]
        gate = gate_ref[:, sl].astype(F32)
        swish = gate / (1.0 + jnp.exp(-gate))
        o_ref[:, sl] = (swish * yn).astype(BF16)


def _retention(proj, gn, *, seq):
    m = proj.shape[0]
    width = RET_HEADS * HEAD_DIM
    decay, q_dec, k_dec, blk_dec = _retention_tables()
    blocks_per_seq = seq // SEQ_BLOCK
    section = lambda s: pl.BlockSpec((SEQ_BLOCK, width), lambda i: (i, s))
    whole = lambda shape: pl.BlockSpec(shape, lambda i: (0,) * len(shape))
    return pl.pallas_call(
        functools.partial(_retention_kernel, blocks_per_seq=blocks_per_seq, blk_dec=blk_dec),
        out_shape=jax.ShapeDtypeStruct((m, width), BF16),
        grid=(m // SEQ_BLOCK,),
        in_specs=[section(0), section(1), section(2), section(3),
                  whole(decay.shape), whole(q_dec.shape), whole(k_dec.shape), whole((1, width))],
        out_specs=pl.BlockSpec((SEQ_BLOCK, width), lambda i: (i, 0)),
        scratch_shapes=[pltpu.VMEM((RET_HEADS, HEAD_DIM, HEAD_DIM), F32)],
        compiler_params=_params(1),
        name="retention",
    )(proj, proj, proj, proj, decay, q_dec, k_dec, gn)


def _chunk_attn_kernel(q_ref, kp0_ref, kp1_ref, k_ref, vp0_ref, vp1_ref, v_ref, bias_ref, *rest,
                       steps_per_seq, n_cast):
    w_refs, o_ref, wb_refs = rest[:n_cast], rest[n_cast], rest[n_cast + 1:]
    first_step = pl.program_id(0) % steps_per_seq == 0
    add = lambda a, b: a + b
    lane_tiles = SEQ_BLOCK // HEAD_DIM
    rows = lambda i0, i1: slice(i0 * SEQ_BLOCK, (i1 + 1) * SEQ_BLOCK)

    def key_block(prev_refs, main_ref, b, sl):
        if b < LEFT_BLOCKS:
            return prev_refs[b][:, sl]
        return main_ref[rows(b - LEFT_BLOCKS, b - LEFT_BLOCKS), sl]

    def attend(first_block):
        for w_ref, wb_ref in zip(w_refs, wb_refs):
            wb_ref[...] = w_ref[...].astype(BF16)
        key_blocks = range(first_block, ATT_BLOCKS + LEFT_BLOCKS)
        users = {b: (max(0, b - LEFT_BLOCKS), min(ATT_BLOCKS - 1, b)) for b in key_blocks}
        for h in range(ATT_HEADS):
            sl = slice(h * HEAD_DIM, (h + 1) * HEAD_DIM)
            tiles = {i: [] for i in range(ATT_BLOCKS)}
            for b in key_blocks:
                i0, i1 = users[b]
                s = _dot_nt(q_ref[rows(i0, i1), sl], key_block((kp0_ref, kp1_ref), k_ref, b, sl))
                for i in range(i0, i1 + 1):
                    w = b - i
                    sb = (s[rows(i - i0, i - i0), :]
                          + bias_ref[h, :, w * SEQ_BLOCK:(w + 1) * SEQ_BLOCK])
                    tiles[i] += [sb[:, c * HEAD_DIM:(c + 1) * HEAD_DIM] for c in range(lane_tiles)]
            probs, denom = {}, {}
            for i in range(ATT_BLOCKS):
                mx = jnp.max(functools.reduce(jnp.maximum, tiles[i]), axis=-1, keepdims=True)
                p = [jnp.exp2(tile - mx) for tile in tiles[i]]
                denom[i] = jnp.sum(functools.reduce(add, p), axis=-1, keepdims=True)
                visible = [b for b in key_blocks if users[b][0] <= i <= users[b][1]]
                for n, b in enumerate(visible):
                    probs[i, b] = jnp.concatenate(
                        p[n * lane_tiles:(n + 1) * lane_tiles], axis=1).astype(BF16)
            out = {i: [] for i in range(ATT_BLOCKS)}
            for b in key_blocks:
                i0, i1 = users[b]
                lhs = jnp.concatenate([probs[i, b] for i in range(i0, i1 + 1)], axis=0)
                pv = _dot(lhs, key_block((vp0_ref, vp1_ref), v_ref, b, sl))
                for i in range(i0, i1 + 1):
                    out[i].append(pv[rows(i - i0, i - i0), :])
            for i in range(ATT_BLOCKS):
                o_ref[rows(i, i), sl] = (functools.reduce(add, out[i]) / denom[i]).astype(BF16)

    pl.when(first_step)(functools.partial(attend, LEFT_BLOCKS))
    pl.when(jnp.logical_not(first_step))(functools.partial(attend, 0))


def _chunk_attn(proj, bias_tile, weights, *, seq):
    m = proj.shape[0]
    width = ATT_HEADS * HEAD_DIM
    step_rows = ATT_BLOCKS * SEQ_BLOCK
    assert seq % step_rows == 0 and ATT_BLOCKS >= LEFT_BLOCKS
    steps_per_seq = seq // step_rows
    steps = m // step_rows

    def previous(section, b):
        def index(i):
            seq_start = (i - i % steps_per_seq) * ATT_BLOCKS
            return (jnp.maximum(i * ATT_BLOCKS - (LEFT_BLOCKS - b), seq_start), section)
        return pl.BlockSpec((SEQ_BLOCK, width), index)

    main = lambda section: pl.BlockSpec((step_rows, width), lambda i: (i, section))

    def slab(w):
        rows, cols = w.shape
        assert rows % (steps * BF16_SUBLANES) == 0
        return pl.BlockSpec((rows // steps, cols), lambda i: (i, 0))

    att, *weights_bf16 = pl.pallas_call(
        functools.partial(_chunk_attn_kernel, steps_per_seq=steps_per_seq, n_cast=len(weights)),
        out_shape=[jax.ShapeDtypeStruct((m, width), BF16)]
                  + [jax.ShapeDtypeStruct(w.shape, BF16) for w in weights],
        grid=(steps,),
        in_specs=[main(4)]
                 + [previous(5, b) for b in range(LEFT_BLOCKS)] + [main(5)]
                 + [previous(6, b) for b in range(LEFT_BLOCKS)] + [main(6)]
                 + [pl.BlockSpec(bias_tile.shape, lambda i: (0, 0, 0))]
                 + [slab(w) for w in weights],
        out_specs=[pl.BlockSpec((step_rows, width), lambda i: (i, 0))] + [slab(w) for w in weights],
        compiler_params=_params(1),
        name="chunk_attn",
    )(proj, proj, proj, proj, proj, proj, proj, bias_tile, *weights)
    return att, weights_bf16


def _out_proj_kernel(ret_ref, att_ref, wr_ref, wa_ref, x_ref, o_ref):
    o_ref[...] = x_ref[...] + (_dot(ret_ref[...], wr_ref[...]) + _dot(att_ref[...], wa_ref[...]))


def _out_proj(ret, att, w_bf16, x2d, *, tm, tn):
    m, d = x2d.shape
    kr, ka = ret.shape[1], att.shape[1]
    assert kr == ka and kr + ka == w_bf16.shape[0] and m % tm == 0 and d % tn == 0
    return pl.pallas_call(
        _out_proj_kernel,
        out_shape=jax.ShapeDtypeStruct((m, d), F32),
        grid=(m // tm, d // tn),
        in_specs=[
            pl.BlockSpec((tm, kr), lambda i, j: (i, 0)),
            pl.BlockSpec((tm, ka), lambda i, j: (i, 0)),
            pl.BlockSpec((kr, tn), lambda i, j: (0, j)),
            pl.BlockSpec((ka, tn), lambda i, j: (1, j)),
            pl.BlockSpec((tm, tn), lambda i, j: (i, j)),
        ],
        out_specs=pl.BlockSpec((tm, tn), lambda i, j: (i, j)),
        compiler_params=_params(2),
        name="out_proj",
    )(ret, att, w_bf16, w_bf16, x2d)


def _ffn_kernel(x_ref, g_ref, w1_ref, w2_ref, o_ref, h_ref, r_ref):
    f = pl.program_id(1)

    def chunk():
        a = _dot(h_ref[...], w1_ref[...])
        hidden = []
        for c in range(a.shape[1] // HEAD_DIM):
            act = jnp.maximum(a[:, c * HEAD_DIM:(c + 1) * HEAD_DIM] * r_ref[...], 0.0)
            hidden.append((act * act).astype(BF16))
        o_ref[...] += _dot(jnp.concatenate(hidden, axis=1), w2_ref[...])

    @pl.when(f == 0)
    def _():
        x = x_ref[...]
        ms = jnp.mean(x * x, axis=-1, keepdims=True)
        r_ref[...] = jnp.broadcast_to(lax.rsqrt(ms + EPS), r_ref.shape)
        h_ref[...] = (x * g_ref[...]).astype(BF16)
        o_ref[...] = x
        chunk()

    pl.when(f > 0)(chunk)


def _ffn(x2d, g2, w1_bf16, w2_bf16, *, tm, tf):
    m, d = x2d.shape
    d_ff = w1_bf16.shape[1]
    assert m % tm == 0 and d_ff % tf == 0
    return pl.pallas_call(
        _ffn_kernel,
        out_shape=jax.ShapeDtypeStruct((m, d), F32),
        grid=(m // tm, d_ff // tf),
        in_specs=[
            pl.BlockSpec((tm, d), lambda i, f: (i, 0)),
            pl.BlockSpec((1, d), lambda i, f: (0, 0)),
            pl.BlockSpec((d, tf), lambda i, f: (0, f)),
            pl.BlockSpec((tf, d), lambda i, f: (f, 0)),
        ],
        out_specs=pl.BlockSpec((tm, d), lambda i, f: (i, 0)),
        scratch_shapes=[pltpu.VMEM((tm, d), BF16), pltpu.VMEM((tm, HEAD_DIM), F32)],
        compiler_params=_params(2, V7X_VMEM_BYTES - 4 * 1024 * 1024),
        name="ffn",
    )(x2d, g2, w1_bf16, w2_bf16)


def _rotary_tables(seq):
    half = HEAD_DIM // 2
    inv_freq = ROPE_BASE ** (-np.arange(half, dtype=np.float64) / half)
    ang = np.arange(seq, dtype=np.float64)[:, None] * inv_freq[None, :]
    cos, sin = np.cos(ang), np.sin(ang)
    cos2 = np.concatenate([cos, cos], axis=1).astype(np.float32)
    sin2 = np.concatenate([-sin, sin], axis=1).astype(np.float32)
    return cos2, sin2


def kernel(x, norm1_g, w_in, ret_norm_g, q_norm_g, k_norm_g, rel_bias, w_out, norm2_g, w_ff1, w_ff2):
    b, s, d = x.shape
    depth = w_in.shape[0]
    assert s % SEQ_BLOCK == 0 and SEQ_BLOCK % CHUNK == 0
    assert RET_HEADS == HEADS_PER_SECTION and ATT_HEADS == HEADS_PER_SECTION
    cos2, sin2 = _rotary_tables(s)
    row = lambda v: v.astype(F32).reshape(1, -1)
    x2d = x.reshape(b * s, d)
    for layer in range(depth):
        proj, (w1, w2) = _in_proj(x2d, row(norm1_g[layer]), w_in[layer].astype(BF16), cos2, sin2,
                                  row(q_norm_g[layer]), row(k_norm_g[layer]),
                                  (w_ff1[layer], w_ff2[layer]), seq=s, tm=1024)
        ret = _retention(proj, row(ret_norm_g[layer]), seq=s)
        att, (wo,) = _chunk_attn(proj, _bias_tile(rel_bias[layer]), (w_out[layer],), seq=s)
        x2d = _out_proj(ret, att, wo, x2d, tm=512, tn=2048)
        x2d = _ffn(x2d, row(norm2_g[layer]), w1, w2, tm=1024, tf=1024)
    return x2d.reshape(b, s, d)
```

```python
import functools
import math

import numpy as np
import jax
import jax.numpy as jnp
from jax import lax
from jax.experimental import pallas as pl
from jax.experimental.pallas import tpu as pltpu

CHUNK = 64
LEFT_CHUNKS = 8
REL_CLIP = 128
HEAD_DIM = 128
RET_HEADS = 8
ATT_HEADS = 8
ROPE_BASE = 10000.0
EPS = 1e-6
GN_EPS = 1e-5
NEG = -1e30
LOG2E = math.log2(math.e)

V7X_VMEM_BYTES = 64 * 1024 * 1024
V7X_VMEM_LIMIT_BYTES = 56 * 1024 * 1024

SEQ_BLOCK = 256
LEFT_BLOCKS = LEFT_CHUNKS * CHUNK // SEQ_BLOCK
WINDOW = (LEFT_BLOCKS + 1) * SEQ_BLOCK
ATT_BLOCKS = 4
RET_BLOCKS = 4
TOEPLITZ_WIDTH = 1024

F32 = jnp.float32
BF16 = jnp.bfloat16
BF16_SUBLANES = 16


def _params(n_axes, vmem_bytes=V7X_VMEM_LIMIT_BYTES):
    return pltpu.CompilerParams(
        dimension_semantics=("arbitrary",) * n_axes, vmem_limit_bytes=vmem_bytes)


def _dot(a, b):
    return jnp.dot(a, b, preferred_element_type=F32)


def _dot_nt(a, b):
    return lax.dot_general(a, b, (((1,), (1,)), ((), ())), preferred_element_type=F32)


def _dot_tn(a, b):
    return lax.dot_general(a, b, (((0,), (0,)), ((), ())), preferred_element_type=F32)


N_SECTIONS = 7
HEADS_PER_SECTION = 8
IN_PROJ_TILES = 4
MXU_COLS = 256


def _in_proj_kernel(x_ref, g_ref, w_ref, cos_ref, sin_ref, qg_ref, kg_ref, *rest, n_cast):
    w_refs, o_ref, wb_refs = rest[:n_cast], rest[n_cast], rest[n_cast + 1:2 * n_cast + 1]
    h_ref, r_ref = rest[2 * n_cast + 1:]
    j = pl.program_id(1)
    heads_per_tile = o_ref.shape[1] // HEAD_DIM
    heads_per_dot = MXU_COLS // HEAD_DIM
    scale = HEAD_DIM ** -0.5

    def epilogue(section, a):
        if section in (0, 1):
            y = a * cos_ref[...] + pltpu.roll(a, HEAD_DIM // 2, 1) * sin_ref[...]
            return y * scale if section == 1 else y
        if section in (4, 5):
            ms = jnp.mean(a * a, axis=-1, keepdims=True)
            y = a * lax.rsqrt(ms + EPS) * (qg_ref if section == 4 else kg_ref)[...]
            return y * (scale * LOG2E) if section == 4 else y
        return a

    def tile(tile_index):
        if tile_index == 0:
            x = x_ref[...]
            ms = jnp.mean(x * x, axis=-1, keepdims=True)
            r_ref[...] = jnp.broadcast_to(lax.rsqrt(ms + EPS), r_ref.shape)
            h_ref[...] = (x * g_ref[...]).astype(BF16)
        for src_ref, dst_ref in zip(w_refs, wb_refs):
            dst_ref[...] = src_ref[...].astype(BF16)
        for first in range(0, heads_per_tile, heads_per_dot):
            cols = slice(first * HEAD_DIM, (first + heads_per_dot) * HEAD_DIM)
            acc = _dot(h_ref[...], w_ref[:, cols])
            for u in range(heads_per_dot):
                section = (tile_index * heads_per_tile + first + u) // HEADS_PER_SECTION
                a = acc[:, u * HEAD_DIM:(u + 1) * HEAD_DIM] * r_ref[...]
                lanes = slice((first + u) * HEAD_DIM, (first + u + 1) * HEAD_DIM)
                o_ref[:, lanes] = epilogue(section, a).astype(BF16)

    for tile_index in range(IN_PROJ_TILES):
        pl.when(j == tile_index)(functools.partial(tile, tile_index))


def _in_proj(x2d, g1, w_bf16, cos2, sin2, qg, kg, weights, *, seq, tm):
    m, d = x2d.shape
    n = w_bf16.shape[1]
    tn = n // IN_PROJ_TILES
    assert n == N_SECTIONS * HEADS_PER_SECTION * HEAD_DIM and tn % MXU_COLS == 0
    assert m % tm == 0 and seq % tm == 0
    blocks_per_seq = seq // tm
    n_slabs = (m // tm) * IN_PROJ_TILES

    def slab(w):
        rows, cols = w.shape
        assert rows % (n_slabs * BF16_SUBLANES) == 0
        return pl.BlockSpec((rows // n_slabs, cols), lambda i, j: (i * IN_PROJ_TILES + j, 0))

    slabs = [slab(w) for w in weights]
    proj, *weights_bf16 = pl.pallas_call(
        functools.partial(_in_proj_kernel, n_cast=len(weights)),
        out_shape=[jax.ShapeDtypeStruct((m, n), BF16)]
                  + [jax.ShapeDtypeStruct(w.shape, BF16) for w in weights],
        grid=(m // tm, IN_PROJ_TILES),
        in_specs=[
            pl.BlockSpec((tm, d), lambda i, j: (i, 0)),
            pl.BlockSpec((1, d), lambda i, j: (0, 0)),
            pl.BlockSpec((d, tn), lambda i, j: (0, j)),
            pl.BlockSpec((tm, HEAD_DIM), lambda i, j: (i % blocks_per_seq, 0)),
            pl.BlockSpec((tm, HEAD_DIM), lambda i, j: (i % blocks_per_seq, 0)),
            pl.BlockSpec((1, HEAD_DIM), lambda i, j: (0, 0)),
            pl.BlockSpec((1, HEAD_DIM), lambda i, j: (0, 0)),
        ] + slabs,
        out_specs=[pl.BlockSpec((tm, tn), lambda i, j: (i, j))] + slabs,
        scratch_shapes=[pltpu.VMEM((tm, d), BF16), pltpu.VMEM((tm, HEAD_DIM), F32)],
        compiler_params=_params(2, V7X_VMEM_BYTES - 4 * 1024 * 1024),
        name="in_proj",
    )(x2d, g1, w_bf16, cos2, sin2, qg, kg, *weights)
    return proj, weights_bf16


def _bias_tile_kernel(g_ref, o_ref):
    row = jnp.broadcast_to(g_ref[0], (SEQ_BLOCK, TOEPLITZ_WIDTH))
    t = pltpu.roll(row, 0, 1, stride=1, stride_axis=0)[:, :WINDOW]
    qc = lax.broadcasted_iota(jnp.int32, (SEQ_BLOCK, WINDOW), 0) // CHUNK
    kc = lax.broadcasted_iota(jnp.int32, (SEQ_BLOCK, WINDOW), 1) // CHUNK
    visible = (kc >= qc) & (kc <= qc + LEFT_CHUNKS)
    o_ref[0] = jnp.where(visible, t * LOG2E, NEG)


def _bias_tile(rel_bias):
    h, rel_size = rel_bias.shape
    assert rel_size == CHUNK + REL_CLIP
    far = rel_bias[:, rel_size - 1:]
    near = rel_bias[:, :1]
    left = LEFT_CHUNKS * CHUNK - REL_CLIP
    g = jnp.concatenate([
        jnp.broadcast_to(far, (h, left)),
        jnp.flip(rel_bias, axis=-1),
        jnp.broadcast_to(near, (h, WINDOW - left - rel_size)),
        jnp.broadcast_to(far, (h, TOEPLITZ_WIDTH - WINDOW)),
    ], axis=-1).astype(F32).reshape(h, 1, TOEPLITZ_WIDTH)
    return pl.pallas_call(
        _bias_tile_kernel,
        out_shape=jax.ShapeDtypeStruct((h, SEQ_BLOCK, WINDOW), F32),
        grid=(h,),
        in_specs=[pl.BlockSpec((1, 1, TOEPLITZ_WIDTH), lambda i: (i, 0, 0))],
        out_specs=pl.BlockSpec((1, SEQ_BLOCK, WINDOW), lambda i: (i, 0, 0)),
        compiler_params=_params(1),
        name="rel_bias_tile",
    )(g)


def _retention_tables():
    h = np.arange(RET_HEADS, dtype=np.float64)
    log_g = np.log1p(-np.exp2(-(5.0 + h)))
    pos = np.arange(SEQ_BLOCK, dtype=np.float64)
    dist = np.abs(pos[:, None] - pos[None, :])
    vis = (pos[None, :] // CHUNK) <= (pos[:, None] // CHUNK)
    decay = np.exp(log_g[:, None, None] * dist) * vis
    q_dec = np.exp(log_g[None, :] * (pos + 1.0)[:, None])
    k_dec = np.exp(log_g[None, :] * (SEQ_BLOCK - 1.0 - pos)[:, None])
    blk_dec = np.exp(log_g * SEQ_BLOCK)
    lanes = lambda t: np.repeat(t, HEAD_DIM, axis=1).astype(np.float32)
    return decay.astype(np.float32), lanes(q_dec), lanes(k_dec), [float(v) for v in blk_dec]


def _retention_kernel(q_ref, k_ref, v_ref, gate_ref, dec_ref, qdec_ref, kdec_ref, gn_ref,
                      o_ref, st_ref, *, blocks_per_seq, blk_dec):
    @pl.when(pl.program_id(0) % blocks_per_seq == 0)
    def _():
        st_ref[...] = jnp.zeros_like(st_ref)

    for h in range(RET_HEADS):
        sl = slice(h * HEAD_DIM, (h + 1) * HEAD_DIM)
        state = st_ref[h]
        for blk in range(RET_BLOCKS):
            rows = slice(blk * SEQ_BLOCK, (blk + 1) * SEQ_BLOCK)
            q = q_ref[rows, sl]
            k = k_ref[rows, sl]
            v = v_ref[rows, sl]
            p = (_dot_nt(q, k) * dec_ref[h]).astype(BF16)
            y = _dot(p, v) + _dot(q, state.astype(BF16)) * qdec_ref[:, sl]
            kd = (k.astype(F32) * kdec_ref[:, sl]).astype(BF16)
            state = blk_dec[h] * state + _dot_tn(kd, v)
            mu = jnp.mean(y, axis=-1, keepdims=True)
            yc = y - mu
            var = jnp.mean(yc * yc, axis=-1, keepdims=True)
            yn = yc * lax.rsqrt(var + GN_EPS) * gn_ref[:, sl]
            gate = gate_ref[rows, sl].astype(F32)
            swish = gate / (1.0 + jnp.exp(-gate))
            o_ref[rows, sl] = (swish * yn).astype(BF16)
        st_ref[h] = state


def _retention(proj, gn, *, seq):
    m = proj.shape[0]
    width = RET_HEADS * HEAD_DIM
    decay, q_dec, k_dec, blk_dec = _retention_tables()
    step_rows = RET_BLOCKS * SEQ_BLOCK
    assert seq % step_rows == 0
    blocks_per_seq = seq // step_rows
    section = lambda s: pl.BlockSpec((step_rows, width), lambda i: (i, s))
    whole = lambda shape: pl.BlockSpec(shape, lambda i: (0,) * len(shape))
    return pl.pallas_call(
        functools.partial(_retention_kernel, blocks_per_seq=blocks_per_seq, blk_dec=blk_dec),
        out_shape=jax.ShapeDtypeStruct((m, width), BF16),
        grid=(m // step_rows,),
        in_specs=[section(0), section(1), section(2), section(3),
                  whole(decay.shape), whole(q_dec.shape), whole(k_dec.shape), whole((1, width))],
        out_specs=pl.BlockSpec((step_rows, width), lambda i: (i, 0)),
        scratch_shapes=[pltpu.VMEM((RET_HEADS, HEAD_DIM, HEAD_DIM), F32)],
        compiler_params=_params(1),
        name="retention",
    )(proj, proj, proj, proj, decay, q_dec, k_dec, gn)


def _chunk_attn_kernel(q_ref, kp0_ref, kp1_ref, k_ref, vp0_ref, vp1_ref, v_ref, bias_ref, *rest,
                       steps_per_seq, n_cast):
    w_refs, o_ref, wb_refs = rest[:n_cast], rest[n_cast], rest[n_cast + 1:]
    first_step = pl.program_id(0) % steps_per_seq == 0
    add = lambda a, b: a + b
    lane_tiles = SEQ_BLOCK // HEAD_DIM
    rows = lambda i0, i1: slice(i0 * SEQ_BLOCK, (i1 + 1) * SEQ_BLOCK)

    def key_block(prev_refs, main_ref, b, sl):
        if b < LEFT_BLOCKS:
            return prev_refs[b][:, sl]
        return main_ref[rows(b - LEFT_BLOCKS, b - LEFT_BLOCKS), sl]

    def attend(first_block):
        for w_ref, wb_ref in zip(w_refs, wb_refs):
            wb_ref[...] = w_ref[...].astype(BF16)
        key_blocks = range(first_block, ATT_BLOCKS + LEFT_BLOCKS)
        users = {b: (max(0, b - LEFT_BLOCKS), min(ATT_BLOCKS - 1, b)) for b in key_blocks}
        for h in range(ATT_HEADS):
            sl = slice(h * HEAD_DIM, (h + 1) * HEAD_DIM)
            tiles = {i: [] for i in range(ATT_BLOCKS)}
            for b in key_blocks:
                i0, i1 = users[b]
                s = _dot_nt(q_ref[rows(i0, i1), sl], key_block((kp0_ref, kp1_ref), k_ref, b, sl))
                for i in range(i0, i1 + 1):
                    w = b - i
                    sb = (s[rows(i - i0, i - i0), :]
                          + bias_ref[h, :, w * SEQ_BLOCK:(w + 1) * SEQ_BLOCK])
                    tiles[i] += [sb[:, c * HEAD_DIM:(c + 1) * HEAD_DIM] for c in range(lane_tiles)]
            probs, denom = {}, {}
            for i in range(ATT_BLOCKS):
                mx = jnp.max(functools.reduce(jnp.maximum, tiles[i]), axis=-1, keepdims=True)
                p = [jnp.exp2(tile - mx) for tile in tiles[i]]
                denom[i] = jnp.sum(functools.reduce(add, p), axis=-1, keepdims=True)
                visible = [b for b in key_blocks if users[b][0] <= i <= users[b][1]]
                for n, b in enumerate(visible):
                    probs[i, b] = jnp.concatenate(
                        p[n * lane_tiles:(n + 1) * lane_tiles], axis=1).astype(BF16)
            out = {i: [] for i in range(ATT_BLOCKS)}
            for b in key_blocks:
                i0, i1 = users[b]
                lhs = jnp.concatenate([probs[i, b] for i in range(i0, i1 + 1)], axis=0)
                pv = _dot(lhs, key_block((vp0_ref, vp1_ref), v_ref, b, sl))
                for i in range(i0, i1 + 1):
                    out[i].append(pv[rows(i - i0, i - i0), :])
            for i in range(ATT_BLOCKS):
                o_ref[rows(i, i), sl] = (functools.reduce(add, out[i]) / denom[i]).astype(BF16)

    pl.when(first_step)(functools.partial(attend, LEFT_BLOCKS))
    pl.when(jnp.logical_not(first_step))(functools.partial(attend, 0))


def _chunk_attn(proj, bias_tile, weights, *, seq):
    m = proj.shape[0]
    width = ATT_HEADS * HEAD_DIM
    step_rows = ATT_BLOCKS * SEQ_BLOCK
    assert seq % step_rows == 0 and ATT_BLOCKS >= LEFT_BLOCKS
    steps_per_seq = seq // step_rows
    steps = m // step_rows

    def previous(section, b):
        def index(i):
            seq_start = (i - i % steps_per_seq) * ATT_BLOCKS
            return (jnp.maximum(i * ATT_BLOCKS - (LEFT_BLOCKS - b), seq_start), section)
        return pl.BlockSpec((SEQ_BLOCK, width), index)

    main = lambda section: pl.BlockSpec((step_rows, width), lambda i: (i, section))

    def slab(w):
        rows, cols = w.shape
        assert rows % (steps * BF16_SUBLANES) == 0
        return pl.BlockSpec((rows // steps, cols), lambda i: (i, 0))

    att, *weights_bf16 = pl.pallas_call(
        functools.partial(_chunk_attn_kernel, steps_per_seq=steps_per_seq, n_cast=len(weights)),
        out_shape=[jax.ShapeDtypeStruct((m, width), BF16)]
                  + [jax.ShapeDtypeStruct(w.shape, BF16) for w in weights],
        grid=(steps,),
        in_specs=[main(4)]
                 + [previous(5, b) for b in range(LEFT_BLOCKS)] + [main(5)]
                 + [previous(6, b) for b in range(LEFT_BLOCKS)] + [main(6)]
                 + [pl.BlockSpec(bias_tile.shape, lambda i: (0, 0, 0))]
                 + [slab(w) for w in weights],
        out_specs=[pl.BlockSpec((step_rows, width), lambda i: (i, 0))] + [slab(w) for w in weights],
        compiler_params=_params(1),
        name="chunk_attn",
    )(proj, proj, proj, proj, proj, proj, proj, bias_tile, *weights)
    return att, weights_bf16


def _out_proj_kernel(ret_ref, att_ref, wr_ref, wa_ref, x_ref, o_ref):
    o_ref[...] = x_ref[...] + (_dot(ret_ref[...], wr_ref[...]) + _dot(att_ref[...], wa_ref[...]))


def _out_proj(ret, att, w_bf16, x2d, *, tm, tn):
    m, d = x2d.shape
    kr, ka = ret.shape[1], att.shape[1]
    assert kr == ka and kr + ka == w_bf16.shape[0] and m % tm == 0 and d % tn == 0
    return pl.pallas_call(
        _out_proj_kernel,
        out_shape=jax.ShapeDtypeStruct((m, d), F32),
        grid=(m // tm, d // tn),
        in_specs=[
            pl.BlockSpec((tm, kr), lambda i, j: (i, 0)),
            pl.BlockSpec((tm, ka), lambda i, j: (i, 0)),
            pl.BlockSpec((kr, tn), lambda i, j: (0, j)),
            pl.BlockSpec((ka, tn), lambda i, j: (1, j)),
            pl.BlockSpec((tm, tn), lambda i, j: (i, j)),
        ],
        out_specs=pl.BlockSpec((tm, tn), lambda i, j: (i, j)),
        compiler_params=_params(2),
        name="out_proj",
    )(ret, att, w_bf16, w_bf16, x2d)


def _ffn_kernel(x_ref, g_ref, w1_ref, w2_ref, o_ref, h_ref, r_ref):
    f = pl.program_id(1)

    def chunk():
        a = _dot(h_ref[...], w1_ref[...])
        hidden = []
        for c in range(a.shape[1] // HEAD_DIM):
            act = jnp.maximum(a[:, c * HEAD_DIM:(c + 1) * HEAD_DIM] * r_ref[...], 0.0)
            hidden.append((act * act).astype(BF16))
        o_ref[...] += _dot(jnp.concatenate(hidden, axis=1), w2_ref[...])

    @pl.when(f == 0)
    def _():
        x = x_ref[...]
        ms = jnp.mean(x * x, axis=-1, keepdims=True)
        r_ref[...] = jnp.broadcast_to(lax.rsqrt(ms + EPS), r_ref.shape)
        h_ref[...] = (x * g_ref[...]).astype(BF16)
        o_ref[...] = x
        chunk()

    pl.when(f > 0)(chunk)


def _ffn(x2d, g2, w1_bf16, w2_bf16, *, tm, tf):
    m, d = x2d.shape
    d_ff = w1_bf16.shape[1]
    assert m % tm == 0 and d_ff % tf == 0
    return pl.pallas_call(
        _ffn_kernel,
        out_shape=jax.ShapeDtypeStruct((m, d), F32),
        grid=(m // tm, d_ff // tf),
        in_specs=[
            pl.BlockSpec((tm, d), lambda i, f: (i, 0)),
            pl.BlockSpec((1, d), lambda i, f: (0, 0)),
            pl.BlockSpec((d, tf), lambda i, f: (0, f)),
            pl.BlockSpec((tf, d), lambda i, f: (f, 0)),
        ],
        out_specs=pl.BlockSpec((tm, d), lambda i, f: (i, 0)),
        scratch_shapes=[pltpu.VMEM((tm, d), BF16), pltpu.VMEM((tm, HEAD_DIM), F32)],
        compiler_params=_params(2, V7X_VMEM_BYTES - 4 * 1024 * 1024),
        name="ffn",
    )(x2d, g2, w1_bf16, w2_bf16)


def _rotary_tables(seq):
    half = HEAD_DIM // 2
    inv_freq = ROPE_BASE ** (-np.arange(half, dtype=np.float64) / half)
    ang = np.arange(seq, dtype=np.float64)[:, None] * inv_freq[None, :]
    cos, sin = np.cos(ang), np.sin(ang)
    cos2 = np.concatenate([cos, cos], axis=1).astype(np.float32)
    sin2 = np.concatenate([-sin, sin], axis=1).astype(np.float32)
    return cos2, sin2


def kernel(x, norm1_g, w_in, ret_norm_g, q_norm_g, k_norm_g, rel_bias, w_out, norm2_g, w_ff1, w_ff2):
    b, s, d = x.shape
    depth = w_in.shape[0]
    assert s % SEQ_BLOCK == 0 and SEQ_BLOCK % CHUNK == 0
    assert RET_HEADS == HEADS_PER_SECTION and ATT_HEADS == HEADS_PER_SECTION
    cos2, sin2 = _rotary_tables(s)
    row = lambda v: v.astype(F32).reshape(1, -1)
    x2d = x.reshape(b * s, d)
    for layer in range(depth):
        proj, (w1, w2) = _in_proj(x2d, row(norm1_g[layer]), w_in[layer].astype(BF16), cos2, sin2,
                                  row(q_norm_g[layer]), row(k_norm_g[layer]),
                                  (w_ff1[layer], w_ff2[layer]), seq=s, tm=1024)
        ret = _retention(proj, row(ret_norm_g[layer]), seq=s)
        att, (wo,) = _chunk_attn(proj, _bias_tile(rel_bias[layer]), (w_out[layer],), seq=s)
        x2d = _out_proj(ret, att, wo, x2d, tm=512, tn=2048)
        x2d = _ffn(x2d, row(norm2_g[layer]), w1, w2, tm=1024, tf=1024)
    return x2d.reshape(b, s, d)
```

```python
import functools
import math

import numpy as np
import jax
import jax.numpy as jnp
from jax import lax
from jax.experimental import pallas as pl
from jax.experimental.pallas import tpu as pltpu

CHUNK = 64
LEFT_CHUNKS = 8
REL_CLIP = 128
HEAD_DIM = 128
RET_HEADS = 8
ATT_HEADS = 8
ROPE_BASE = 10000.0
EPS = 1e-6
GN_EPS = 1e-5
NEG = -1e30
LOG2E = math.log2(math.e)

V7X_VMEM_BYTES = 64 * 1024 * 1024
V7X_VMEM_LIMIT_BYTES = 56 * 1024 * 1024

SEQ_BLOCK = 256
LEFT_BLOCKS = LEFT_CHUNKS * CHUNK // SEQ_BLOCK
WINDOW = (LEFT_BLOCKS + 1) * SEQ_BLOCK
ATT_BLOCKS = 4
RET_BLOCKS = 4
TOEPLITZ_WIDTH = 1024

F32 = jnp.float32
BF16 = jnp.bfloat16
BF16_SUBLANES = 16


def _params(n_axes, vmem_bytes=V7X_VMEM_LIMIT_BYTES):
    return pltpu.CompilerParams(
        dimension_semantics=("arbitrary",) * n_axes, vmem_limit_bytes=vmem_bytes)


def _dot(a, b):
    return jnp.dot(a, b, preferred_element_type=F32)


def _dot_nt(a, b):
    return lax.dot_general(a, b, (((1,), (1,)), ((), ())), preferred_element_type=F32)


def _dot_tn(a, b):
    return lax.dot_general(a, b, (((0,), (0,)), ((), ())), preferred_element_type=F32)


N_SECTIONS = 7
HEADS_PER_SECTION = 8
IN_PROJ_TILES = 4
MXU_COLS = 256


def _in_proj_kernel(x_ref, g_ref, w_ref, cos_ref, sin_ref, qg_ref, kg_ref, *rest, n_cast):
    w_refs, o_ref, wb_refs = rest[:n_cast], rest[n_cast], rest[n_cast + 1:2 * n_cast + 1]
    h_ref, r_ref = rest[2 * n_cast + 1:]
    j = pl.program_id(1)
    heads_per_tile = o_ref.shape[1] // HEAD_DIM
    heads_per_dot = MXU_COLS // HEAD_DIM
    scale = HEAD_DIM ** -0.5

    def epilogue(section, a):
        if section in (0, 1):
            y = a * cos_ref[...] + pltpu.roll(a, HEAD_DIM // 2, 1) * sin_ref[...]
            return y * scale if section == 1 else y
        if section in (4, 5):
            ms = jnp.mean(a * a, axis=-1, keepdims=True)
            y = a * lax.rsqrt(ms + EPS) * (qg_ref if section == 4 else kg_ref)[...]
            return y * (scale * LOG2E) if section == 4 else y
        return a

    def tile(tile_index):
        if tile_index == 0:
            x = x_ref[...]
            ms = jnp.mean(x * x, axis=-1, keepdims=True)
            r_ref[...] = jnp.broadcast_to(lax.rsqrt(ms + EPS), r_ref.shape)
            h_ref[...] = (x * g_ref[...]).astype(BF16)
        for src_ref, dst_ref in zip(w_refs, wb_refs):
            dst_ref[...] = src_ref[...].astype(BF16)
        for first in range(0, heads_per_tile, heads_per_dot):
            cols = slice(first * HEAD_DIM, (first + heads_per_dot) * HEAD_DIM)
            acc = _dot(h_ref[...], w_ref[:, cols])
            for u in range(heads_per_dot):
                section = (tile_index * heads_per_tile + first + u) // HEADS_PER_SECTION
                a = acc[:, u * HEAD_DIM:(u + 1) * HEAD_DIM] * r_ref[...]
                lanes = slice((first + u) * HEAD_DIM, (first + u + 1) * HEAD_DIM)
                o_ref[:, lanes] = epilogue(section, a).astype(BF16)

    for tile_index in range(IN_PROJ_TILES):
        pl.when(j == tile_index)(functools.partial(tile, tile_index))


def _in_proj(x2d, g1, w_bf16, cos2, sin2, qg, kg, weights, *, seq, tm):
    m, d = x2d.shape
    n = w_bf16.shape[1]
    tn = n // IN_PROJ_TILES
    assert n == N_SECTIONS * HEADS_PER_SECTION * HEAD_DIM and tn % MXU_COLS == 0
    assert m % tm == 0 and seq % tm == 0
    blocks_per_seq = seq // tm
    n_slabs = (m // tm) * IN_PROJ_TILES

    def slab(w):
        rows, cols = w.shape
        assert rows % (n_slabs * BF16_SUBLANES) == 0
        return pl.BlockSpec((rows // n_slabs, cols), lambda i, j: (i * IN_PROJ_TILES + j, 0))

    slabs = [slab(w) for w in weights]
    proj, *weights_bf16 = pl.pallas_call(
        functools.partial(_in_proj_kernel, n_cast=len(weights)),
        out_shape=[jax.ShapeDtypeStruct((m, n), BF16)]
                  + [jax.ShapeDtypeStruct(w.shape, BF16) for w in weights],
        grid=(m // tm, IN_PROJ_TILES),
        in_specs=[
            pl.BlockSpec((tm, d), lambda i, j: (i, 0)),
            pl.BlockSpec((1, d), lambda i, j: (0, 0)),
            pl.BlockSpec((d, tn), lambda i, j: (0, j)),
            pl.BlockSpec((tm, HEAD_DIM), lambda i, j: (i % blocks_per_seq, 0)),
            pl.BlockSpec((tm, HEAD_DIM), lambda i, j: (i % blocks_per_seq, 0)),
            pl.BlockSpec((1, HEAD_DIM), lambda i, j: (0, 0)),
            pl.BlockSpec((1, HEAD_DIM), lambda i, j: (0, 0)),
        ] + slabs,
        out_specs=[pl.BlockSpec((tm, tn), lambda i, j: (i, j))] + slabs,
        scratch_shapes=[pltpu.VMEM((tm, d), BF16), pltpu.VMEM((tm, HEAD_DIM), F32)],
        compiler_params=_params(2, V7X_VMEM_BYTES - 4 * 1024 * 1024),
        name="in_proj",
    )(x2d, g1, w_bf16, cos2, sin2, qg, kg, *weights)
    return proj, weights_bf16


def _bias_tile_kernel(g_ref, o_ref):
    row = jnp.broadcast_to(g_ref[0], (SEQ_BLOCK, TOEPLITZ_WIDTH))
    t = pltpu.roll(row, 0, 1, stride=1, stride_axis=0)[:, :WINDOW]
    qc = lax.broadcasted_iota(jnp.int32, (SEQ_BLOCK, WINDOW), 0) // CHUNK
    kc = lax.broadcasted_iota(jnp.int32, (SEQ_BLOCK, WINDOW), 1) // CHUNK
    visible = (kc >= qc) & (kc <= qc + LEFT_CHUNKS)
    o_ref[0] = jnp.where(visible, t * LOG2E, NEG)


def _bias_tile(rel_bias):
    h, rel_size = rel_bias.shape
    assert rel_size == CHUNK + REL_CLIP
    far = rel_bias[:, rel_size - 1:]
    near = rel_bias[:, :1]
    left = LEFT_CHUNKS * CHUNK - REL_CLIP
    g = jnp.concatenate([
        jnp.broadcast_to(far, (h, left)),
        jnp.flip(rel_bias, axis=-1),
        jnp.broadcast_to(near, (h, WINDOW - left - rel_size)),
        jnp.broadcast_to(far, (h, TOEPLITZ_WIDTH - WINDOW)),
    ], axis=-1).astype(F32).reshape(h, 1, TOEPLITZ_WIDTH)
    return pl.pallas_call(
        _bias_tile_kernel,
        out_shape=jax.ShapeDtypeStruct((h, SEQ_BLOCK, WINDOW), F32),
        grid=(h,),
        in_specs=[pl.BlockSpec((1, 1, TOEPLITZ_WIDTH), lambda i: (i, 0, 0))],
        out_specs=pl.BlockSpec((1, SEQ_BLOCK, WINDOW), lambda i: (i, 0, 0)),
        compiler_params=_params(1),
        name="rel_bias_tile",
    )(g)


def _retention_tables():
    h = np.arange(RET_HEADS, dtype=np.float64)
    log_g = np.log1p(-np.exp2(-(5.0 + h)))
    pos = np.arange(SEQ_BLOCK, dtype=np.float64)
    dist = np.abs(pos[:, None] - pos[None, :])
    vis = (pos[None, :] // CHUNK) <= (pos[:, None] // CHUNK)
    decay = np.exp(log_g[:, None, None] * dist) * vis
    q_dec = np.exp(log_g[None, :] * (pos + 1.0)[:, None])
    k_dec = np.exp(log_g[None, :] * (SEQ_BLOCK - 1.0 - pos)[:, None])
    blk_dec = np.exp(log_g * SEQ_BLOCK)
    lanes = lambda t: np.repeat(t, HEAD_DIM, axis=1).astype(np.float32)
    return decay.astype(np.float32), lanes(q_dec), lanes(k_dec), [float(v) for v in blk_dec]


def _retention_kernel(q_ref, k_ref, v_ref, gate_ref, dec_ref, qdec_ref, kdec_ref, gn_ref,
                      o_ref, st_ref, *, blocks_per_seq, blk_dec):
    @pl.when(pl.program_id(0) % blocks_per_seq == 0)
    def _():
        st_ref[...] = jnp.zeros_like(st_ref)

    for h in range(RET_HEADS):
        sl = slice(h * HEAD_DIM, (h + 1) * HEAD_DIM)
        state = st_ref[h]
        for blk in range(RET_BLOCKS):
            rows = slice(blk * SEQ_BLOCK, (blk + 1) * SEQ_BLOCK)
            q = q_ref[rows, sl]
            k = k_ref[rows, sl]
            v = v_ref[rows, sl]
            p = (_dot_nt(q, k) * dec_ref[h]).astype(BF16)
            y = _dot(p, v) + _dot(q, state.astype(BF16)) * qdec_ref[:, sl]
            kd = (k.astype(F32) * kdec_ref[:, sl]).astype(BF16)
            state = blk_dec[h] * state + _dot_tn(kd, v)
            mu = jnp.mean(y, axis=-1, keepdims=True)
            yc = y - mu
            var = jnp.mean(yc * yc, axis=-1, keepdims=True)
            yn = yc * lax.rsqrt(var + GN_EPS) * gn_ref[:, sl]
            gate = gate_ref[rows, sl].astype(F32)
            swish = gate / (1.0 + jnp.exp(-gate))
            o_ref[rows, sl] = (swish * yn).astype(BF16)
        st_ref[h] = state


def _retention(proj, gn, *, seq):
    m = proj.shape[0]
    width = RET_HEADS * HEAD_DIM
    decay, q_dec, k_dec, blk_dec = _retention_tables()
    step_rows = RET_BLOCKS * SEQ_BLOCK
    assert seq % step_rows == 0
    blocks_per_seq = seq // step_rows
    section = lambda s: pl.BlockSpec((step_rows, width), lambda i: (i, s))
    whole = lambda shape: pl.BlockSpec(shape, lambda i: (0,) * len(shape))
    return pl.pallas_call(
        functools.partial(_retention_kernel, blocks_per_seq=blocks_per_seq, blk_dec=blk_dec),
        out_shape=jax.ShapeDtypeStruct((m, width), BF16),
        grid=(m // step_rows,),
        in_specs=[section(0), section(1), section(2), section(3),
                  whole(decay.shape), whole(q_dec.shape), whole(k_dec.shape), whole((1, width))],
        out_specs=pl.BlockSpec((step_rows, width), lambda i: (i, 0)),
        scratch_shapes=[pltpu.VMEM((RET_HEADS, HEAD_DIM, HEAD_DIM), F32)],
        compiler_params=_params(1),
        name="retention",
    )(proj, proj, proj, proj, decay, q_dec, k_dec, gn)


def _chunk_attn_kernel(q_ref, kp0_ref, kp1_ref, k_ref, vp0_ref, vp1_ref, v_ref, bias_ref, *rest,
                       steps_per_seq, n_cast):
    w_refs, o_ref, wb_refs = rest[:n_cast], rest[n_cast], rest[n_cast + 1:]
    first_step = pl.program_id(0) % steps_per_seq == 0
    add = lambda a, b: a + b
    lane_tiles = SEQ_BLOCK // HEAD_DIM
    rows = lambda i0, i1: slice(i0 * SEQ_BLOCK, (i1 + 1) * SEQ_BLOCK)

    def key_block(prev_refs, main_ref, b, sl):
        if b < LEFT_BLOCKS:
            return prev_refs[b][:, sl]
        return main_ref[rows(b - LEFT_BLOCKS, b - LEFT_BLOCKS), sl]

    def attend(first_block):
        for w_ref, wb_ref in zip(w_refs, wb_refs):
            wb_ref[...] = w_ref[...].astype(BF16)
        key_blocks = range(first_block, ATT_BLOCKS + LEFT_BLOCKS)
        users = {b: (max(0, b - LEFT_BLOCKS), min(ATT_BLOCKS - 1, b)) for b in key_blocks}
        for h in range(ATT_HEADS):
            sl = slice(h * HEAD_DIM, (h + 1) * HEAD_DIM)
            tiles = {i: [] for i in range(ATT_BLOCKS)}
            for b in key_blocks:
                i0, i1 = users[b]
                s = _dot_nt(q_ref[rows(i0, i1), sl], key_block((kp0_ref, kp1_ref), k_ref, b, sl))
                for i in range(i0, i1 + 1):
                    w = b - i
                    sb = (s[rows(i - i0, i - i0), :]
                          + bias_ref[h, :, w * SEQ_BLOCK:(w + 1) * SEQ_BLOCK])
                    tiles[i] += [sb[:, c * HEAD_DIM:(c + 1) * HEAD_DIM] for c in range(lane_tiles)]
            probs, denom = {}, {}
            for i in range(ATT_BLOCKS):
                mx = jnp.max(functools.reduce(jnp.maximum, tiles[i]), axis=-1, keepdims=True)
                p = [jnp.exp2(tile - mx) for tile in tiles[i]]
                denom[i] = jnp.sum(functools.reduce(add, p), axis=-1, keepdims=True)
                visible = [b for b in key_blocks if users[b][0] <= i <= users[b][1]]
                for n, b in enumerate(visible):
                    probs[i, b] = jnp.concatenate(
                        p[n * lane_tiles:(n + 1) * lane_tiles], axis=1).astype(BF16)
            out = {i: [] for i in range(ATT_BLOCKS)}
            for b in key_blocks:
                i0, i1 = users[b]
                lhs = jnp.concatenate([probs[i, b] for i in range(i0, i1 + 1)], axis=0)
                pv = _dot(lhs, key_block((vp0_ref, vp1_ref), v_ref, b, sl))
                for i in range(i0, i1 + 1):
                    out[i].append(pv[rows(i - i0, i - i0), :])
            for i in range(ATT_BLOCKS):
                o_ref[rows(i, i), sl] = (functools.reduce(add, out[i]) / denom[i]).astype(BF16)

    pl.when(first_step)(functools.partial(attend, LEFT_BLOCKS))
    pl.when(jnp.logical_not(first_step))(functools.partial(attend, 0))


def _chunk_attn(proj, bias_tile, weights, *, seq):
    m = proj.shape[0]
    width = ATT_HEADS * HEAD_DIM
    step_rows = ATT_BLOCKS * SEQ_BLOCK
    assert seq % step_rows == 0 and ATT_BLOCKS >= LEFT_BLOCKS
    steps_per_seq = seq // step_rows
    steps = m // step_rows

    def previous(section, b):
        def index(i):
            seq_start = (i - i % steps_per_seq) * ATT_BLOCKS
            return (jnp.maximum(i * ATT_BLOCKS - (LEFT_BLOCKS - b), seq_start), section)
        return pl.BlockSpec((SEQ_BLOCK, width), index)

    main = lambda section: pl.BlockSpec((step_rows, width), lambda i: (i, section))

    def slab(w):
        rows, cols = w.shape
        assert rows % (steps * BF16_SUBLANES) == 0
        return pl.BlockSpec((rows // steps, cols), lambda i: (i, 0))

    att, *weights_bf16 = pl.pallas_call(
        functools.partial(_chunk_attn_kernel, steps_per_seq=steps_per_seq, n_cast=len(weights)),
        out_shape=[jax.ShapeDtypeStruct((m, width), BF16)]
                  + [jax.ShapeDtypeStruct(w.shape, BF16) for w in weights],
        grid=(steps,),
        in_specs=[main(4)]
                 + [previous(5, b) for b in range(LEFT_BLOCKS)] + [main(5)]
                 + [previous(6, b) for b in range(LEFT_BLOCKS)] + [main(6)]
                 + [pl.BlockSpec(bias_tile.shape, lambda i: (0, 0, 0))]
                 + [slab(w) for w in weights],
        out_specs=[pl.BlockSpec((step_rows, width), lambda i: (i, 0))] + [slab(w) for w in weights],
        compiler_params=_params(1),
        name="chunk_attn",
    )(proj, proj, proj, proj, proj, proj, proj, bias_tile, *weights)
    return att, weights_bf16


def _out_proj_kernel(ret_ref, att_ref, wr_ref, wa_ref, x_ref, o_ref):
    o_ref[...] = x_ref[...] + (_dot(ret_ref[...], wr_ref[...]) + _dot(att_ref[...], wa_ref[...]))


def _out_proj(ret, att, w_bf16, x2d, *, tm, tn):
    m, d = x2d.shape
    kr, ka = ret.shape[1], att.shape[1]
    assert kr == ka and kr + ka == w_bf16.shape[0] and m % tm == 0 and d % tn == 0
    return pl.pallas_call(
        _out_proj_kernel,
        out_shape=jax.ShapeDtypeStruct((m, d), F32),
        grid=(m // tm, d // tn),
        in_specs=[
            pl.BlockSpec((tm, kr), lambda i, j: (i, 0)),
            pl.BlockSpec((tm, ka), lambda i, j: (i, 0)),
            pl.BlockSpec((kr, tn), lambda i, j: (0, j)),
            pl.BlockSpec((ka, tn), lambda i, j: (1, j)),
            pl.BlockSpec((tm, tn), lambda i, j: (i, j)),
        ],
        out_specs=pl.BlockSpec((tm, tn), lambda i, j: (i, j)),
        compiler_params=_params(2),
        name="out_proj",
    )(ret, att, w_bf16, w_bf16, x2d)


def _ffn_kernel(x_ref, g_ref, w1_ref, w2_ref, o_ref, h_ref, r_ref):
    f = pl.program_id(1)

    def chunk():
        a = _dot(h_ref[...], w1_ref[...])
        hidden = []
        for c in range(a.shape[1] // HEAD_DIM):
            act = jnp.maximum(a[:, c * HEAD_DIM:(c + 1) * HEAD_DIM] * r_ref[...], 0.0)
            hidden.append((act * act).astype(BF16))
        o_ref[...] += _dot(jnp.concatenate(hidden, axis=1), w2_ref[...])

    @pl.when(f == 0)
    def _():
        x = x_ref[...]
        ms = jnp.mean(x * x, axis=-1, keepdims=True)
        r_ref[...] = jnp.broadcast_to(lax.rsqrt(ms + EPS), r_ref.shape)
        h_ref[...] = (x * g_ref[...]).astype(BF16)
        o_ref[...] = x
        chunk()

    pl.when(f > 0)(chunk)


def _ffn(x2d, g2, w1_bf16, w2_bf16, *, tm, tf):
    m, d = x2d.shape
    d_ff = w1_bf16.shape[1]
    assert m % tm == 0 and d_ff % tf == 0
    return pl.pallas_call(
        _ffn_kernel,
        out_shape=jax.ShapeDtypeStruct((m, d), F32),
        grid=(m // tm, d_ff // tf),
        in_specs=[
            pl.BlockSpec((tm, d), lambda i, f: (i, 0)),
            pl.BlockSpec((1, d), lambda i, f: (0, 0)),
            pl.BlockSpec((d, tf), lambda i, f: (0, f)),
            pl.BlockSpec((tf, d), lambda i, f: (f, 0)),
        ],
        out_specs=pl.BlockSpec((tm, d), lambda i, f: (i, 0)),
        scratch_shapes=[pltpu.VMEM((tm, d), BF16), pltpu.VMEM((tm, HEAD_DIM), F32)],
        compiler_params=_params(2, V7X_VMEM_BYTES - 4 * 1024 * 1024),
        name="ffn",
    )(x2d, g2, w1_bf16, w2_bf16)


def _rotary_tables(seq):
    half = HEAD_DIM // 2
    inv_freq = ROPE_BASE ** (-np.arange(half, dtype=np.float64) / half)
    ang = np.arange(seq, dtype=np.float64)[:, None] * inv_freq[None, :]
    cos, sin = np.cos(ang), np.sin(ang)
    cos2 = np.concatenate([cos, cos], axis=1).astype(np.float32)
    sin2 = np.concatenate([-sin, sin], axis=1).astype(np.float32)
    return cos2, sin2


def kernel(x, norm1_g, w_in, ret_norm_g, q_norm_g, k_norm_g, rel_bias, w_out, norm2_g, w_ff1, w_ff2):
    b, s, d = x.shape
    depth = w_in.shape[0]
    assert s % SEQ_BLOCK == 0 and SEQ_BLOCK % CHUNK == 0
    assert RET_HEADS == HEADS_PER_SECTION and ATT_HEADS == HEADS_PER_SECTION
    cos2, sin2 = _rotary_tables(s)
    row = lambda v: v.astype(F32).reshape(1, -1)
    x2d = x.reshape(b * s, d)
    for layer in range(depth):
        proj, _ = _in_proj(x2d, row(norm1_g[layer]), w_in[layer].astype(BF16), cos2, sin2,
                           row(q_norm_g[layer]), row(k_norm_g[layer]), (), seq=s, tm=1024)
        w1, w2 = w_ff1[layer].astype(BF16), w_ff2[layer].astype(BF16)
        ret = _retention(proj, row(ret_norm_g[layer]), seq=s)
        att, (wo,) = _chunk_attn(proj, _bias_tile(rel_bias[layer]), (w_out[layer],), seq=s)
        x2d = _out_proj(ret, att, wo, x2d, tm=512, tn=2048)
        x2d = _ffn(x2d, row(norm2_g[layer]), w1, w2, tm=1024, tf=1024)
    return x2d.reshape(b, s, d)
```

```python
import functools
import math

import numpy as np
import jax
import jax.numpy as jnp
from jax import lax
from jax.experimental import pallas as pl
from jax.experimental.pallas import tpu as pltpu

CHUNK = 64
LEFT_CHUNKS = 8
REL_CLIP = 128
HEAD_DIM = 128
RET_HEADS = 8
ATT_HEADS = 8
ROPE_BASE = 10000.0
EPS = 1e-6
GN_EPS = 1e-5
NEG = -1e30
LOG2E = math.log2(math.e)

V7X_VMEM_BYTES = 64 * 1024 * 1024
V7X_VMEM_LIMIT_BYTES = V7X_VMEM_BYTES - 8 * 1024 * 1024
V7X_VMEM_HIGH_LIMIT_BYTES = V7X_VMEM_BYTES - 4 * 1024 * 1024

SEQ_BLOCK = 256
LEFT_BLOCKS = LEFT_CHUNKS * CHUNK // SEQ_BLOCK
WINDOW = (LEFT_BLOCKS + 1) * SEQ_BLOCK
ATT_BLOCKS = 4
RET_BLOCKS = 4
TOEPLITZ_WIDTH = 1024
IN_PROJ_ROWS = 1024
OUT_PROJ_ROWS = 512
FFN_ROWS = 1024
FFN_CHUNK = 1024

F32 = jnp.float32
BF16 = jnp.bfloat16
BF16_SUBLANES = 16


def _params(n_axes, vmem_bytes=V7X_VMEM_LIMIT_BYTES):
    return pltpu.CompilerParams(
        dimension_semantics=("arbitrary",) * n_axes, vmem_limit_bytes=vmem_bytes)


def _dot(a, b):
    return jnp.dot(a, b, preferred_element_type=F32)


def _dot_nt(a, b):
    return lax.dot_general(a, b, (((1,), (1,)), ((), ())), preferred_element_type=F32)


def _dot_tn(a, b):
    return lax.dot_general(a, b, (((0,), (0,)), ((), ())), preferred_element_type=F32)


N_SECTIONS = 7
PLAIN_SECTIONS = (2, 3, 6)
HEADS_PER_SECTION = 8
IN_PROJ_TILES = 4
MXU_COLS = 256


def _in_proj_kernel(x_ref, g_ref, w_ref, cos_ref, sin_ref, qg_ref, kg_ref, *rest, n_cast):
    w_refs, o_ref, wb_refs = rest[:n_cast], rest[n_cast], rest[n_cast + 1:2 * n_cast + 1]
    h_ref, r_ref = rest[2 * n_cast + 1:]
    j = pl.program_id(1)
    heads_per_tile = o_ref.shape[1] // HEAD_DIM
    heads_per_dot = MXU_COLS // HEAD_DIM
    scale = HEAD_DIM ** -0.5

    def epilogue(section, a):
        if section in (0, 1):
            y = a * cos_ref[...] + pltpu.roll(a, HEAD_DIM // 2, 1) * sin_ref[...]
            return y * scale if section == 1 else y
        if section in (4, 5):
            ms = jnp.mean(a * a, axis=-1, keepdims=True)
            y = a * lax.rsqrt(ms + EPS) * (qg_ref if section == 4 else kg_ref)[...]
            return y * (scale * LOG2E) if section == 4 else y
        return a

    def tile(tile_index):
        if tile_index == 0:
            x = x_ref[...]
            ms = jnp.mean(x * x, axis=-1, keepdims=True)
            r_ref[...] = jnp.broadcast_to(lax.rsqrt(ms + EPS), r_ref.shape)
            h_ref[...] = (x * g_ref[...]).astype(BF16)
        for src_ref, dst_ref in zip(w_refs, wb_refs):
            dst_ref[...] = src_ref[...].astype(BF16)
        section_of = lambda head: (tile_index * heads_per_tile + head) // HEADS_PER_SECTION
        firsts = sorted(range(0, heads_per_tile, heads_per_dot),
                        key=lambda first: section_of(first) in PLAIN_SECTIONS)
        for first in firsts:
            cols = slice(first * HEAD_DIM, (first + heads_per_dot) * HEAD_DIM)
            acc = _dot(h_ref[...], w_ref[:, cols])
            for u in range(heads_per_dot):
                section = section_of(first + u)
                a = acc[:, u * HEAD_DIM:(u + 1) * HEAD_DIM] * r_ref[...]
                lanes = slice((first + u) * HEAD_DIM, (first + u + 1) * HEAD_DIM)
                o_ref[:, lanes] = epilogue(section, a).astype(BF16)

    for tile_index in range(IN_PROJ_TILES):
        pl.when(j == tile_index)(functools.partial(tile, tile_index))


def _in_proj(x2d, g1, w_bf16, cos2, sin2, qg, kg, weights, *, seq, tm):
    m, d = x2d.shape
    n = w_bf16.shape[1]
    tn = n // IN_PROJ_TILES
    assert n == N_SECTIONS * HEADS_PER_SECTION * HEAD_DIM and tn % MXU_COLS == 0
    assert m % tm == 0 and seq % tm == 0
    blocks_per_seq = seq // tm
    n_slabs = (m // tm) * IN_PROJ_TILES

    def slab(w):
        rows, cols = w.shape
        assert rows % (n_slabs * BF16_SUBLANES) == 0
        return pl.BlockSpec((rows // n_slabs, cols), lambda i, j: (i * IN_PROJ_TILES + j, 0))

    slabs = [slab(w) for w in weights]
    proj, *weights_bf16 = pl.pallas_call(
        functools.partial(_in_proj_kernel, n_cast=len(weights)),
        out_shape=[jax.ShapeDtypeStruct((m, n), BF16)]
                  + [jax.ShapeDtypeStruct(w.shape, BF16) for w in weights],
        grid=(m // tm, IN_PROJ_TILES),
        in_specs=[
            pl.BlockSpec((tm, d), lambda i, j: (i, 0)),
            pl.BlockSpec((1, d), lambda i, j: (0, 0)),
            pl.BlockSpec((d, tn), lambda i, j: (0, j)),
            pl.BlockSpec((tm, HEAD_DIM), lambda i, j: (i % blocks_per_seq, 0)),
            pl.BlockSpec((tm, HEAD_DIM), lambda i, j: (i % blocks_per_seq, 0)),
            pl.BlockSpec((1, HEAD_DIM), lambda i, j: (0, 0)),
            pl.BlockSpec((1, HEAD_DIM), lambda i, j: (0, 0)),
        ] + slabs,
        out_specs=[pl.BlockSpec((tm, tn), lambda i, j: (i, j))] + slabs,
        scratch_shapes=[pltpu.VMEM((tm, d), BF16), pltpu.VMEM((tm, HEAD_DIM), F32)],
        compiler_params=_params(2, V7X_VMEM_HIGH_LIMIT_BYTES),
        name="in_proj",
    )(x2d, g1, w_bf16, cos2, sin2, qg, kg, *weights)
    return proj, weights_bf16


def _bias_tile_kernel(g_ref, o_ref):
    row = jnp.broadcast_to(g_ref[0], (SEQ_BLOCK, TOEPLITZ_WIDTH))
    t = pltpu.roll(row, 0, 1, stride=1, stride_axis=0)[:, :WINDOW]
    qc = lax.broadcasted_iota(jnp.int32, (SEQ_BLOCK, WINDOW), 0) // CHUNK
    kc = lax.broadcasted_iota(jnp.int32, (SEQ_BLOCK, WINDOW), 1) // CHUNK
    visible = (kc >= qc) & (kc <= qc + LEFT_CHUNKS)
    o_ref[0] = jnp.where(visible, t * LOG2E, NEG)


def _bias_tile(rel_bias):
    h, rel_size = rel_bias.shape
    assert rel_size == CHUNK + REL_CLIP
    far = rel_bias[:, rel_size - 1:]
    near = rel_bias[:, :1]
    left = LEFT_CHUNKS * CHUNK - REL_CLIP
    g = jnp.concatenate([
        jnp.broadcast_to(far, (h, left)),
        jnp.flip(rel_bias, axis=-1),
        jnp.broadcast_to(near, (h, WINDOW - left - rel_size)),
        jnp.broadcast_to(far, (h, TOEPLITZ_WIDTH - WINDOW)),
    ], axis=-1).astype(F32).reshape(h, 1, TOEPLITZ_WIDTH)
    return pl.pallas_call(
        _bias_tile_kernel,
        out_shape=jax.ShapeDtypeStruct((h, SEQ_BLOCK, WINDOW), F32),
        grid=(h,),
        in_specs=[pl.BlockSpec((1, 1, TOEPLITZ_WIDTH), lambda i: (i, 0, 0))],
        out_specs=pl.BlockSpec((1, SEQ_BLOCK, WINDOW), lambda i: (i, 0, 0)),
        compiler_params=_params(1),
        name="rel_bias_tile",
    )(g)


def _retention_tables():
    h = np.arange(RET_HEADS, dtype=np.float64)
    log_g = np.log1p(-np.exp2(-(5.0 + h)))
    pos = np.arange(SEQ_BLOCK, dtype=np.float64)
    dist = np.abs(pos[:, None] - pos[None, :])
    vis = (pos[None, :] // CHUNK) <= (pos[:, None] // CHUNK)
    decay = np.exp(log_g[:, None, None] * dist) * vis
    q_dec = np.exp(log_g[None, :] * (pos + 1.0)[:, None])
    k_dec = np.exp(log_g[None, :] * (SEQ_BLOCK - 1.0 - pos)[:, None])
    blk_dec = np.exp(log_g * SEQ_BLOCK)
    lanes = lambda t: np.repeat(t, HEAD_DIM, axis=1).astype(np.float32)
    return decay.astype(np.float32), lanes(q_dec), lanes(k_dec), [float(v) for v in blk_dec]


def _retention_kernel(q_ref, k_ref, v_ref, gate_ref, dec_ref, qdec_ref, kdec_ref, gn_ref,
                      o_ref, st_ref, *, blocks_per_seq, blk_dec):
    @pl.when(pl.program_id(0) % blocks_per_seq == 0)
    def _():
        st_ref[...] = jnp.zeros_like(st_ref)

    for h in range(RET_HEADS):
        sl = slice(h * HEAD_DIM, (h + 1) * HEAD_DIM)
        state = st_ref[h]
        for blk in range(RET_BLOCKS):
            rows = slice(blk * SEQ_BLOCK, (blk + 1) * SEQ_BLOCK)
            q = q_ref[rows, sl]
            k = k_ref[rows, sl]
            v = v_ref[rows, sl]
            p = (_dot_nt(q, k) * dec_ref[h]).astype(BF16)
            y = _dot(p, v) + _dot(q, state.astype(BF16)) * qdec_ref[:, sl]
            kd = (k.astype(F32) * kdec_ref[:, sl]).astype(BF16)
            state = blk_dec[h] * state + _dot_tn(kd, v)
            mu = jnp.mean(y, axis=-1, keepdims=True)
            yc = y - mu
            var = jnp.mean(yc * yc, axis=-1, keepdims=True)
            yn = yc * lax.rsqrt(var + GN_EPS) * gn_ref[:, sl]
            gate = gate_ref[rows, sl].astype(F32)
            swish = gate * (0.5 + 0.5 * jnp.tanh(0.5 * gate))
            o_ref[rows, sl] = (swish * yn).astype(BF16)
        st_ref[h] = state


def _retention(proj, gn, *, seq):
    m = proj.shape[0]
    width = RET_HEADS * HEAD_DIM
    decay, q_dec, k_dec, blk_dec = _retention_tables()
    step_rows = RET_BLOCKS * SEQ_BLOCK
    assert seq % step_rows == 0
    blocks_per_seq = seq // step_rows
    section = lambda s: pl.BlockSpec((step_rows, width), lambda i: (i, s))
    whole = lambda shape: pl.BlockSpec(shape, lambda i: (0,) * len(shape))
    return pl.pallas_call(
        functools.partial(_retention_kernel, blocks_per_seq=blocks_per_seq, blk_dec=blk_dec),
        out_shape=jax.ShapeDtypeStruct((m, width), BF16),
        grid=(m // step_rows,),
        in_specs=[section(0), section(1), section(2), section(3),
                  whole(decay.shape), whole(q_dec.shape), whole(k_dec.shape), whole((1, width))],
        out_specs=pl.BlockSpec((step_rows, width), lambda i: (i, 0)),
        scratch_shapes=[pltpu.VMEM((RET_HEADS, HEAD_DIM, HEAD_DIM), F32)],
        compiler_params=_params(1),
        name="retention",
    )(proj, proj, proj, proj, decay, q_dec, k_dec, gn)


def _chunk_attn_kernel(q_ref, kp0_ref, kp1_ref, k_ref, vp0_ref, vp1_ref, v_ref, bias_ref, *rest,
                       steps_per_seq, n_cast):
    w_refs, o_ref, wb_refs = rest[:n_cast], rest[n_cast], rest[n_cast + 1:]
    first_step = pl.program_id(0) % steps_per_seq == 0
    add = lambda a, b: a + b
    lane_tiles = SEQ_BLOCK // HEAD_DIM
    rows = lambda i0, i1: slice(i0 * SEQ_BLOCK, (i1 + 1) * SEQ_BLOCK)

    def key_block(prev_refs, main_ref, b, sl):
        if b < LEFT_BLOCKS:
            return prev_refs[b][:, sl]
        return main_ref[rows(b - LEFT_BLOCKS, b - LEFT_BLOCKS), sl]

    def attend(first_block):
        for w_ref, wb_ref in zip(w_refs, wb_refs):
            wb_ref[...] = w_ref[...].astype(BF16)
        key_blocks = range(first_block, ATT_BLOCKS + LEFT_BLOCKS)
        users = {b: (max(0, b - LEFT_BLOCKS), min(ATT_BLOCKS - 1, b)) for b in key_blocks}
        for h in range(ATT_HEADS):
            sl = slice(h * HEAD_DIM, (h + 1) * HEAD_DIM)
            tiles = {i: [] for i in range(ATT_BLOCKS)}
            for b in key_blocks:
                i0, i1 = users[b]
                s = _dot_nt(q_ref[rows(i0, i1), sl], key_block((kp0_ref, kp1_ref), k_ref, b, sl))
                for i in range(i0, i1 + 1):
                    w = b - i
                    sb = (s[rows(i - i0, i - i0), :]
                          + bias_ref[h, :, w * SEQ_BLOCK:(w + 1) * SEQ_BLOCK])
                    tiles[i] += [sb[:, c * HEAD_DIM:(c + 1) * HEAD_DIM] for c in range(lane_tiles)]
            probs, denom = {}, {}
            for i in range(ATT_BLOCKS):
                mx = jnp.max(functools.reduce(jnp.maximum, tiles[i]), axis=-1, keepdims=True)
                p = [jnp.exp2(tile - mx) for tile in tiles[i]]
                denom[i] = jnp.sum(functools.reduce(add, p), axis=-1, keepdims=True)
                visible = [b for b in key_blocks if users[b][0] <= i <= users[b][1]]
                for n, b in enumerate(visible):
                    probs[i, b] = jnp.concatenate(
                        p[n * lane_tiles:(n + 1) * lane_tiles], axis=1).astype(BF16)
            out = {i: [] for i in range(ATT_BLOCKS)}
            for b in key_blocks:
                i0, i1 = users[b]
                lhs = jnp.concatenate([probs[i, b] for i in range(i0, i1 + 1)], axis=0)
                pv = _dot(lhs, key_block((vp0_ref, vp1_ref), v_ref, b, sl))
                for i in range(i0, i1 + 1):
                    out[i].append(pv[rows(i - i0, i - i0), :])
            for i in range(ATT_BLOCKS):
                o_ref[rows(i, i), sl] = (functools.reduce(add, out[i]) / denom[i]).astype(BF16)

    pl.when(first_step)(functools.partial(attend, LEFT_BLOCKS))
    pl.when(jnp.logical_not(first_step))(functools.partial(attend, 0))


def _chunk_attn(proj, bias_tile, weights, *, seq):
    m = proj.shape[0]
    width = ATT_HEADS * HEAD_DIM
    step_rows = ATT_BLOCKS * SEQ_BLOCK
    assert seq % step_rows == 0 and ATT_BLOCKS >= LEFT_BLOCKS
    steps_per_seq = seq // step_rows
    steps = m // step_rows

    def previous(section, b):
        def index(i):
            seq_start = (i - i % steps_per_seq) * ATT_BLOCKS
            return (jnp.maximum(i * ATT_BLOCKS - (LEFT_BLOCKS - b), seq_start), section)
        return pl.BlockSpec((SEQ_BLOCK, width), index)

    main = lambda section: pl.BlockSpec((step_rows, width), lambda i: (i, section))

    def slab(w):
        rows, cols = w.shape
        assert rows % (steps * BF16_SUBLANES) == 0
        return pl.BlockSpec((rows // steps, cols), lambda i: (i, 0))

    att, *weights_bf16 = pl.pallas_call(
        functools.partial(_chunk_attn_kernel, steps_per_seq=steps_per_seq, n_cast=len(weights)),
        out_shape=[jax.ShapeDtypeStruct((m, width), BF16)]
                  + [jax.ShapeDtypeStruct(w.shape, BF16) for w in weights],
        grid=(steps,),
        in_specs=[main(4)]
                 + [previous(5, b) for b in range(LEFT_BLOCKS)] + [main(5)]
                 + [previous(6, b) for b in range(LEFT_BLOCKS)] + [main(6)]
                 + [pl.BlockSpec(bias_tile.shape, lambda i: (0, 0, 0))]
                 + [slab(w) for w in weights],
        out_specs=[pl.BlockSpec((step_rows, width), lambda i: (i, 0))] + [slab(w) for w in weights],
        compiler_params=_params(1),
        name="chunk_attn",
    )(proj, proj, proj, proj, proj, proj, proj, bias_tile, *weights)
    return att, weights_bf16


def _out_proj_kernel(ret_ref, att_ref, wr_ref, wa_ref, x_ref, o_ref):
    o_ref[...] = x_ref[...] + (_dot(ret_ref[...], wr_ref[...]) + _dot(att_ref[...], wa_ref[...]))


def _out_proj(ret, att, w_bf16, x2d, *, tm, tn):
    m, d = x2d.shape
    kr, ka = ret.shape[1], att.shape[1]
    assert kr == ka and kr + ka == w_bf16.shape[0] and m % tm == 0 and d % tn == 0
    return pl.pallas_call(
        _out_proj_kernel,
        out_shape=jax.ShapeDtypeStruct((m, d), F32),
        grid=(m // tm, d // tn),
        in_specs=[
            pl.BlockSpec((tm, kr), lambda i, j: (i, 0)),
            pl.BlockSpec((tm, ka), lambda i, j: (i, 0)),
            pl.BlockSpec((kr, tn), lambda i, j: (0, j)),
            pl.BlockSpec((ka, tn), lambda i, j: (1, j)),
            pl.BlockSpec((tm, tn), lambda i, j: (i, j)),
        ],
        out_specs=pl.BlockSpec((tm, tn), lambda i, j: (i, j)),
        compiler_params=_params(2),
        name="out_proj",
    )(ret, att, w_bf16, w_bf16, x2d)


def _ffn_kernel(x_ref, g_ref, w1_ref, w2_ref, o_ref, h_ref, r_ref):
    f = pl.program_id(1)

    def chunk():
        a = _dot(h_ref[...], w1_ref[...])
        hidden = []
        for c in range(a.shape[1] // HEAD_DIM):
            act = jnp.maximum(a[:, c * HEAD_DIM:(c + 1) * HEAD_DIM] * r_ref[...], 0.0)
            hidden.append((act * act).astype(BF16))
        o_ref[...] += _dot(jnp.concatenate(hidden, axis=1), w2_ref[...])

    @pl.when(f == 0)
    def _():
        x = x_ref[...]
        ms = jnp.mean(x * x, axis=-1, keepdims=True)
        r_ref[...] = jnp.broadcast_to(lax.rsqrt(ms + EPS), r_ref.shape)
        h_ref[...] = (x * g_ref[...]).astype(BF16)
        o_ref[...] = x
        chunk()

    pl.when(f > 0)(chunk)


def _ffn(x2d, g2, w1_bf16, w2_bf16, *, tm, tf):
    m, d = x2d.shape
    d_ff = w1_bf16.shape[1]
    assert m % tm == 0 and d_ff % tf == 0
    return pl.pallas_call(
        _ffn_kernel,
        out_shape=jax.ShapeDtypeStruct((m, d), F32),
        grid=(m // tm, d_ff // tf),
        in_specs=[
            pl.BlockSpec((tm, d), lambda i, f: (i, 0)),
            pl.BlockSpec((1, d), lambda i, f: (0, 0)),
            pl.BlockSpec((d, tf), lambda i, f: (0, f)),
            pl.BlockSpec((tf, d), lambda i, f: (f, 0)),
        ],
        out_specs=pl.BlockSpec((tm, d), lambda i, f: (i, 0)),
        scratch_shapes=[pltpu.VMEM((tm, d), BF16), pltpu.VMEM((tm, HEAD_DIM), F32)],
        compiler_params=_params(2, V7X_VMEM_HIGH_LIMIT_BYTES),
        name="ffn",
    )(x2d, g2, w1_bf16, w2_bf16)


def _rotary_tables(seq):
    half = HEAD_DIM // 2
    inv_freq = ROPE_BASE ** (-np.arange(half, dtype=np.float64) / half)
    ang = np.arange(seq, dtype=np.float64)[:, None] * inv_freq[None, :]
    cos, sin = np.cos(ang), np.sin(ang)
    cos2 = np.concatenate([cos, cos], axis=1).astype(np.float32)
    sin2 = np.concatenate([-sin, sin], axis=1).astype(np.float32)
    return cos2, sin2


def kernel(x, norm1_g, w_in, ret_norm_g, q_norm_g, k_norm_g, rel_bias, w_out, norm2_g, w_ff1, w_ff2):
    b, s, d = x.shape
    depth = w_in.shape[0]
    assert s % SEQ_BLOCK == 0 and SEQ_BLOCK % CHUNK == 0
    assert RET_HEADS == HEADS_PER_SECTION and ATT_HEADS == HEADS_PER_SECTION
    cos2, sin2 = _rotary_tables(s)
    row = lambda v: v.astype(F32).reshape(1, -1)
    x2d = x.reshape(b * s, d)
    for layer in range(depth):
        proj, (w1, w2) = _in_proj(x2d, row(norm1_g[layer]), w_in[layer].astype(BF16), cos2, sin2,
                                  row(q_norm_g[layer]), row(k_norm_g[layer]),
                                  (w_ff1[layer], w_ff2[layer]), seq=s, tm=IN_PROJ_ROWS)
        ret = _retention(proj, row(ret_norm_g[layer]), seq=s)
        att, (wo,) = _chunk_attn(proj, _bias_tile(rel_bias[layer]), (w_out[layer],), seq=s)
        x2d = _out_proj(ret, att, wo, x2d, tm=OUT_PROJ_ROWS, tn=d)
        x2d = _ffn(x2d, row(norm2_g[layer]), w1, w2, tm=FFN_ROWS, tf=FFN_CHUNK)
    return x2d.reshape(b, s, d)
```

```python
import functools
import math

import numpy as np
import jax
import jax.numpy as jnp
from jax import lax
from jax.experimental import pallas as pl
from jax.experimental.pallas import tpu as pltpu

CHUNK = 64
LEFT_CHUNKS = 8
REL_CLIP = 128
HEAD_DIM = 128
RET_HEADS = 8
ATT_HEADS = 8
ROPE_BASE = 10000.0
EPS = 1e-6
GN_EPS = 1e-5
NEG = -1e30
LOG2E = math.log2(math.e)

V7X_VMEM_BYTES = 64 * 1024 * 1024
V7X_VMEM_LIMIT_BYTES = V7X_VMEM_BYTES - 8 * 1024 * 1024
V7X_VMEM_HIGH_LIMIT_BYTES = V7X_VMEM_BYTES - 4 * 1024 * 1024

SEQ_BLOCK = 256
LEFT_BLOCKS = LEFT_CHUNKS * CHUNK // SEQ_BLOCK
WINDOW = (LEFT_BLOCKS + 1) * SEQ_BLOCK
ATT_BLOCKS = 4
RET_BLOCKS = 4
TOEPLITZ_WIDTH = 1024
IN_PROJ_ROWS = 1024
OUT_PROJ_ROWS = 512
FFN_ROWS = 1024
FFN_CHUNK = 1024

F32 = jnp.float32
BF16 = jnp.bfloat16
BF16_SUBLANES = 16


def _params(n_axes, vmem_bytes=V7X_VMEM_LIMIT_BYTES):
    return pltpu.CompilerParams(
        dimension_semantics=("arbitrary",) * n_axes, vmem_limit_bytes=vmem_bytes)


def _dot(a, b):
    return jnp.dot(a, b, preferred_element_type=F32)


def _dot_nt(a, b):
    return lax.dot_general(a, b, (((1,), (1,)), ((), ())), preferred_element_type=F32)


def _dot_tn(a, b):
    return lax.dot_general(a, b, (((0,), (0,)), ((), ())), preferred_element_type=F32)


N_SECTIONS = 7
PLAIN_SECTIONS = (2, 3, 6)
HEADS_PER_SECTION = 8
IN_PROJ_TILES = 4
MXU_COLS = 256


def _in_proj_kernel(x_ref, g_ref, w_ref, cos_ref, sin_ref, qg_ref, kg_ref, *rest, n_cast):
    w_refs, o_ref, wb_refs = rest[:n_cast], rest[n_cast], rest[n_cast + 1:2 * n_cast + 1]
    h_ref, r_ref = rest[2 * n_cast + 1:]
    j = pl.program_id(1)
    heads_per_tile = o_ref.shape[1] // HEAD_DIM
    heads_per_dot = MXU_COLS // HEAD_DIM
    scale = HEAD_DIM ** -0.5

    def epilogue(section, a):
        if section in (0, 1):
            y = a * cos_ref[...] + pltpu.roll(a, HEAD_DIM // 2, 1) * sin_ref[...]
            return y * scale if section == 1 else y
        if section in (4, 5):
            ms = jnp.mean(a * a, axis=-1, keepdims=True)
            y = a * lax.rsqrt(ms + EPS) * (qg_ref if section == 4 else kg_ref)[...]
            return y * (scale * LOG2E) if section == 4 else y
        return a

    def tile(tile_index):
        if tile_index == 0:
            x = x_ref[...]
            ms = jnp.mean(x * x, axis=-1, keepdims=True)
            r_ref[...] = jnp.broadcast_to(lax.rsqrt(ms + EPS), r_ref.shape)
            h_ref[...] = (x * g_ref[...]).astype(BF16)
        for src_ref, dst_ref in zip(w_refs, wb_refs):
            dst_ref[...] = src_ref[...].astype(BF16)
        section_of = lambda head: (tile_index * heads_per_tile + head) // HEADS_PER_SECTION
        firsts = sorted(range(0, heads_per_tile, heads_per_dot),
                        key=lambda first: section_of(first) in PLAIN_SECTIONS)
        for first in firsts:
            cols = slice(first * HEAD_DIM, (first + heads_per_dot) * HEAD_DIM)
            acc = _dot(h_ref[...], w_ref[:, cols])
            for u in range(heads_per_dot):
                section = section_of(first + u)
                a = acc[:, u * HEAD_DIM:(u + 1) * HEAD_DIM] * r_ref[...]
                lanes = slice((first + u) * HEAD_DIM, (first + u + 1) * HEAD_DIM)
                o_ref[:, lanes] = epilogue(section, a).astype(BF16)

    for tile_index in range(IN_PROJ_TILES):
        pl.when(j == tile_index)(functools.partial(tile, tile_index))


def _in_proj(x2d, g1, w_bf16, cos2, sin2, qg, kg, weights, *, seq, tm):
    m, d = x2d.shape
    n = w_bf16.shape[1]
    tn = n // IN_PROJ_TILES
    assert n == N_SECTIONS * HEADS_PER_SECTION * HEAD_DIM and tn % MXU_COLS == 0
    assert m % tm == 0 and seq % tm == 0
    blocks_per_seq = seq // tm
    n_slabs = (m // tm) * IN_PROJ_TILES

    def slab(w):
        rows, cols = w.shape
        assert rows % (n_slabs * BF16_SUBLANES) == 0
        return pl.BlockSpec((rows // n_slabs, cols), lambda i, j: (i * IN_PROJ_TILES + j, 0))

    slabs = [slab(w) for w in weights]
    proj, *weights_bf16 = pl.pallas_call(
        functools.partial(_in_proj_kernel, n_cast=len(weights)),
        out_shape=[jax.ShapeDtypeStruct((m, n), BF16)]
                  + [jax.ShapeDtypeStruct(w.shape, BF16) for w in weights],
        grid=(m // tm, IN_PROJ_TILES),
        in_specs=[
            pl.BlockSpec((tm, d), lambda i, j: (i, 0)),
            pl.BlockSpec((1, d), lambda i, j: (0, 0)),
            pl.BlockSpec((d, tn), lambda i, j: (0, j)),
            pl.BlockSpec((tm, HEAD_DIM), lambda i, j: (i % blocks_per_seq, 0)),
            pl.BlockSpec((tm, HEAD_DIM), lambda i, j: (i % blocks_per_seq, 0)),
            pl.BlockSpec((1, HEAD_DIM), lambda i, j: (0, 0)),
            pl.BlockSpec((1, HEAD_DIM), lambda i, j: (0, 0)),
        ] + slabs,
        out_specs=[pl.BlockSpec((tm, tn), lambda i, j: (i, j))] + slabs,
        scratch_shapes=[pltpu.VMEM((tm, d), BF16), pltpu.VMEM((tm, HEAD_DIM), F32)],
        compiler_params=_params(2, V7X_VMEM_HIGH_LIMIT_BYTES),
        name="in_proj",
    )(x2d, g1, w_bf16, cos2, sin2, qg, kg, *weights)
    return proj, weights_bf16


def _bias_rows(rel_bias):
    h, rel_size = rel_bias.shape
    assert rel_size == CHUNK + REL_CLIP
    far = rel_bias[:, rel_size - 1:]
    near = rel_bias[:, :1]
    left = LEFT_CHUNKS * CHUNK - REL_CLIP
    return jnp.concatenate([
        jnp.broadcast_to(far, (h, left)),
        jnp.flip(rel_bias, axis=-1),
        jnp.broadcast_to(near, (h, WINDOW - left - rel_size)),
        jnp.broadcast_to(far, (h, TOEPLITZ_WIDTH - WINDOW)),
    ], axis=-1).astype(F32).reshape(h, 1, TOEPLITZ_WIDTH)


def _fill_bias_tile(g_ref, bias_ref):
    qc = lax.broadcasted_iota(jnp.int32, (SEQ_BLOCK, WINDOW), 0) // CHUNK
    kc = lax.broadcasted_iota(jnp.int32, (SEQ_BLOCK, WINDOW), 1) // CHUNK
    visible = (kc >= qc) & (kc <= qc + LEFT_CHUNKS)
    for h in range(ATT_HEADS):
        row = jnp.broadcast_to(g_ref[h], (SEQ_BLOCK, TOEPLITZ_WIDTH))
        t = pltpu.roll(row, 0, 1, stride=1, stride_axis=0)[:, :WINDOW]
        bias_ref[h] = jnp.where(visible, t * LOG2E, NEG)


def _retention_tables():
    h = np.arange(RET_HEADS, dtype=np.float64)
    log_g = np.log1p(-np.exp2(-(5.0 + h)))
    pos = np.arange(SEQ_BLOCK, dtype=np.float64)
    dist = np.abs(pos[:, None] - pos[None, :])
    vis = (pos[None, :] // CHUNK) <= (pos[:, None] // CHUNK)
    decay = np.exp(log_g[:, None, None] * dist) * vis
    q_dec = np.exp(log_g[None, :] * (pos + 1.0)[:, None])
    k_dec = np.exp(log_g[None, :] * (SEQ_BLOCK - 1.0 - pos)[:, None])
    blk_dec = np.exp(log_g * SEQ_BLOCK)
    lanes = lambda t: np.repeat(t, HEAD_DIM, axis=1).astype(np.float32)
    return decay.astype(np.float32), lanes(q_dec), lanes(k_dec), [float(v) for v in blk_dec]


def _retention_kernel(q_ref, k_ref, v_ref, gate_ref, dec_ref, qdec_ref, kdec_ref, gn_ref,
                      o_ref, st_ref, *, blocks_per_seq, blk_dec):
    @pl.when(pl.program_id(0) % blocks_per_seq == 0)
    def _():
        st_ref[...] = jnp.zeros_like(st_ref)

    for h in range(RET_HEADS):
        sl = slice(h * HEAD_DIM, (h + 1) * HEAD_DIM)
        state = st_ref[h]
        for blk in range(RET_BLOCKS):
            rows = slice(blk * SEQ_BLOCK, (blk + 1) * SEQ_BLOCK)
            q = q_ref[rows, sl]
            k = k_ref[rows, sl]
            v = v_ref[rows, sl]
            p = (_dot_nt(q, k) * dec_ref[h]).astype(BF16)
            y = _dot(p, v) + _dot(q, state.astype(BF16)) * qdec_ref[:, sl]
            kd = (k.astype(F32) * kdec_ref[:, sl]).astype(BF16)
            state = blk_dec[h] * state + _dot_tn(kd, v)
            mu = jnp.mean(y, axis=-1, keepdims=True)
            yc = y - mu
            var = jnp.mean(yc * yc, axis=-1, keepdims=True)
            yn = yc * lax.rsqrt(var + GN_EPS) * gn_ref[:, sl]
            gate = gate_ref[rows, sl].astype(F32)
            swish = gate * (0.5 + 0.5 * jnp.tanh(0.5 * gate))
            o_ref[rows, sl] = (swish * yn).astype(BF16)
        st_ref[h] = state


def _retention(proj, gn, *, seq):
    m = proj.shape[0]
    width = RET_HEADS * HEAD_DIM
    decay, q_dec, k_dec, blk_dec = _retention_tables()
    step_rows = RET_BLOCKS * SEQ_BLOCK
    assert seq % step_rows == 0
    blocks_per_seq = seq // step_rows
    section = lambda s: pl.BlockSpec((step_rows, width), lambda i: (i, s))
    whole = lambda shape: pl.BlockSpec(shape, lambda i: (0,) * len(shape))
    return pl.pallas_call(
        functools.partial(_retention_kernel, blocks_per_seq=blocks_per_seq, blk_dec=blk_dec),
        out_shape=jax.ShapeDtypeStruct((m, width), BF16),
        grid=(m // step_rows,),
        in_specs=[section(0), section(1), section(2), section(3),
                  whole(decay.shape), whole(q_dec.shape), whole(k_dec.shape), whole((1, width))],
        out_specs=pl.BlockSpec((step_rows, width), lambda i: (i, 0)),
        scratch_shapes=[pltpu.VMEM((RET_HEADS, HEAD_DIM, HEAD_DIM), F32)],
        compiler_params=_params(1),
        name="retention",
    )(proj, proj, proj, proj, decay, q_dec, k_dec, gn)


def _chunk_attn_kernel(q_ref, kp0_ref, kp1_ref, k_ref, vp0_ref, vp1_ref, v_ref, g_ref, *rest,
                       steps_per_seq, n_cast):
    w_refs, o_ref, wb_refs = rest[:n_cast], rest[n_cast], rest[n_cast + 1:2 * n_cast + 1]
    bias_ref = rest[2 * n_cast + 1]
    pl.when(pl.program_id(0) == 0)(functools.partial(_fill_bias_tile, g_ref, bias_ref))
    first_step = pl.program_id(0) % steps_per_seq == 0
    add = lambda a, b: a + b
    lane_tiles = SEQ_BLOCK // HEAD_DIM
    rows = lambda i0, i1: slice(i0 * SEQ_BLOCK, (i1 + 1) * SEQ_BLOCK)

    def key_block(prev_refs, main_ref, b, sl):
        if b < LEFT_BLOCKS:
            return prev_refs[b][:, sl]
        return main_ref[rows(b - LEFT_BLOCKS, b - LEFT_BLOCKS), sl]

    def attend(first_block):
        for w_ref, wb_ref in zip(w_refs, wb_refs):
            wb_ref[...] = w_ref[...].astype(BF16)
        key_blocks = range(first_block, ATT_BLOCKS + LEFT_BLOCKS)
        users = {b: (max(0, b - LEFT_BLOCKS), min(ATT_BLOCKS - 1, b)) for b in key_blocks}
        for h in range(ATT_HEADS):
            sl = slice(h * HEAD_DIM, (h + 1) * HEAD_DIM)
            tiles = {i: [] for i in range(ATT_BLOCKS)}
            for b in key_blocks:
                i0, i1 = users[b]
                s = _dot_nt(q_ref[rows(i0, i1), sl], key_block((kp0_ref, kp1_ref), k_ref, b, sl))
                for i in range(i0, i1 + 1):
                    w = b - i
                    sb = (s[rows(i - i0, i - i0), :]
                          + bias_ref[h, :, w * SEQ_BLOCK:(w + 1) * SEQ_BLOCK])
                    tiles[i] += [sb[:, c * HEAD_DIM:(c + 1) * HEAD_DIM] for c in range(lane_tiles)]
            probs, denom = {}, {}
            for i in range(ATT_BLOCKS):
                mx = jnp.max(functools.reduce(jnp.maximum, tiles[i]), axis=-1, keepdims=True)
                p = [jnp.exp2(tile - mx) for tile in tiles[i]]
                denom[i] = jnp.sum(functools.reduce(add, p), axis=-1, keepdims=True)
                visible = [b for b in key_blocks if users[b][0] <= i <= users[b][1]]
                for n, b in enumerate(visible):
                    probs[i, b] = jnp.concatenate(
                        p[n * lane_tiles:(n + 1) * lane_tiles], axis=1).astype(BF16)
            out = {i: [] for i in range(ATT_BLOCKS)}
            for b in key_blocks:
                i0, i1 = users[b]
                lhs = jnp.concatenate([probs[i, b] for i in range(i0, i1 + 1)], axis=0)
                pv = _dot(lhs, key_block((vp0_ref, vp1_ref), v_ref, b, sl))
                for i in range(i0, i1 + 1):
                    out[i].append(pv[rows(i - i0, i - i0), :])
            for i in range(ATT_BLOCKS):
                o_ref[rows(i, i), sl] = (functools.reduce(add, out[i]) / denom[i]).astype(BF16)

    pl.when(first_step)(functools.partial(attend, LEFT_BLOCKS))
    pl.when(jnp.logical_not(first_step))(functools.partial(attend, 0))


def _chunk_attn(proj, bias_rows, weights, *, seq):
    m = proj.shape[0]
    width = ATT_HEADS * HEAD_DIM
    step_rows = ATT_BLOCKS * SEQ_BLOCK
    assert seq % step_rows == 0 and ATT_BLOCKS >= LEFT_BLOCKS
    steps_per_seq = seq // step_rows
    steps = m // step_rows

    def previous(section, b):
        def index(i):
            seq_start = (i - i % steps_per_seq) * ATT_BLOCKS
            return (jnp.maximum(i * ATT_BLOCKS - (LEFT_BLOCKS - b), seq_start), section)
        return pl.BlockSpec((SEQ_BLOCK, width), index)

    main = lambda section: pl.BlockSpec((step_rows, width), lambda i: (i, section))

    def slab(w):
        rows, cols = w.shape
        assert rows % (steps * BF16_SUBLANES) == 0
        return pl.BlockSpec((rows // steps, cols), lambda i: (i, 0))

    att, *weights_bf16 = pl.pallas_call(
        functools.partial(_chunk_attn_kernel, steps_per_seq=steps_per_seq, n_cast=len(weights)),
        out_shape=[jax.ShapeDtypeStruct((m, width), BF16)]
                  + [jax.ShapeDtypeStruct(w.shape, BF16) for w in weights],
        grid=(steps,),
        in_specs=[main(4)]
                 + [previous(5, b) for b in range(LEFT_BLOCKS)] + [main(5)]
                 + [previous(6, b) for b in range(LEFT_BLOCKS)] + [main(6)]
                 + [pl.BlockSpec(bias_rows.shape, lambda i: (0, 0, 0))]
                 + [slab(w) for w in weights],
        out_specs=[pl.BlockSpec((step_rows, width), lambda i: (i, 0))] + [slab(w) for w in weights],
        scratch_shapes=[pltpu.VMEM((ATT_HEADS, SEQ_BLOCK, WINDOW), F32)],
        compiler_params=_params(1),
        name="chunk_attn",
    )(proj, proj, proj, proj, proj, proj, proj, bias_rows, *weights)
    return att, weights_bf16


def _out_proj_kernel(ret_ref, att_ref, wr_ref, wa_ref, x_ref, o_ref):
    o_ref[...] = x_ref[...] + (_dot(ret_ref[...], wr_ref[...]) + _dot(att_ref[...], wa_ref[...]))


def _out_proj(ret, att, w_bf16, x2d, *, tm, tn):
    m, d = x2d.shape
    kr, ka = ret.shape[1], att.shape[1]
    assert kr == ka and kr + ka == w_bf16.shape[0] and m % tm == 0 and d % tn == 0
    return pl.pallas_call(
        _out_proj_kernel,
        out_shape=jax.ShapeDtypeStruct((m, d), F32),
        grid=(m // tm, d // tn),
        in_specs=[
            pl.BlockSpec((tm, kr), lambda i, j: (i, 0)),
            pl.BlockSpec((tm, ka), lambda i, j: (i, 0)),
            pl.BlockSpec((kr, tn), lambda i, j: (0, j)),
            pl.BlockSpec((ka, tn), lambda i, j: (1, j)),
            pl.BlockSpec((tm, tn), lambda i, j: (i, j)),
        ],
        out_specs=pl.BlockSpec((tm, tn), lambda i, j: (i, j)),
        compiler_params=_params(2),
        name="out_proj",
    )(ret, att, w_bf16, w_bf16, x2d)


def _ffn_kernel(x_ref, g_ref, w1_ref, w2_ref, o_ref, h_ref, r_ref):
    f = pl.program_id(1)

    def chunk():
        a = _dot(h_ref[...], w1_ref[...])
        hidden = []
        for c in range(a.shape[1] // HEAD_DIM):
            act = jnp.maximum(a[:, c * HEAD_DIM:(c + 1) * HEAD_DIM] * r_ref[...], 0.0)
            hidden.append((act * act).astype(BF16))
        o_ref[...] += _dot(jnp.concatenate(hidden, axis=1), w2_ref[...])

    @pl.when(f == 0)
    def _():
        x = x_ref[...]
        ms = jnp.mean(x * x, axis=-1, keepdims=True)
        r_ref[...] = jnp.broadcast_to(lax.rsqrt(ms + EPS), r_ref.shape)
        h_ref[...] = (x * g_ref[...]).astype(BF16)
        o_ref[...] = x
        chunk()

    pl.when(f > 0)(chunk)


def _ffn(x2d, g2, w1_bf16, w2_bf16, *, tm, tf):
    m, d = x2d.shape
    d_ff = w1_bf16.shape[1]
    assert m % tm == 0 and d_ff % tf == 0
    return pl.pallas_call(
        _ffn_kernel,
        out_shape=jax.ShapeDtypeStruct((m, d), F32),
        grid=(m // tm, d_ff // tf),
        in_specs=[
            pl.BlockSpec((tm, d), lambda i, f: (i, 0)),
            pl.BlockSpec((1, d), lambda i, f: (0, 0)),
            pl.BlockSpec((d, tf), lambda i, f: (0, f)),
            pl.BlockSpec((tf, d), lambda i, f: (f, 0)),
        ],
        out_specs=pl.BlockSpec((tm, d), lambda i, f: (i, 0)),
        scratch_shapes=[pltpu.VMEM((tm, d), BF16), pltpu.VMEM((tm, HEAD_DIM), F32)],
        compiler_params=_params(2, V7X_VMEM_HIGH_LIMIT_BYTES),
        name="ffn",
    )(x2d, g2, w1_bf16, w2_bf16)


def _rotary_tables(seq):
    half = HEAD_DIM // 2
    inv_freq = ROPE_BASE ** (-np.arange(half, dtype=np.float64) / half)
    ang = np.arange(seq, dtype=np.float64)[:, None] * inv_freq[None, :]
    cos, sin = np.cos(ang), np.sin(ang)
    cos2 = np.concatenate([cos, cos], axis=1).astype(np.float32)
    sin2 = np.concatenate([-sin, sin], axis=1).astype(np.float32)
    return cos2, sin2


def kernel(x, norm1_g, w_in, ret_norm_g, q_norm_g, k_norm_g, rel_bias, w_out, norm2_g, w_ff1, w_ff2):
    b, s, d = x.shape
    depth = w_in.shape[0]
    assert s % SEQ_BLOCK == 0 and SEQ_BLOCK % CHUNK == 0
    assert RET_HEADS == HEADS_PER_SECTION and ATT_HEADS == HEADS_PER_SECTION
    cos2, sin2 = _rotary_tables(s)
    row = lambda v: v.astype(F32).reshape(1, -1)
    x2d = x.reshape(b * s, d)
    for layer in range(depth):
        proj, (w1, w2) = _in_proj(x2d, row(norm1_g[layer]), w_in[layer].astype(BF16), cos2, sin2,
                                  row(q_norm_g[layer]), row(k_norm_g[layer]),
                                  (w_ff1[layer], w_ff2[layer]), seq=s, tm=IN_PROJ_ROWS)
        ret = _retention(proj, row(ret_norm_g[layer]), seq=s)
        att, (wo,) = _chunk_attn(proj, _bias_rows(rel_bias[layer]), (w_out[layer],), seq=s)
        x2d = _out_proj(ret, att, wo, x2d, tm=OUT_PROJ_ROWS, tn=d)
        x2d = _ffn(x2d, row(norm2_g[layer]), w1, w2, tm=FFN_ROWS, tf=FFN_CHUNK)
    return x2d.reshape(b, s, d)
```

```python
import functools
import math

import numpy as np
import jax
import jax.numpy as jnp
from jax import lax
from jax.experimental import pallas as pl
from jax.experimental.pallas import tpu as pltpu

CHUNK = 64
LEFT_CHUNKS = 8
REL_CLIP = 128
HEAD_DIM = 128
RET_HEADS = 8
ATT_HEADS = 8
ROPE_BASE = 10000.0
EPS = 1e-6
GN_EPS = 1e-5
NEG = -1e30
LOG2E = math.log2(math.e)

V7X_VMEM_BYTES = 64 * 1024 * 1024
V7X_VMEM_LIMIT_BYTES = V7X_VMEM_BYTES - 8 * 1024 * 1024
V7X_VMEM_HIGH_LIMIT_BYTES = V7X_VMEM_BYTES - 4 * 1024 * 1024

SEQ_BLOCK = 256
LEFT_BLOCKS = LEFT_CHUNKS * CHUNK // SEQ_BLOCK
WINDOW = (LEFT_BLOCKS + 1) * SEQ_BLOCK
ATT_BLOCKS = 4
RET_BLOCKS = 4
TOEPLITZ_WIDTH = 1024
IN_PROJ_ROWS = 1024
OUT_PROJ_ROWS = 512
FFN_ROWS = 1024
FFN_CHUNK = 1024

F32 = jnp.float32
BF16 = jnp.bfloat16
BF16_SUBLANES = 16


def _params(n_axes, vmem_bytes=V7X_VMEM_LIMIT_BYTES):
    return pltpu.CompilerParams(
        dimension_semantics=("arbitrary",) * n_axes, vmem_limit_bytes=vmem_bytes)


def _dot(a, b):
    return jnp.dot(a, b, preferred_element_type=F32)


def _dot_nt(a, b):
    return lax.dot_general(a, b, (((1,), (1,)), ((), ())), preferred_element_type=F32)


def _dot_tn(a, b):
    return lax.dot_general(a, b, (((0,), (0,)), ((), ())), preferred_element_type=F32)


N_SECTIONS = 7
PLAIN_SECTIONS = (2, 3, 6)
HEADS_PER_SECTION = 8
IN_PROJ_TILES = 4
IN_PROJ_PARTS = 2
MXU_COLS = 256


def _in_proj_kernel(x_ref, g_ref, w_ref, cos_ref, sin_ref, qg_ref, kg_ref, *rest, n_cast):
    w_refs, o_ref, wb_refs = rest[:n_cast], rest[n_cast], rest[n_cast + 1:2 * n_cast + 1]
    h_ref, r_ref = rest[2 * n_cast + 1:]
    j = pl.program_id(1)
    heads_per_tile = o_ref.shape[1] // HEAD_DIM
    heads_per_dot = MXU_COLS // HEAD_DIM
    scale = HEAD_DIM ** -0.5

    def epilogue(section, a, rows):
        if section in (0, 1):
            y = a * cos_ref[rows, :] + pltpu.roll(a, HEAD_DIM // 2, 1) * sin_ref[rows, :]
            return y * scale if section == 1 else y
        if section in (4, 5):
            ms = jnp.mean(a * a, axis=-1, keepdims=True)
            y = a * lax.rsqrt(ms + EPS) * (qg_ref if section == 4 else kg_ref)[...]
            return y * (scale * LOG2E) if section == 4 else y
        return a

    def part_of(ref, part):
        n = ref.shape[0] // IN_PROJ_PARTS
        return pl.ds(pl.multiple_of(part * n, n), n)

    def tile_part(tile_index, part, carry):
        rows = part_of(h_ref, part)
        if tile_index == 0:
            x = x_ref[rows, :]
            ms = jnp.mean(x * x, axis=-1, keepdims=True)
            r_ref[rows, :] = jnp.broadcast_to(lax.rsqrt(ms + EPS), (x.shape[0], HEAD_DIM))
            h_ref[rows, :] = (x * g_ref[...]).astype(BF16)
        for src_ref, dst_ref in zip(w_refs, wb_refs):
            slab_rows = part_of(src_ref, part)
            dst_ref[slab_rows, :] = src_ref[slab_rows, :].astype(BF16)
        section_of = lambda head: (tile_index * heads_per_tile + head) // HEADS_PER_SECTION
        firsts = sorted(range(0, heads_per_tile, heads_per_dot),
                        key=lambda first: section_of(first) in PLAIN_SECTIONS)
        for first in firsts:
            cols = slice(first * HEAD_DIM, (first + heads_per_dot) * HEAD_DIM)
            acc = _dot(h_ref[rows, :], w_ref[:, cols])
            for u in range(heads_per_dot):
                section = section_of(first + u)
                a = acc[:, u * HEAD_DIM:(u + 1) * HEAD_DIM] * r_ref[rows, :]
                lanes = slice((first + u) * HEAD_DIM, (first + u + 1) * HEAD_DIM)
                o_ref[rows, lanes] = epilogue(section, a, rows).astype(BF16)
        return carry

    def tile(tile_index):
        lax.fori_loop(0, IN_PROJ_PARTS, functools.partial(tile_part, tile_index), 0)

    for tile_index in range(IN_PROJ_TILES):
        pl.when(j == tile_index)(functools.partial(tile, tile_index))


def _in_proj(x2d, g1, w_bf16, cos2, sin2, qg, kg, weights, *, seq, tm):
    m, d = x2d.shape
    n = w_bf16.shape[1]
    tn = n // IN_PROJ_TILES
    assert n == N_SECTIONS * HEADS_PER_SECTION * HEAD_DIM and tn % MXU_COLS == 0
    assert m % tm == 0 and seq % tm == 0
    blocks_per_seq = seq // tm
    n_slabs = (m // tm) * IN_PROJ_TILES

    def slab(w):
        rows, cols = w.shape
        assert rows % (n_slabs * IN_PROJ_PARTS * BF16_SUBLANES) == 0
        return pl.BlockSpec((rows // n_slabs, cols), lambda i, j: (i * IN_PROJ_TILES + j, 0))

    slabs = [slab(w) for w in weights]
    proj, *weights_bf16 = pl.pallas_call(
        functools.partial(_in_proj_kernel, n_cast=len(weights)),
        out_shape=[jax.ShapeDtypeStruct((m, n), BF16)]
                  + [jax.ShapeDtypeStruct(w.shape, BF16) for w in weights],
        grid=(m // tm, IN_PROJ_TILES),
        in_specs=[
            pl.BlockSpec((tm, d), lambda i, j: (i, 0)),
            pl.BlockSpec((1, d), lambda i, j: (0, 0)),
            pl.BlockSpec((d, tn), lambda i, j: (0, j)),
            pl.BlockSpec((tm, HEAD_DIM), lambda i, j: (i % blocks_per_seq, 0)),
            pl.BlockSpec((tm, HEAD_DIM), lambda i, j: (i % blocks_per_seq, 0)),
            pl.BlockSpec((1, HEAD_DIM), lambda i, j: (0, 0)),
            pl.BlockSpec((1, HEAD_DIM), lambda i, j: (0, 0)),
        ] + slabs,
        out_specs=[pl.BlockSpec((tm, tn), lambda i, j: (i, j))] + slabs,
        scratch_shapes=[pltpu.VMEM((tm, d), BF16), pltpu.VMEM((tm, HEAD_DIM), F32)],
        compiler_params=_params(2, V7X_VMEM_HIGH_LIMIT_BYTES),
        name="in_proj",
    )(x2d, g1, w_bf16, cos2, sin2, qg, kg, *weights)
    return proj, weights_bf16


def _bias_rows(rel_bias):
    h, rel_size = rel_bias.shape
    assert rel_size == CHUNK + REL_CLIP
    far = rel_bias[:, rel_size - 1:]
    near = rel_bias[:, :1]
    left = LEFT_CHUNKS * CHUNK - REL_CLIP
    return jnp.concatenate([
        jnp.broadcast_to(far, (h, left)),
        jnp.flip(rel_bias, axis=-1),
        jnp.broadcast_to(near, (h, WINDOW - left - rel_size)),
        jnp.broadcast_to(far, (h, TOEPLITZ_WIDTH - WINDOW)),
    ], axis=-1).astype(F32).reshape(h, 1, TOEPLITZ_WIDTH)


def _fill_bias_tile(g_ref, bias_ref):
    qc = lax.broadcasted_iota(jnp.int32, (SEQ_BLOCK, WINDOW), 0) // CHUNK
    kc = lax.broadcasted_iota(jnp.int32, (SEQ_BLOCK, WINDOW), 1) // CHUNK
    visible = (kc >= qc) & (kc <= qc + LEFT_CHUNKS)
    for h in range(ATT_HEADS):
        row = jnp.broadcast_to(g_ref[h], (SEQ_BLOCK, TOEPLITZ_WIDTH))
        t = pltpu.roll(row, 0, 1, stride=1, stride_axis=0)[:, :WINDOW]
        bias_ref[h] = jnp.where(visible, t * LOG2E, NEG)


def _retention_tables():
    h = np.arange(RET_HEADS, dtype=np.float64)
    log_g = np.log1p(-np.exp2(-(5.0 + h)))
    pos = np.arange(SEQ_BLOCK, dtype=np.float64)
    dist = np.abs(pos[:, None] - pos[None, :])
    vis = (pos[None, :] // CHUNK) <= (pos[:, None] // CHUNK)
    decay = np.exp(log_g[:, None, None] * dist) * vis
    q_dec = np.exp(log_g[None, :] * (pos + 1.0)[:, None])
    k_dec = np.exp(log_g[None, :] * (SEQ_BLOCK - 1.0 - pos)[:, None])
    blk_dec = np.exp(log_g * SEQ_BLOCK)
    lanes = lambda t: np.repeat(t, HEAD_DIM, axis=1).astype(np.float32)
    return decay.astype(np.float32), lanes(q_dec), lanes(k_dec), [float(v) for v in blk_dec]


def _retention_kernel(q_ref, k_ref, v_ref, gate_ref, dec_ref, qdec_ref, kdec_ref, gn_ref,
                      o_ref, st_ref, *, blocks_per_seq, blk_dec):
    @pl.when(pl.program_id(0) % blocks_per_seq == 0)
    def _():
        st_ref[...] = jnp.zeros_like(st_ref)

    for h in range(RET_HEADS):
        sl = slice(h * HEAD_DIM, (h + 1) * HEAD_DIM)
        state = st_ref[h]
        for blk in range(RET_BLOCKS):
            rows = slice(blk * SEQ_BLOCK, (blk + 1) * SEQ_BLOCK)
            q = q_ref[rows, sl]
            k = k_ref[rows, sl]
            v = v_ref[rows, sl]
            p = (_dot_nt(q, k) * dec_ref[h]).astype(BF16)
            y = _dot(p, v) + _dot(q, state.astype(BF16)) * qdec_ref[:, sl]
            kd = (k.astype(F32) * kdec_ref[:, sl]).astype(BF16)
            state = blk_dec[h] * state + _dot_tn(kd, v)
            mu = jnp.mean(y, axis=-1, keepdims=True)
            yc = y - mu
            var = jnp.mean(yc * yc, axis=-1, keepdims=True)
            yn = yc * lax.rsqrt(var + GN_EPS) * gn_ref[:, sl]
            gate = gate_ref[rows, sl].astype(F32)
            swish = gate * (0.5 + 0.5 * jnp.tanh(0.5 * gate))
            o_ref[rows, sl] = (swish * yn).astype(BF16)
        st_ref[h] = state


def _retention(proj, gn, *, seq):
    m = proj.shape[0]
    width = RET_HEADS * HEAD_DIM
    decay, q_dec, k_dec, blk_dec = _retention_tables()
    step_rows = RET_BLOCKS * SEQ_BLOCK
    assert seq % step_rows == 0
    blocks_per_seq = seq // step_rows
    section = lambda s: pl.BlockSpec((step_rows, width), lambda i: (i, s))
    whole = lambda shape: pl.BlockSpec(shape, lambda i: (0,) * len(shape))
    return pl.pallas_call(
        functools.partial(_retention_kernel, blocks_per_seq=blocks_per_seq, blk_dec=blk_dec),
        out_shape=jax.ShapeDtypeStruct((m, width), BF16),
        grid=(m // step_rows,),
        in_specs=[section(0), section(1), section(2), section(3),
                  whole(decay.shape), whole(q_dec.shape), whole(k_dec.shape), whole((1, width))],
        out_specs=pl.BlockSpec((step_rows, width), lambda i: (i, 0)),
        scratch_shapes=[pltpu.VMEM((RET_HEADS, HEAD_DIM, HEAD_DIM), F32)],
        compiler_params=_params(1),
        name="retention",
    )(proj, proj, proj, proj, decay, q_dec, k_dec, gn)


def _chunk_attn_kernel(q_ref, kp0_ref, kp1_ref, k_ref, vp0_ref, vp1_ref, v_ref, g_ref, *rest,
                       steps_per_seq, n_cast):
    w_refs, o_ref, wb_refs = rest[:n_cast], rest[n_cast], rest[n_cast + 1:2 * n_cast + 1]
    bias_ref = rest[2 * n_cast + 1]
    pl.when(pl.program_id(0) == 0)(functools.partial(_fill_bias_tile, g_ref, bias_ref))
    first_step = pl.program_id(0) % steps_per_seq == 0
    add = lambda a, b: a + b
    lane_tiles = SEQ_BLOCK // HEAD_DIM
    rows = lambda i0, i1: slice(i0 * SEQ_BLOCK, (i1 + 1) * SEQ_BLOCK)

    def key_block(prev_refs, main_ref, b, sl):
        if b < LEFT_BLOCKS:
            return prev_refs[b][:, sl]
        return main_ref[rows(b - LEFT_BLOCKS, b - LEFT_BLOCKS), sl]

    def attend(first_block):
        for w_ref, wb_ref in zip(w_refs, wb_refs):
            wb_ref[...] = w_ref[...].astype(BF16)
        key_blocks = range(first_block, ATT_BLOCKS + LEFT_BLOCKS)
        users = {b: (max(0, b - LEFT_BLOCKS), min(ATT_BLOCKS - 1, b)) for b in key_blocks}
        for h in range(ATT_HEADS):
            sl = slice(h * HEAD_DIM, (h + 1) * HEAD_DIM)
            tiles = {i: [] for i in range(ATT_BLOCKS)}
            for b in key_blocks:
                i0, i1 = users[b]
                s = _dot_nt(q_ref[rows(i0, i1), sl], key_block((kp0_ref, kp1_ref), k_ref, b, sl))
                for i in range(i0, i1 + 1):
                    w = b - i
                    sb = (s[rows(i - i0, i - i0), :]
                          + bias_ref[h, :, w * SEQ_BLOCK:(w + 1) * SEQ_BLOCK])
                    tiles[i] += [sb[:, c * HEAD_DIM:(c + 1) * HEAD_DIM] for c in range(lane_tiles)]
            probs, denom = {}, {}
            for i in range(ATT_BLOCKS):
                mx = jnp.max(functools.reduce(jnp.maximum, tiles[i]), axis=-1, keepdims=True)
                p = [jnp.exp2(tile - mx) for tile in tiles[i]]
                denom[i] = jnp.sum(functools.reduce(add, p), axis=-1, keepdims=True)
                visible = [b for b in key_blocks if users[b][0] <= i <= users[b][1]]
                for n, b in enumerate(visible):
                    probs[i, b] = jnp.concatenate(
                        p[n * lane_tiles:(n + 1) * lane_tiles], axis=1).astype(BF16)
            out = {i: [] for i in range(ATT_BLOCKS)}
            for b in key_blocks:
                i0, i1 = users[b]
                lhs = jnp.concatenate([probs[i, b] for i in range(i0, i1 + 1)], axis=0)
                pv = _dot(lhs, key_block((vp0_ref, vp1_ref), v_ref, b, sl))
                for i in range(i0, i1 + 1):
                    out[i].append(pv[rows(i - i0, i - i0), :])
            for i in range(ATT_BLOCKS):
                o_ref[rows(i, i), sl] = (functools.reduce(add, out[i]) / denom[i]).astype(BF16)

    pl.when(first_step)(functools.partial(attend, LEFT_BLOCKS))
    pl.when(jnp.logical_not(first_step))(functools.partial(attend, 0))


def _chunk_attn(proj, bias_rows, weights, *, seq):
    m = proj.shape[0]
    width = ATT_HEADS * HEAD_DIM
    step_rows = ATT_BLOCKS * SEQ_BLOCK
    assert seq % step_rows == 0 and ATT_BLOCKS >= LEFT_BLOCKS
    steps_per_seq = seq // step_rows
    steps = m // step_rows

    def previous(section, b):
        def index(i):
            seq_start = (i - i % steps_per_seq) * ATT_BLOCKS
            return (jnp.maximum(i * ATT_BLOCKS - (LEFT_BLOCKS - b), seq_start), section)
        return pl.BlockSpec((SEQ_BLOCK, width), index)

    main = lambda section: pl.BlockSpec((step_rows, width), lambda i: (i, section))

    def slab(w):
        rows, cols = w.shape
        assert rows % (steps * BF16_SUBLANES) == 0
        return pl.BlockSpec((rows // steps, cols), lambda i: (i, 0))

    att, *weights_bf16 = pl.pallas_call(
        functools.partial(_chunk_attn_kernel, steps_per_seq=steps_per_seq, n_cast=len(weights)),
        out_shape=[jax.ShapeDtypeStruct((m, width), BF16)]
                  + [jax.ShapeDtypeStruct(w.shape, BF16) for w in weights],
        grid=(steps,),
        in_specs=[main(4)]
                 + [previous(5, b) for b in range(LEFT_BLOCKS)] + [main(5)]
                 + [previous(6, b) for b in range(LEFT_BLOCKS)] + [main(6)]
                 + [pl.BlockSpec(bias_rows.shape, lambda i: (0, 0, 0))]
                 + [slab(w) for w in weights],
        out_specs=[pl.BlockSpec((step_rows, width), lambda i: (i, 0))] + [slab(w) for w in weights],
        scratch_shapes=[pltpu.VMEM((ATT_HEADS, SEQ_BLOCK, WINDOW), F32)],
        compiler_params=_params(1),
        name="chunk_attn",
    )(proj, proj, proj, proj, proj, proj, proj, bias_rows, *weights)
    return att, weights_bf16


def _out_proj_kernel(ret_ref, att_ref, wr_ref, wa_ref, x_ref, o_ref):
    o_ref[...] = x_ref[...] + (_dot(ret_ref[...], wr_ref[...]) + _dot(att_ref[...], wa_ref[...]))


def _out_proj(ret, att, w_bf16, x2d, *, tm, tn):
    m, d = x2d.shape
    kr, ka = ret.shape[1], att.shape[1]
    assert kr == ka and kr + ka == w_bf16.shape[0] and m % tm == 0 and d % tn == 0
    return pl.pallas_call(
        _out_proj_kernel,
        out_shape=jax.ShapeDtypeStruct((m, d), F32),
        grid=(m // tm, d // tn),
        in_specs=[
            pl.BlockSpec((tm, kr), lambda i, j: (i, 0)),
            pl.BlockSpec((tm, ka), lambda i, j: (i, 0)),
            pl.BlockSpec((kr, tn), lambda i, j: (0, j)),
            pl.BlockSpec((ka, tn), lambda i, j: (1, j)),
            pl.BlockSpec((tm, tn), lambda i, j: (i, j)),
        ],
        out_specs=pl.BlockSpec((tm, tn), lambda i, j: (i, j)),
        compiler_params=_params(2),
        name="out_proj",
    )(ret, att, w_bf16, w_bf16, x2d)


def _ffn_kernel(x_ref, g_ref, w1_ref, w2_ref, o_ref, h_ref, r_ref):
    f = pl.program_id(1)

    def chunk():
        a = _dot(h_ref[...], w1_ref[...])
        hidden = []
        for c in range(a.shape[1] // HEAD_DIM):
            act = jnp.maximum(a[:, c * HEAD_DIM:(c + 1) * HEAD_DIM] * r_ref[...], 0.0)
            hidden.append((act * act).astype(BF16))
        o_ref[...] += _dot(jnp.concatenate(hidden, axis=1), w2_ref[...])

    @pl.when(f == 0)
    def _():
        x = x_ref[...]
        ms = jnp.mean(x * x, axis=-1, keepdims=True)
        r_ref[...] = jnp.broadcast_to(lax.rsqrt(ms + EPS), r_ref.shape)
        h_ref[...] = (x * g_ref[...]).astype(BF16)
        o_ref[...] = x
        chunk()

    pl.when(f > 0)(chunk)


def _ffn(x2d, g2, w1_bf16, w2_bf16, *, tm, tf):
    m, d = x2d.shape
    d_ff = w1_bf16.shape[1]
    assert m % tm == 0 and d_ff % tf == 0
    return pl.pallas_call(
        _ffn_kernel,
        out_shape=jax.ShapeDtypeStruct((m, d), F32),
        grid=(m // tm, d_ff // tf),
        in_specs=[
            pl.BlockSpec((tm, d), lambda i, f: (i, 0)),
            pl.BlockSpec((1, d), lambda i, f: (0, 0)),
            pl.BlockSpec((d, tf), lambda i, f: (0, f)),
            pl.BlockSpec((tf, d), lambda i, f: (f, 0)),
        ],
        out_specs=pl.BlockSpec((tm, d), lambda i, f: (i, 0)),
        scratch_shapes=[pltpu.VMEM((tm, d), BF16), pltpu.VMEM((tm, HEAD_DIM), F32)],
        compiler_params=_params(2, V7X_VMEM_HIGH_LIMIT_BYTES),
        name="ffn",
    )(x2d, g2, w1_bf16, w2_bf16)


def _rotary_tables(seq):
    half = HEAD_DIM // 2
    inv_freq = ROPE_BASE ** (-np.arange(half, dtype=np.float64) / half)
    ang = np.arange(seq, dtype=np.float64)[:, None] * inv_freq[None, :]
    cos, sin = np.cos(ang), np.sin(ang)
    cos2 = np.concatenate([cos, cos], axis=1).astype(np.float32)
    sin2 = np.concatenate([-sin, sin], axis=1).astype(np.float32)
    return cos2, sin2


def kernel(x, norm1_g, w_in, ret_norm_g, q_norm_g, k_norm_g, rel_bias, w_out, norm2_g, w_ff1, w_ff2):
    b, s, d = x.shape
    depth = w_in.shape[0]
    assert s % SEQ_BLOCK == 0 and SEQ_BLOCK % CHUNK == 0
    assert RET_HEADS == HEADS_PER_SECTION and ATT_HEADS == HEADS_PER_SECTION
    cos2, sin2 = _rotary_tables(s)
    row = lambda v: v.astype(F32).reshape(1, -1)
    x2d = x.reshape(b * s, d)
    for layer in range(depth):
        proj, (w1, w2) = _in_proj(x2d, row(norm1_g[layer]), w_in[layer].astype(BF16), cos2, sin2,
                                  row(q_norm_g[layer]), row(k_norm_g[layer]),
                                  (w_ff1[layer], w_ff2[layer]), seq=s, tm=IN_PROJ_ROWS)
        ret = _retention(proj, row(ret_norm_g[layer]), seq=s)
        att, (wo,) = _chunk_attn(proj, _bias_rows(rel_bias[layer]), (w_out[layer],), seq=s)
        x2d = _out_proj(ret, att, wo, x2d, tm=OUT_PROJ_ROWS, tn=d)
        x2d = _ffn(x2d, row(norm2_g[layer]), w1, w2, tm=FFN_ROWS, tf=FFN_CHUNK)
    return x2d.reshape(b, s, d)
```

```python
import functools
import math

import numpy as np
import jax
import jax.numpy as jnp
from jax import lax
from jax.experimental import pallas as pl
from jax.experimental.pallas import tpu as pltpu

CHUNK = 64
LEFT_CHUNKS = 8
REL_CLIP = 128
HEAD_DIM = 128
RET_HEADS = 8
ATT_HEADS = 8
ROPE_BASE = 10000.0
EPS = 1e-6
GN_EPS = 1e-5
NEG = -1e30
LOG2E = math.log2(math.e)

V7X_VMEM_BYTES = 64 * 1024 * 1024
V7X_VMEM_LIMIT_BYTES = V7X_VMEM_BYTES - 8 * 1024 * 1024
V7X_VMEM_HIGH_LIMIT_BYTES = V7X_VMEM_BYTES - 4 * 1024 * 1024

SEQ_BLOCK = 256
LEFT_BLOCKS = LEFT_CHUNKS * CHUNK // SEQ_BLOCK
WINDOW = (LEFT_BLOCKS + 1) * SEQ_BLOCK
ATT_BLOCKS = 4
RET_BLOCKS = 4
TOEPLITZ_WIDTH = 1024
IN_PROJ_ROWS = 1024
OUT_PROJ_ROWS = 512
FFN_ROWS = 1024
FFN_CHUNK = 1024

F32 = jnp.float32
BF16 = jnp.bfloat16
BF16_SUBLANES = 16


def _params(n_axes, vmem_bytes=V7X_VMEM_LIMIT_BYTES):
    return pltpu.CompilerParams(
        dimension_semantics=("arbitrary",) * n_axes, vmem_limit_bytes=vmem_bytes)


def _dot(a, b):
    return jnp.dot(a, b, preferred_element_type=F32)


def _dot_nt(a, b):
    return lax.dot_general(a, b, (((1,), (1,)), ((), ())), preferred_element_type=F32)


def _dot_tn(a, b):
    return lax.dot_general(a, b, (((0,), (0,)), ((), ())), preferred_element_type=F32)


N_SECTIONS = 7
PLAIN_SECTIONS = (2, 3, 6)
HEADS_PER_SECTION = 8
IN_PROJ_TILES = 4
IN_PROJ_PARTS = 2
MXU_COLS = 256


def _in_proj_kernel(x_ref, g_ref, w_ref, cos_ref, sin_ref, qg_ref, kg_ref, *rest, n_cast):
    w_refs, o_ref, wb_refs = rest[:n_cast], rest[n_cast], rest[n_cast + 1:2 * n_cast + 1]
    h_ref, r_ref = rest[2 * n_cast + 1:]
    j = pl.program_id(1)
    heads_per_tile = o_ref.shape[1] // HEAD_DIM
    heads_per_dot = MXU_COLS // HEAD_DIM
    scale = HEAD_DIM ** -0.5

    def epilogue(section, a, rows):
        if section in (0, 1):
            y = a * cos_ref[rows, :] + pltpu.roll(a, HEAD_DIM // 2, 1) * sin_ref[rows, :]
            return y * scale if section == 1 else y
        if section in (4, 5):
            ms = jnp.mean(a * a, axis=-1, keepdims=True)
            y = a * lax.rsqrt(ms + EPS) * (qg_ref if section == 4 else kg_ref)[...]
            return y * (scale * LOG2E) if section == 4 else y
        return a

    def part_of(ref, part):
        n = ref.shape[0] // IN_PROJ_PARTS
        return pl.ds(pl.multiple_of(part * n, n), n)

    def tile_part(tile_index, part, carry):
        rows = part_of(h_ref, part)
        if tile_index == 0:
            x = x_ref[rows, :]
            ms = jnp.mean(x * x, axis=-1, keepdims=True)
            r_ref[rows, :] = jnp.broadcast_to(lax.rsqrt(ms + EPS), (x.shape[0], HEAD_DIM))
            h_ref[rows, :] = (x * g_ref[...]).astype(BF16)
        for src_ref, dst_ref in zip(w_refs, wb_refs):
            slab_rows = part_of(src_ref, part)
            dst_ref[slab_rows, :] = src_ref[slab_rows, :].astype(BF16)
        section_of = lambda head: (tile_index * heads_per_tile + head) // HEADS_PER_SECTION
        firsts = sorted(range(0, heads_per_tile, heads_per_dot),
                        key=lambda first: section_of(first) in PLAIN_SECTIONS)
        for first in firsts:
            cols = slice(first * HEAD_DIM, (first + heads_per_dot) * HEAD_DIM)
            acc = _dot(h_ref[rows, :], w_ref[:, cols])
            for u in range(heads_per_dot):
                section = section_of(first + u)
                a = acc[:, u * HEAD_DIM:(u + 1) * HEAD_DIM] * r_ref[rows, :]
                lanes = slice((first + u) * HEAD_DIM, (first + u + 1) * HEAD_DIM)
                o_ref[rows, lanes] = epilogue(section, a, rows).astype(BF16)
        return carry

    def tile(tile_index):
        lax.fori_loop(0, IN_PROJ_PARTS, functools.partial(tile_part, tile_index), 0)

    for tile_index in range(IN_PROJ_TILES):
        pl.when(j == tile_index)(functools.partial(tile, tile_index))


def _in_proj(x2d, g1, w_bf16, cos2, sin2, qg, kg, weights, *, seq, tm):
    m, d = x2d.shape
    n = w_bf16.shape[1]
    tn = n // IN_PROJ_TILES
    assert n == N_SECTIONS * HEADS_PER_SECTION * HEAD_DIM and tn % MXU_COLS == 0
    assert m % tm == 0 and seq % tm == 0
    blocks_per_seq = seq // tm
    n_slabs = (m // tm) * IN_PROJ_TILES

    def slab(w):
        rows, cols = w.shape
        assert rows % (n_slabs * IN_PROJ_PARTS * BF16_SUBLANES) == 0
        return pl.BlockSpec((rows // n_slabs, cols), lambda i, j: (i * IN_PROJ_TILES + j, 0))

    slabs = [slab(w) for w in weights]
    proj, *weights_bf16 = pl.pallas_call(
        functools.partial(_in_proj_kernel, n_cast=len(weights)),
        out_shape=[jax.ShapeDtypeStruct((m, n), BF16)]
                  + [jax.ShapeDtypeStruct(w.shape, BF16) for w in weights],
        grid=(m // tm, IN_PROJ_TILES),
        in_specs=[
            pl.BlockSpec((tm, d), lambda i, j: (i, 0)),
            pl.BlockSpec((1, d), lambda i, j: (0, 0)),
            pl.BlockSpec((d, tn), lambda i, j: (0, j)),
            pl.BlockSpec((tm, HEAD_DIM), lambda i, j: (i % blocks_per_seq, 0)),
            pl.BlockSpec((tm, HEAD_DIM), lambda i, j: (i % blocks_per_seq, 0)),
            pl.BlockSpec((1, HEAD_DIM), lambda i, j: (0, 0)),
            pl.BlockSpec((1, HEAD_DIM), lambda i, j: (0, 0)),
        ] + slabs,
        out_specs=[pl.BlockSpec((tm, tn), lambda i, j: (i, j))] + slabs,
        scratch_shapes=[pltpu.VMEM((tm, d), BF16), pltpu.VMEM((tm, HEAD_DIM), F32)],
        compiler_params=_params(2, V7X_VMEM_HIGH_LIMIT_BYTES),
        name="in_proj",
    )(x2d, g1, w_bf16, cos2, sin2, qg, kg, *weights)
    return proj, weights_bf16


def _bias_rows(rel_bias):
    h, rel_size = rel_bias.shape
    assert rel_size == CHUNK + REL_CLIP
    far = rel_bias[:, rel_size - 1:]
    near = rel_bias[:, :1]
    left = LEFT_CHUNKS * CHUNK - REL_CLIP
    return jnp.concatenate([
        jnp.broadcast_to(far, (h, left)),
        jnp.flip(rel_bias, axis=-1),
        jnp.broadcast_to(near, (h, WINDOW - left - rel_size)),
        jnp.broadcast_to(far, (h, TOEPLITZ_WIDTH - WINDOW)),
    ], axis=-1).astype(F32).reshape(h, 1, TOEPLITZ_WIDTH)


def _fill_bias_tile(g_ref, bias_ref):
    qc = lax.broadcasted_iota(jnp.int32, (SEQ_BLOCK, WINDOW), 0) // CHUNK
    kc = lax.broadcasted_iota(jnp.int32, (SEQ_BLOCK, WINDOW), 1) // CHUNK
    visible = (kc >= qc) & (kc <= qc + LEFT_CHUNKS)
    for h in range(ATT_HEADS):
        row = jnp.broadcast_to(g_ref[h], (SEQ_BLOCK, TOEPLITZ_WIDTH))
        t = pltpu.roll(row, 0, 1, stride=1, stride_axis=0)[:, :WINDOW]
        bias_ref[h] = jnp.where(visible, t * LOG2E, NEG)


def _retention_tables():
    h = np.arange(RET_HEADS, dtype=np.float64)
    log_g = np.log1p(-np.exp2(-(5.0 + h)))
    pos = np.arange(SEQ_BLOCK, dtype=np.float64)
    dist = np.abs(pos[:, None] - pos[None, :])
    vis = (pos[None, :] // CHUNK) <= (pos[:, None] // CHUNK)
    decay = np.exp(log_g[:, None, None] * dist) * vis
    q_dec = np.exp(log_g[None, :] * (pos + 1.0)[:, None])
    k_dec = np.exp(log_g[None, :] * (SEQ_BLOCK - 1.0 - pos)[:, None])
    blk_dec = np.exp(log_g * SEQ_BLOCK)
    lanes = lambda t: np.repeat(t, HEAD_DIM, axis=1).astype(np.float32)
    return decay.astype(np.float32), lanes(q_dec), lanes(k_dec), [float(v) for v in blk_dec]


def _retention_kernel(q_ref, k_ref, v_ref, gate_ref, dec_ref, qdec_ref, kdec_ref, gn_ref,
                      o_ref, st_ref, *, blocks_per_seq, blk_dec):
    @pl.when(pl.program_id(0) % blocks_per_seq == 0)
    def _():
        st_ref[...] = jnp.zeros_like(st_ref)

    for h in range(RET_HEADS):
        sl = slice(h * HEAD_DIM, (h + 1) * HEAD_DIM)
        state = st_ref[h]
        for blk in range(RET_BLOCKS):
            rows = slice(blk * SEQ_BLOCK, (blk + 1) * SEQ_BLOCK)
            q = q_ref[rows, sl]
            k = k_ref[rows, sl]
            v = v_ref[rows, sl]
            p = (_dot_nt(q, k) * dec_ref[h]).astype(BF16)
            y = _dot(p, v) + _dot(q, state.astype(BF16)) * qdec_ref[:, sl]
            kd = (k.astype(F32) * kdec_ref[:, sl]).astype(BF16)
            state = blk_dec[h] * state + _dot_tn(kd, v)
            mu = jnp.mean(y, axis=-1, keepdims=True)
            yc = y - mu
            var = jnp.mean(yc * yc, axis=-1, keepdims=True)
            yn = yc * lax.rsqrt(var + GN_EPS) * gn_ref[:, sl]
            gate = gate_ref[rows, sl].astype(F32)
            swish = gate * (0.5 + 0.5 * jnp.tanh(0.5 * gate))
            o_ref[rows, sl] = (swish * yn).astype(BF16)
        st_ref[h] = state


def _retention(proj, gn, *, seq):
    m = proj.shape[0]
    width = RET_HEADS * HEAD_DIM
    decay, q_dec, k_dec, blk_dec = _retention_tables()
    step_rows = RET_BLOCKS * SEQ_BLOCK
    assert seq % step_rows == 0
    blocks_per_seq = seq // step_rows
    section = lambda s: pl.BlockSpec((step_rows, width), lambda i: (i, s))
    whole = lambda shape: pl.BlockSpec(shape, lambda i: (0,) * len(shape))
    return pl.pallas_call(
        functools.partial(_retention_kernel, blocks_per_seq=blocks_per_seq, blk_dec=blk_dec),
        out_shape=jax.ShapeDtypeStruct((m, width), BF16),
        grid=(m // step_rows,),
        in_specs=[section(0), section(1), section(2), section(3),
                  whole(decay.shape), whole(q_dec.shape), whole(k_dec.shape), whole((1, width))],
        out_specs=pl.BlockSpec((step_rows, width), lambda i: (i, 0)),
        scratch_shapes=[pltpu.VMEM((RET_HEADS, HEAD_DIM, HEAD_DIM), F32)],
        compiler_params=_params(1),
        name="retention",
    )(proj, proj, proj, proj, decay, q_dec, k_dec, gn)


def _chunk_attn_kernel(q_ref, kp0_ref, kp1_ref, k_ref, vp0_ref, vp1_ref, v_ref, g_ref, *rest,
                       steps_per_seq, n_cast):
    w_refs, o_ref, wb_refs = rest[:n_cast], rest[n_cast], rest[n_cast + 1:2 * n_cast + 1]
    bias_ref = rest[2 * n_cast + 1]
    pl.when(pl.program_id(0) == 0)(functools.partial(_fill_bias_tile, g_ref, bias_ref))
    first_step = pl.program_id(0) % steps_per_seq == 0
    add = lambda a, b: a + b
    lane_tiles = SEQ_BLOCK // HEAD_DIM
    rows = lambda i0, i1: slice(i0 * SEQ_BLOCK, (i1 + 1) * SEQ_BLOCK)

    def key_block(prev_refs, main_ref, b, sl):
        if b < LEFT_BLOCKS:
            return prev_refs[b][:, sl]
        return main_ref[rows(b - LEFT_BLOCKS, b - LEFT_BLOCKS), sl]

    def attend(first_block):
        for w_ref, wb_ref in zip(w_refs, wb_refs):
            wb_ref[...] = w_ref[...].astype(BF16)
        key_blocks = range(first_block, ATT_BLOCKS + LEFT_BLOCKS)
        users = {b: (max(0, b - LEFT_BLOCKS), min(ATT_BLOCKS - 1, b)) for b in key_blocks}
        for h in range(ATT_HEADS):
            sl = slice(h * HEAD_DIM, (h + 1) * HEAD_DIM)
            tiles = {i: [] for i in range(ATT_BLOCKS)}
            for b in key_blocks:
                i0, i1 = users[b]
                s = _dot_nt(q_ref[rows(i0, i1), sl], key_block((kp0_ref, kp1_ref), k_ref, b, sl))
                for i in range(i0, i1 + 1):
                    w = b - i
                    sb = (s[rows(i - i0, i - i0), :]
                          + bias_ref[h, :, w * SEQ_BLOCK:(w + 1) * SEQ_BLOCK])
                    tiles[i] += [sb[:, c * HEAD_DIM:(c + 1) * HEAD_DIM] for c in range(lane_tiles)]
            probs, denom = {}, {}
            for i in range(ATT_BLOCKS):
                mx = jnp.max(functools.reduce(jnp.maximum, tiles[i]), axis=-1, keepdims=True)
                p = [jnp.exp2(tile - mx) for tile in tiles[i]]
                denom[i] = jnp.sum(functools.reduce(add, p), axis=-1, keepdims=True)
                visible = [b for b in key_blocks if users[b][0] <= i <= users[b][1]]
                for n, b in enumerate(visible):
                    probs[i, b] = jnp.concatenate(
                        p[n * lane_tiles:(n + 1) * lane_tiles], axis=1).astype(BF16)
            out = {i: [] for i in range(ATT_BLOCKS)}
            for b in key_blocks:
                i0, i1 = users[b]
                lhs = jnp.concatenate([probs[i, b] for i in range(i0, i1 + 1)], axis=0)
                pv = _dot(lhs, key_block((vp0_ref, vp1_ref), v_ref, b, sl))
                for i in range(i0, i1 + 1):
                    out[i].append(pv[rows(i - i0, i - i0), :])
            for i in range(ATT_BLOCKS):
                o_ref[rows(i, i), sl] = (functools.reduce(add, out[i]) / denom[i]).astype(BF16)

    pl.when(first_step)(functools.partial(attend, LEFT_BLOCKS))
    pl.when(jnp.logical_not(first_step))(functools.partial(attend, 0))


def _chunk_attn(proj, bias_rows, weights, *, seq):
    m = proj.shape[0]
    width = ATT_HEADS * HEAD_DIM
    step_rows = ATT_BLOCKS * SEQ_BLOCK
    assert seq % step_rows == 0 and ATT_BLOCKS >= LEFT_BLOCKS
    steps_per_seq = seq // step_rows
    steps = m // step_rows

    def previous(section, b):
        def index(i):
            seq_start = (i - i % steps_per_seq) * ATT_BLOCKS
            return (jnp.maximum(i * ATT_BLOCKS - (LEFT_BLOCKS - b), seq_start), section)
        return pl.BlockSpec((SEQ_BLOCK, width), index)

    main = lambda section: pl.BlockSpec((step_rows, width), lambda i: (i, section))

    def slab(w):
        rows, cols = w.shape
        assert rows % (steps * BF16_SUBLANES) == 0
        return pl.BlockSpec((rows // steps, cols), lambda i: (i, 0))

    att, *weights_bf16 = pl.pallas_call(
        functools.partial(_chunk_attn_kernel, steps_per_seq=steps_per_seq, n_cast=len(weights)),
        out_shape=[jax.ShapeDtypeStruct((m, width), BF16)]
                  + [jax.ShapeDtypeStruct(w.shape, BF16) for w in weights],
        grid=(steps,),
        in_specs=[main(4)]
                 + [previous(5, b) for b in range(LEFT_BLOCKS)] + [main(5)]
                 + [previous(6, b) for b in range(LEFT_BLOCKS)] + [main(6)]
                 + [pl.BlockSpec(bias_rows.shape, lambda i: (0, 0, 0))]
                 + [slab(w) for w in weights],
        out_specs=[pl.BlockSpec((step_rows, width), lambda i: (i, 0))] + [slab(w) for w in weights],
        scratch_shapes=[pltpu.VMEM((ATT_HEADS, SEQ_BLOCK, WINDOW), F32)],
        compiler_params=_params(1),
        name="chunk_attn",
    )(proj, proj, proj, proj, proj, proj, proj, bias_rows, *weights)
    return att, weights_bf16


def _out_proj_kernel(ret_ref, att_ref, wr_ref, wa_ref, x_ref, o_ref):
    o_ref[...] = x_ref[...] + (_dot(ret_ref[...], wr_ref[...]) + _dot(att_ref[...], wa_ref[...]))


def _out_proj(ret, att, w_bf16, x2d, *, tm, tn):
    m, d = x2d.shape
    kr, ka = ret.shape[1], att.shape[1]
    assert kr == ka and kr + ka == w_bf16.shape[0] and m % tm == 0 and d % tn == 0
    return pl.pallas_call(
        _out_proj_kernel,
        out_shape=jax.ShapeDtypeStruct((m, d), F32),
        grid=(m // tm, d // tn),
        in_specs=[
            pl.BlockSpec((tm, kr), lambda i, j: (i, 0)),
            pl.BlockSpec((tm, ka), lambda i, j: (i, 0)),
            pl.BlockSpec((kr, tn), lambda i, j: (0, j)),
            pl.BlockSpec((ka, tn), lambda i, j: (1, j)),
            pl.BlockSpec((tm, tn), lambda i, j: (i, j)),
        ],
        out_specs=pl.BlockSpec((tm, tn), lambda i, j: (i, j)),
        compiler_params=_params(2),
        name="out_proj",
    )(ret, att, w_bf16, w_bf16, x2d)


def _ffn_kernel(x_ref, g_ref, w1_ref, w2_ref, o_ref, h_ref, r_ref):
    f = pl.program_id(1)

    def chunk():
        a = _dot(h_ref[...], w1_ref[...])
        hidden = []
        for c in range(a.shape[1] // HEAD_DIM):
            act = jnp.maximum(a[:, c * HEAD_DIM:(c + 1) * HEAD_DIM] * r_ref[...], 0.0)
            hidden.append((act * act).astype(BF16))
        o_ref[...] += _dot(jnp.concatenate(hidden, axis=1), w2_ref[...])

    @pl.when(f == 0)
    def _():
        x = x_ref[...]
        ms = jnp.mean(x * x, axis=-1, keepdims=True)
        r_ref[...] = jnp.broadcast_to(lax.rsqrt(ms + EPS), r_ref.shape)
        h_ref[...] = (x * g_ref[...]).astype(BF16)
        o_ref[...] = x
        chunk()

    pl.when(f > 0)(chunk)


def _ffn(x2d, g2, w1_bf16, w2_bf16, *, tm, tf):
    m, d = x2d.shape
    d_ff = w1_bf16.shape[1]
    assert m % tm == 0 and d_ff % tf == 0
    return pl.pallas_call(
        _ffn_kernel,
        out_shape=jax.ShapeDtypeStruct((m, d), F32),
        grid=(m // tm, d_ff // tf),
        in_specs=[
            pl.BlockSpec((tm, d), lambda i, f: (i, 0)),
            pl.BlockSpec((1, d), lambda i, f: (0, 0)),
            pl.BlockSpec((d, tf), lambda i, f: (0, f)),
            pl.BlockSpec((tf, d), lambda i, f: (f, 0)),
        ],
        out_specs=pl.BlockSpec((tm, d), lambda i, f: (i, 0)),
        scratch_shapes=[pltpu.VMEM((tm, d), BF16), pltpu.VMEM((tm, HEAD_DIM), F32)],
        compiler_params=_params(2, V7X_VMEM_HIGH_LIMIT_BYTES),
        name="ffn",
    )(x2d, g2, w1_bf16, w2_bf16)


def _rotary_tables(seq):
    half = HEAD_DIM // 2
    inv_freq = ROPE_BASE ** (-np.arange(half, dtype=np.float64) / half)
    ang = np.arange(seq, dtype=np.float64)[:, None] * inv_freq[None, :]
    cos, sin = np.cos(ang), np.sin(ang)
    cos2 = np.concatenate([cos, cos], axis=1).astype(np.float32)
    sin2 = np.concatenate([-sin, sin], axis=1).astype(np.float32)
    return cos2, sin2


def kernel(x, norm1_g, w_in, ret_norm_g, q_norm_g, k_norm_g, rel_bias, w_out, norm2_g, w_ff1, w_ff2):
    b, s, d = x.shape
    depth = w_in.shape[0]
    assert s % SEQ_BLOCK == 0 and SEQ_BLOCK % CHUNK == 0
    assert RET_HEADS == HEADS_PER_SECTION and ATT_HEADS == HEADS_PER_SECTION
    cos2, sin2 = _rotary_tables(s)
    row = lambda v: v.astype(F32).reshape(1, -1)
    x2d = x.reshape(b * s, d)
    for layer in range(depth):
        proj, (w1, w2) = _in_proj(x2d, row(norm1_g[layer]), w_in[layer].astype(BF16), cos2, sin2,
                                  row(q_norm_g[layer]), row(k_norm_g[layer]),
                                  (w_ff1[layer], w_ff2[layer]), seq=s, tm=IN_PROJ_ROWS)
        att, (wo,) = _chunk_attn(proj, _bias_rows(rel_bias[layer]), (w_out[layer],), seq=s)
        ret = _retention(proj, row(ret_norm_g[layer]), seq=s)
        x2d = _out_proj(ret, att, wo, x2d, tm=OUT_PROJ_ROWS, tn=d)
        x2d = _ffn(x2d, row(norm2_g[layer]), w1, w2, tm=FFN_ROWS, tf=FFN_CHUNK)
    return x2d.reshape(b, s, d)
```

```python
import functools
import math

import numpy as np
import jax
import jax.numpy as jnp
from jax import lax
from jax.experimental import pallas as pl
from jax.experimental.pallas import tpu as pltpu

CHUNK = 64
LEFT_CHUNKS = 8
REL_CLIP = 128
HEAD_DIM = 128
RET_HEADS = 8
ATT_HEADS = 8
ROPE_BASE = 10000.0
EPS = 1e-6
GN_EPS = 1e-5
NEG = -1e30
LOG2E = math.log2(math.e)

V7X_VMEM_BYTES = 64 * 1024 * 1024
V7X_VMEM_LIMIT_BYTES = V7X_VMEM_BYTES - 8 * 1024 * 1024
V7X_VMEM_HIGH_LIMIT_BYTES = V7X_VMEM_BYTES - 4 * 1024 * 1024

SEQ_BLOCK = 256
LEFT_BLOCKS = LEFT_CHUNKS * CHUNK // SEQ_BLOCK
WINDOW = (LEFT_BLOCKS + 1) * SEQ_BLOCK
ATT_BLOCKS = 4
ATT_HEAD_GROUPS = 2
RET_BLOCKS = 4
TOEPLITZ_WIDTH = 1024
IN_PROJ_ROWS = 1024
OUT_PROJ_ROWS = 512
FFN_ROWS = 1024
FFN_CHUNK = 1024

F32 = jnp.float32
BF16 = jnp.bfloat16
BF16_SUBLANES = 16


def _params(n_axes, vmem_bytes=V7X_VMEM_LIMIT_BYTES):
    return pltpu.CompilerParams(
        dimension_semantics=("arbitrary",) * n_axes, vmem_limit_bytes=vmem_bytes)


def _dot(a, b):
    return jnp.dot(a, b, preferred_element_type=F32)


def _dot_nt(a, b):
    return lax.dot_general(a, b, (((1,), (1,)), ((), ())), preferred_element_type=F32)


def _dot_tn(a, b):
    return lax.dot_general(a, b, (((0,), (0,)), ((), ())), preferred_element_type=F32)


N_SECTIONS = 7
PLAIN_SECTIONS = (2, 3, 6)
HEADS_PER_SECTION = 8
IN_PROJ_TILES = 4
IN_PROJ_PARTS = 2
MXU_COLS = 256


def _in_proj_kernel(x_ref, g_ref, w_ref, cos_ref, sin_ref, qg_ref, kg_ref, *rest, n_cast):
    w_refs, o_ref, wb_refs = rest[:n_cast], rest[n_cast], rest[n_cast + 1:2 * n_cast + 1]
    h_ref, r_ref = rest[2 * n_cast + 1:]
    j = pl.program_id(1)
    heads_per_tile = o_ref.shape[1] // HEAD_DIM
    heads_per_dot = MXU_COLS // HEAD_DIM
    scale = HEAD_DIM ** -0.5

    def epilogue(section, a, rows):
        if section in (0, 1):
            y = a * cos_ref[rows, :] + pltpu.roll(a, HEAD_DIM // 2, 1) * sin_ref[rows, :]
            return y * scale if section == 1 else y
        if section in (4, 5):
            ms = jnp.mean(a * a, axis=-1, keepdims=True)
            y = a * lax.rsqrt(ms + EPS) * (qg_ref if section == 4 else kg_ref)[...]
            return y * (scale * LOG2E) if section == 4 else y
        return a

    def part_of(ref, part):
        n = ref.shape[0] // IN_PROJ_PARTS
        return pl.ds(pl.multiple_of(part * n, n), n)

    def tile_part(tile_index, part, carry):
        rows = part_of(h_ref, part)
        if tile_index == 0:
            x = x_ref[rows, :]
            ms = jnp.mean(x * x, axis=-1, keepdims=True)
            r_ref[rows, :] = jnp.broadcast_to(lax.rsqrt(ms + EPS), (x.shape[0], HEAD_DIM))
            h_ref[rows, :] = (x * g_ref[...]).astype(BF16)
        for src_ref, dst_ref in zip(w_refs, wb_refs):
            slab_rows = part_of(src_ref, part)
            dst_ref[slab_rows, :] = src_ref[slab_rows, :].astype(BF16)
        section_of = lambda head: (tile_index * heads_per_tile + head) // HEADS_PER_SECTION
        firsts = sorted(range(0, heads_per_tile, heads_per_dot),
                        key=lambda first: section_of(first) in PLAIN_SECTIONS)
        for first in firsts:
            cols = slice(first * HEAD_DIM, (first + heads_per_dot) * HEAD_DIM)
            acc = _dot(h_ref[rows, :], w_ref[:, cols])
            for u in range(heads_per_dot):
                section = section_of(first + u)
                a = acc[:, u * HEAD_DIM:(u + 1) * HEAD_DIM] * r_ref[rows, :]
                lanes = slice((first + u) * HEAD_DIM, (first + u + 1) * HEAD_DIM)
                o_ref[rows, lanes] = epilogue(section, a, rows).astype(BF16)
        return carry

    def tile(tile_index):
        lax.fori_loop(0, IN_PROJ_PARTS, functools.partial(tile_part, tile_index), 0)

    for tile_index in range(IN_PROJ_TILES):
        pl.when(j == tile_index)(functools.partial(tile, tile_index))


def _in_proj(x2d, g1, w_bf16, cos2, sin2, qg, kg, weights, *, seq, tm):
    m, d = x2d.shape
    n = w_bf16.shape[1]
    tn = n // IN_PROJ_TILES
    assert n == N_SECTIONS * HEADS_PER_SECTION * HEAD_DIM and tn % MXU_COLS == 0
    assert m % tm == 0 and seq % tm == 0
    blocks_per_seq = seq // tm
    n_slabs = (m // tm) * IN_PROJ_TILES

    def slab(w):
        rows, cols = w.shape
        assert rows % (n_slabs * IN_PROJ_PARTS * BF16_SUBLANES) == 0
        return pl.BlockSpec((rows // n_slabs, cols), lambda i, j: (i * IN_PROJ_TILES + j, 0))

    slabs = [slab(w) for w in weights]
    proj, *weights_bf16 = pl.pallas_call(
        functools.partial(_in_proj_kernel, n_cast=len(weights)),
        out_shape=[jax.ShapeDtypeStruct((m, n), BF16)]
                  + [jax.ShapeDtypeStruct(w.shape, BF16) for w in weights],
        grid=(m // tm, IN_PROJ_TILES),
        in_specs=[
            pl.BlockSpec((tm, d), lambda i, j: (i, 0)),
            pl.BlockSpec((1, d), lambda i, j: (0, 0)),
            pl.BlockSpec((d, tn), lambda i, j: (0, j)),
            pl.BlockSpec((tm, HEAD_DIM), lambda i, j: (i % blocks_per_seq, 0)),
            pl.BlockSpec((tm, HEAD_DIM), lambda i, j: (i % blocks_per_seq, 0)),
            pl.BlockSpec((1, HEAD_DIM), lambda i, j: (0, 0)),
            pl.BlockSpec((1, HEAD_DIM), lambda i, j: (0, 0)),
        ] + slabs,
        out_specs=[pl.BlockSpec((tm, tn), lambda i, j: (i, j))] + slabs,
        scratch_shapes=[pltpu.VMEM((tm, d), BF16), pltpu.VMEM((tm, HEAD_DIM), F32)],
        compiler_params=_params(2, V7X_VMEM_HIGH_LIMIT_BYTES),
        name="in_proj",
    )(x2d, g1, w_bf16, cos2, sin2, qg, kg, *weights)
    return proj, weights_bf16


def _bias_rows(rel_bias):
    h, rel_size = rel_bias.shape
    assert rel_size == CHUNK + REL_CLIP
    far = rel_bias[:, rel_size - 1:]
    near = rel_bias[:, :1]
    left = LEFT_CHUNKS * CHUNK - REL_CLIP
    return jnp.concatenate([
        jnp.broadcast_to(far, (h, left)),
        jnp.flip(rel_bias, axis=-1),
        jnp.broadcast_to(near, (h, WINDOW - left - rel_size)),
        jnp.broadcast_to(far, (h, TOEPLITZ_WIDTH - WINDOW)),
    ], axis=-1).astype(F32).reshape(h, 1, TOEPLITZ_WIDTH)


def _fill_bias_tile(g_ref, bias_ref):
    qc = lax.broadcasted_iota(jnp.int32, (SEQ_BLOCK, WINDOW), 0) // CHUNK
    kc = lax.broadcasted_iota(jnp.int32, (SEQ_BLOCK, WINDOW), 1) // CHUNK
    visible = (kc >= qc) & (kc <= qc + LEFT_CHUNKS)
    for h in range(ATT_HEADS):
        row = jnp.broadcast_to(g_ref[h], (SEQ_BLOCK, TOEPLITZ_WIDTH))
        t = pltpu.roll(row, 0, 1, stride=1, stride_axis=0)[:, :WINDOW]
        bias_ref[h] = jnp.where(visible, t * LOG2E, NEG)


def _retention_tables():
    h = np.arange(RET_HEADS, dtype=np.float64)
    log_g = np.log1p(-np.exp2(-(5.0 + h)))
    pos = np.arange(SEQ_BLOCK, dtype=np.float64)
    dist = np.abs(pos[:, None] - pos[None, :])
    vis = (pos[None, :] // CHUNK) <= (pos[:, None] // CHUNK)
    decay = np.exp(log_g[:, None, None] * dist) * vis
    q_dec = np.exp(log_g[None, :] * (pos + 1.0)[:, None])
    k_dec = np.exp(log_g[None, :] * (SEQ_BLOCK - 1.0 - pos)[:, None])
    blk_dec = np.exp(log_g * SEQ_BLOCK)
    lanes = lambda t: np.repeat(t, HEAD_DIM, axis=1).astype(np.float32)
    return decay.astype(np.float32), lanes(q_dec), lanes(k_dec), [float(v) for v in blk_dec]


def _retention_kernel(q_ref, k_ref, v_ref, gate_ref, dec_ref, qdec_ref, kdec_ref, gn_ref,
                      o_ref, st_ref, *, blocks_per_seq, blk_dec):
    @pl.when(pl.program_id(0) % blocks_per_seq == 0)
    def _():
        st_ref[...] = jnp.zeros_like(st_ref)

    for h in range(RET_HEADS):
        sl = slice(h * HEAD_DIM, (h + 1) * HEAD_DIM)
        state = st_ref[h]
        for blk in range(RET_BLOCKS):
            rows = slice(blk * SEQ_BLOCK, (blk + 1) * SEQ_BLOCK)
            q = q_ref[rows, sl]
            k = k_ref[rows, sl]
            v = v_ref[rows, sl]
            p = (_dot_nt(q, k) * dec_ref[h]).astype(BF16)
            y = _dot(p, v) + _dot(q, state.astype(BF16)) * qdec_ref[:, sl]
            kd = (k.astype(F32) * kdec_ref[:, sl]).astype(BF16)
            state = blk_dec[h] * state + _dot_tn(kd, v)
            mu = jnp.mean(y, axis=-1, keepdims=True)
            yc = y - mu
            var = jnp.mean(yc * yc, axis=-1, keepdims=True)
            yn = yc * lax.rsqrt(var + GN_EPS) * gn_ref[:, sl]
            gate = gate_ref[rows, sl].astype(F32)
            swish = gate * (0.5 + 0.5 * jnp.tanh(0.5 * gate))
            o_ref[rows, sl] = (swish * yn).astype(BF16)
        st_ref[h] = state


def _retention(proj, gn, *, seq):
    m = proj.shape[0]
    width = RET_HEADS * HEAD_DIM
    decay, q_dec, k_dec, blk_dec = _retention_tables()
    step_rows = RET_BLOCKS * SEQ_BLOCK
    assert seq % step_rows == 0
    blocks_per_seq = seq // step_rows
    section = lambda s: pl.BlockSpec((step_rows, width), lambda i: (i, s))
    whole = lambda shape: pl.BlockSpec(shape, lambda i: (0,) * len(shape))
    return pl.pallas_call(
        functools.partial(_retention_kernel, blocks_per_seq=blocks_per_seq, blk_dec=blk_dec),
        out_shape=jax.ShapeDtypeStruct((m, width), BF16),
        grid=(m // step_rows,),
        in_specs=[section(0), section(1), section(2), section(3),
                  whole(decay.shape), whole(q_dec.shape), whole(k_dec.shape), whole((1, width))],
        out_specs=pl.BlockSpec((step_rows, width), lambda i: (i, 0)),
        scratch_shapes=[pltpu.VMEM((RET_HEADS, HEAD_DIM, HEAD_DIM), F32)],
        compiler_params=_params(1),
        name="retention",
    )(proj, proj, proj, proj, decay, q_dec, k_dec, gn)


def _chunk_attn_kernel(q_ref, kp0_ref, kp1_ref, k_ref, vp0_ref, vp1_ref, v_ref, g_ref, *rest,
                       steps_per_seq, n_cast):
    w_refs, o_ref, wb_refs = rest[:n_cast], rest[n_cast], rest[n_cast + 1:2 * n_cast + 1]
    bias_ref = rest[2 * n_cast + 1]
    pl.when(pl.program_id(0) == 0)(functools.partial(_fill_bias_tile, g_ref, bias_ref))
    first_step = pl.program_id(0) % steps_per_seq == 0
    add = lambda a, b: a + b
    lane_tiles = SEQ_BLOCK // HEAD_DIM
    rows = lambda i0, i1: slice(i0 * SEQ_BLOCK, (i1 + 1) * SEQ_BLOCK)

    def key_block(prev_refs, main_ref, b, sl):
        if b < LEFT_BLOCKS:
            return prev_refs[b][:, sl]
        return main_ref[rows(b - LEFT_BLOCKS, b - LEFT_BLOCKS), sl]

    def attend_heads(first_block, group, carry):
        for w_ref, wb_ref in zip(w_refs, wb_refs):
            n = w_ref.shape[0] // ATT_HEAD_GROUPS
            slab_rows = pl.ds(pl.multiple_of(group * n, n), n)
            wb_ref[slab_rows, :] = w_ref[slab_rows, :].astype(BF16)
        key_blocks = range(first_block, ATT_BLOCKS + LEFT_BLOCKS)
        users = {b: (max(0, b - LEFT_BLOCKS), min(ATT_BLOCKS - 1, b)) for b in key_blocks}
        heads_per_group = ATT_HEADS // ATT_HEAD_GROUPS
        for head_in_group in range(heads_per_group):
            h = group * heads_per_group + head_in_group
            sl = pl.ds(pl.multiple_of(h * HEAD_DIM, HEAD_DIM), HEAD_DIM)
            tiles = {i: [] for i in range(ATT_BLOCKS)}
            for b in key_blocks:
                i0, i1 = users[b]
                s = _dot_nt(q_ref[rows(i0, i1), sl], key_block((kp0_ref, kp1_ref), k_ref, b, sl))
                for i in range(i0, i1 + 1):
                    w = b - i
                    sb = (s[rows(i - i0, i - i0), :]
                          + bias_ref[h, :, w * SEQ_BLOCK:(w + 1) * SEQ_BLOCK])
                    tiles[i] += [sb[:, c * HEAD_DIM:(c + 1) * HEAD_DIM] for c in range(lane_tiles)]
            probs, denom = {}, {}
            for i in range(ATT_BLOCKS):
                mx = jnp.max(functools.reduce(jnp.maximum, tiles[i]), axis=-1, keepdims=True)
                p = [jnp.exp2(tile - mx) for tile in tiles[i]]
                denom[i] = jnp.sum(functools.reduce(add, p), axis=-1, keepdims=True)
                visible = [b for b in key_blocks if users[b][0] <= i <= users[b][1]]
                for n, b in enumerate(visible):
                    probs[i, b] = jnp.concatenate(
                        p[n * lane_tiles:(n + 1) * lane_tiles], axis=1).astype(BF16)
            out = {i: [] for i in range(ATT_BLOCKS)}
            for b in key_blocks:
                i0, i1 = users[b]
                lhs = jnp.concatenate([probs[i, b] for i in range(i0, i1 + 1)], axis=0)
                pv = _dot(lhs, key_block((vp0_ref, vp1_ref), v_ref, b, sl))
                for i in range(i0, i1 + 1):
                    out[i].append(pv[rows(i - i0, i - i0), :])
            for i in range(ATT_BLOCKS):
                o_ref[rows(i, i), sl] = (functools.reduce(add, out[i]) / denom[i]).astype(BF16)
        return carry

    def attend(first_block):
        lax.fori_loop(0, ATT_HEAD_GROUPS, functools.partial(attend_heads, first_block), 0)

    pl.when(first_step)(functools.partial(attend, LEFT_BLOCKS))
    pl.when(jnp.logical_not(first_step))(functools.partial(attend, 0))


def _chunk_attn(proj, bias_rows, weights, *, seq):
    m = proj.shape[0]
    width = ATT_HEADS * HEAD_DIM
    step_rows = ATT_BLOCKS * SEQ_BLOCK
    assert seq % step_rows == 0 and ATT_BLOCKS >= LEFT_BLOCKS
    steps_per_seq = seq // step_rows
    steps = m // step_rows

    def previous(section, b):
        def index(i):
            seq_start = (i - i % steps_per_seq) * ATT_BLOCKS
            return (jnp.maximum(i * ATT_BLOCKS - (LEFT_BLOCKS - b), seq_start), section)
        return pl.BlockSpec((SEQ_BLOCK, width), index)

    main = lambda section: pl.BlockSpec((step_rows, width), lambda i: (i, section))

    def slab(w):
        rows, cols = w.shape
        assert rows % (steps * BF16_SUBLANES) == 0
        return pl.BlockSpec((rows // steps, cols), lambda i: (i, 0))

    att, *weights_bf16 = pl.pallas_call(
        functools.partial(_chunk_attn_kernel, steps_per_seq=steps_per_seq, n_cast=len(weights)),
        out_shape=[jax.ShapeDtypeStruct((m, width), BF16)]
                  + [jax.ShapeDtypeStruct(w.shape, BF16) for w in weights],
        grid=(steps,),
        in_specs=[main(4)]
                 + [previous(5, b) for b in range(LEFT_BLOCKS)] + [main(5)]
                 + [previous(6, b) for b in range(LEFT_BLOCKS)] + [main(6)]
                 + [pl.BlockSpec(bias_rows.shape, lambda i: (0, 0, 0))]
                 + [slab(w) for w in weights],
        out_specs=[pl.BlockSpec((step_rows, width), lambda i: (i, 0))] + [slab(w) for w in weights],
        scratch_shapes=[pltpu.VMEM((ATT_HEADS, SEQ_BLOCK, WINDOW), F32)],
        compiler_params=_params(1),
        name="chunk_attn",
    )(proj, proj, proj, proj, proj, proj, proj, bias_rows, *weights)
    return att, weights_bf16


def _out_proj_kernel(ret_ref, att_ref, wr_ref, wa_ref, x_ref, o_ref):
    o_ref[...] = x_ref[...] + (_dot(ret_ref[...], wr_ref[...]) + _dot(att_ref[...], wa_ref[...]))


def _out_proj(ret, att, w_bf16, x2d, *, tm, tn):
    m, d = x2d.shape
    kr, ka = ret.shape[1], att.shape[1]
    assert kr == ka and kr + ka == w_bf16.shape[0] and m % tm == 0 and d % tn == 0
    return pl.pallas_call(
        _out_proj_kernel,
        out_shape=jax.ShapeDtypeStruct((m, d), F32),
        grid=(m // tm, d // tn),
        in_specs=[
            pl.BlockSpec((tm, kr), lambda i, j: (i, 0)),
            pl.BlockSpec((tm, ka), lambda i, j: (i, 0)),
            pl.BlockSpec((kr, tn), lambda i, j: (0, j)),
            pl.BlockSpec((ka, tn), lambda i, j: (1, j)),
            pl.BlockSpec((tm, tn), lambda i, j: (i, j)),
        ],
        out_specs=pl.BlockSpec((tm, tn), lambda i, j: (i, j)),
        compiler_params=_params(2),
        name="out_proj",
    )(ret, att, w_bf16, w_bf16, x2d)


def _ffn_kernel(x_ref, g_ref, w1_ref, w2_ref, o_ref, h_ref, r_ref):
    f = pl.program_id(1)

    def chunk():
        a = _dot(h_ref[...], w1_ref[...])
        hidden = []
        for c in range(a.shape[1] // HEAD_DIM):
            act = jnp.maximum(a[:, c * HEAD_DIM:(c + 1) * HEAD_DIM] * r_ref[...], 0.0)
            hidden.append((act * act).astype(BF16))
        o_ref[...] += _dot(jnp.concatenate(hidden, axis=1), w2_ref[...])

    @pl.when(f == 0)
    def _():
        x = x_ref[...]
        ms = jnp.mean(x * x, axis=-1, keepdims=True)
        r_ref[...] = jnp.broadcast_to(lax.rsqrt(ms + EPS), r_ref.shape)
        h_ref[...] = (x * g_ref[...]).astype(BF16)
        o_ref[...] = x
        chunk()

    pl.when(f > 0)(chunk)


def _ffn(x2d, g2, w1_bf16, w2_bf16, *, tm, tf):
    m, d = x2d.shape
    d_ff = w1_bf16.shape[1]
    assert m % tm == 0 and d_ff % tf == 0
    return pl.pallas_call(
        _ffn_kernel,
        out_shape=jax.ShapeDtypeStruct((m, d), F32),
        grid=(m // tm, d_ff // tf),
        in_specs=[
            pl.BlockSpec((tm, d), lambda i, f: (i, 0)),
            pl.BlockSpec((1, d), lambda i, f: (0, 0)),
            pl.BlockSpec((d, tf), lambda i, f: (0, f)),
            pl.BlockSpec((tf, d), lambda i, f: (f, 0)),
        ],
        out_specs=pl.BlockSpec((tm, d), lambda i, f: (i, 0)),
        scratch_shapes=[pltpu.VMEM((tm, d), BF16), pltpu.VMEM((tm, HEAD_DIM), F32)],
        compiler_params=_params(2, V7X_VMEM_HIGH_LIMIT_BYTES),
        name="ffn",
    )(x2d, g2, w1_bf16, w2_bf16)


def _rotary_tables(seq):
    half = HEAD_DIM // 2
    inv_freq = ROPE_BASE ** (-np.arange(half, dtype=np.float64) / half)
    ang = np.arange(seq, dtype=np.float64)[:, None] * inv_freq[None, :]
    cos, sin = np.cos(ang), np.sin(ang)
    cos2 = np.concatenate([cos, cos], axis=1).astype(np.float32)
    sin2 = np.concatenate([-sin, sin], axis=1).astype(np.float32)
    return cos2, sin2


def kernel(x, norm1_g, w_in, ret_norm_g, q_norm_g, k_norm_g, rel_bias, w_out, norm2_g, w_ff1, w_ff2):
    b, s, d = x.shape
    depth = w_in.shape[0]
    assert s % SEQ_BLOCK == 0 and SEQ_BLOCK % CHUNK == 0
    assert RET_HEADS == HEADS_PER_SECTION and ATT_HEADS == HEADS_PER_SECTION
    cos2, sin2 = _rotary_tables(s)
    row = lambda v: v.astype(F32).reshape(1, -1)
    x2d = x.reshape(b * s, d)
    for layer in range(depth):
        proj, (w1, w2) = _in_proj(x2d, row(norm1_g[layer]), w_in[layer].astype(BF16), cos2, sin2,
                                  row(q_norm_g[layer]), row(k_norm_g[layer]),
                                  (w_ff1[layer], w_ff2[layer]), seq=s, tm=IN_PROJ_ROWS)
        ret = _retention(proj, row(ret_norm_g[layer]), seq=s)
        att, (wo,) = _chunk_attn(proj, _bias_rows(rel_bias[layer]), (w_out[layer],), seq=s)
        x2d = _out_proj(ret, att, wo, x2d, tm=OUT_PROJ_ROWS, tn=d)
        x2d = _ffn(x2d, row(norm2_g[layer]), w1, w2, tm=FFN_ROWS, tf=FFN_CHUNK)
    return x2d.reshape(b, s, d)
```

```python
import functools
import math

import numpy as np
import jax
import jax.numpy as jnp
from jax import lax
from jax.experimental import pallas as pl
from jax.experimental.pallas import tpu as pltpu

CHUNK = 64
LEFT_CHUNKS = 8
REL_CLIP = 128
HEAD_DIM = 128
RET_HEADS = 8
ATT_HEADS = 8
ROPE_BASE = 10000.0
EPS = 1e-6
GN_EPS = 1e-5
NEG = -1e30
LOG2E = math.log2(math.e)

V7X_VMEM_BYTES = 64 * 1024 * 1024
V7X_VMEM_LIMIT_BYTES = V7X_VMEM_BYTES - 8 * 1024 * 1024
V7X_VMEM_HIGH_LIMIT_BYTES = V7X_VMEM_BYTES - 4 * 1024 * 1024

SEQ_BLOCK = 256
LEFT_BLOCKS = LEFT_CHUNKS * CHUNK // SEQ_BLOCK
WINDOW = (LEFT_BLOCKS + 1) * SEQ_BLOCK
ATT_BLOCKS = 4
ATT_HEAD_GROUPS = 2
RET_BLOCKS = 4
TOEPLITZ_WIDTH = 1024
IN_PROJ_ROWS = 1024
OUT_PROJ_ROWS = 1024
FFN_ROWS = 1024
FFN_CHUNK = 1024

F32 = jnp.float32
BF16 = jnp.bfloat16
BF16_SUBLANES = 16


def _params(n_axes, vmem_bytes=V7X_VMEM_LIMIT_BYTES):
    return pltpu.CompilerParams(
        dimension_semantics=("arbitrary",) * n_axes, vmem_limit_bytes=vmem_bytes)


def _dot(a, b):
    return jnp.dot(a, b, preferred_element_type=F32)


def _dot_nt(a, b):
    return lax.dot_general(a, b, (((1,), (1,)), ((), ())), preferred_element_type=F32)


def _dot_tn(a, b):
    return lax.dot_general(a, b, (((0,), (0,)), ((), ())), preferred_element_type=F32)


N_SECTIONS = 7
PLAIN_SECTIONS = (2, 3, 6)
HEADS_PER_SECTION = 8
IN_PROJ_TILES = 4
IN_PROJ_PARTS = 2
MXU_COLS = 256


def _in_proj_kernel(x_ref, g_ref, w_ref, cos_ref, sin_ref, qg_ref, kg_ref, *rest, n_cast):
    w_refs, o_ref, wb_refs = rest[:n_cast], rest[n_cast], rest[n_cast + 1:2 * n_cast + 1]
    h_ref, r_ref = rest[2 * n_cast + 1:]
    j = pl.program_id(1)
    heads_per_tile = o_ref.shape[1] // HEAD_DIM
    heads_per_dot = MXU_COLS // HEAD_DIM
    scale = HEAD_DIM ** -0.5

    def epilogue(section, a, rows):
        if section in (0, 1):
            y = a * cos_ref[rows, :] + pltpu.roll(a, HEAD_DIM // 2, 1) * sin_ref[rows, :]
            return y * scale if section == 1 else y
        if section in (4, 5):
            ms = jnp.mean(a * a, axis=-1, keepdims=True)
            y = a * lax.rsqrt(ms + EPS) * (qg_ref if section == 4 else kg_ref)[...]
            return y * (scale * LOG2E) if section == 4 else y
        return a

    def part_of(ref, part):
        n = ref.shape[0] // IN_PROJ_PARTS
        return pl.ds(pl.multiple_of(part * n, n), n)

    def tile_part(tile_index, part, carry):
        rows = part_of(h_ref, part)
        if tile_index == 0:
            x = x_ref[rows, :]
            ms = jnp.mean(x * x, axis=-1, keepdims=True)
            r_ref[rows, :] = jnp.broadcast_to(lax.rsqrt(ms + EPS), (x.shape[0], HEAD_DIM))
            h_ref[rows, :] = (x * g_ref[...]).astype(BF16)
        for src_ref, dst_ref in zip(w_refs, wb_refs):
            slab_rows = part_of(src_ref, part)
            dst_ref[slab_rows, :] = src_ref[slab_rows, :].astype(BF16)
        section_of = lambda head: (tile_index * heads_per_tile + head) // HEADS_PER_SECTION
        firsts = sorted(range(0, heads_per_tile, heads_per_dot),
                        key=lambda first: section_of(first) in PLAIN_SECTIONS)
        for first in firsts:
            cols = slice(first * HEAD_DIM, (first + heads_per_dot) * HEAD_DIM)
            acc = _dot(h_ref[rows, :], w_ref[:, cols])
            for u in range(heads_per_dot):
                section = section_of(first + u)
                a = acc[:, u * HEAD_DIM:(u + 1) * HEAD_DIM] * r_ref[rows, :]
                lanes = slice((first + u) * HEAD_DIM, (first + u + 1) * HEAD_DIM)
                o_ref[rows, lanes] = epilogue(section, a, rows).astype(BF16)
        return carry

    def tile(tile_index):
        lax.fori_loop(0, IN_PROJ_PARTS, functools.partial(tile_part, tile_index), 0)

    for tile_index in range(IN_PROJ_TILES):
        pl.when(j == tile_index)(functools.partial(tile, tile_index))


def _in_proj(x2d, g1, w_bf16, cos2, sin2, qg, kg, weights, *, seq, tm):
    m, d = x2d.shape
    n = w_bf16.shape[1]
    tn = n // IN_PROJ_TILES
    assert n == N_SECTIONS * HEADS_PER_SECTION * HEAD_DIM and tn % MXU_COLS == 0
    assert m % tm == 0 and seq % tm == 0
    blocks_per_seq = seq // tm
    n_slabs = (m // tm) * IN_PROJ_TILES

    def slab(w):
        rows, cols = w.shape
        assert rows % (n_slabs * IN_PROJ_PARTS * BF16_SUBLANES) == 0
        return pl.BlockSpec((rows // n_slabs, cols), lambda i, j: (i * IN_PROJ_TILES + j, 0))

    slabs = [slab(w) for w in weights]
    proj, *weights_bf16 = pl.pallas_call(
        functools.partial(_in_proj_kernel, n_cast=len(weights)),
        out_shape=[jax.ShapeDtypeStruct((m, n), BF16)]
                  + [jax.ShapeDtypeStruct(w.shape, BF16) for w in weights],
        grid=(m // tm, IN_PROJ_TILES),
        in_specs=[
            pl.BlockSpec((tm, d), lambda i, j: (i, 0)),
            pl.BlockSpec((1, d), lambda i, j: (0, 0)),
            pl.BlockSpec((d, tn), lambda i, j: (0, j)),
            pl.BlockSpec((tm, HEAD_DIM), lambda i, j: (i % blocks_per_seq, 0)),
            pl.BlockSpec((tm, HEAD_DIM), lambda i, j: (i % blocks_per_seq, 0)),
            pl.BlockSpec((1, HEAD_DIM), lambda i, j: (0, 0)),
            pl.BlockSpec((1, HEAD_DIM), lambda i, j: (0, 0)),
        ] + slabs,
        out_specs=[pl.BlockSpec((tm, tn), lambda i, j: (i, j))] + slabs,
        scratch_shapes=[pltpu.VMEM((tm, d), BF16), pltpu.VMEM((tm, HEAD_DIM), F32)],
        compiler_params=_params(2, V7X_VMEM_HIGH_LIMIT_BYTES),
        name="in_proj",
    )(x2d, g1, w_bf16, cos2, sin2, qg, kg, *weights)
    return proj, weights_bf16


def _bias_rows(rel_bias):
    h, rel_size = rel_bias.shape
    assert rel_size == CHUNK + REL_CLIP
    far = rel_bias[:, rel_size - 1:]
    near = rel_bias[:, :1]
    left = LEFT_CHUNKS * CHUNK - REL_CLIP
    return jnp.concatenate([
        jnp.broadcast_to(far, (h, left)),
        jnp.flip(rel_bias, axis=-1),
        jnp.broadcast_to(near, (h, WINDOW - left - rel_size)),
        jnp.broadcast_to(far, (h, TOEPLITZ_WIDTH - WINDOW)),
    ], axis=-1).astype(F32).reshape(h, 1, TOEPLITZ_WIDTH)


def _fill_bias_tile(g_ref, bias_ref):
    qc = lax.broadcasted_iota(jnp.int32, (SEQ_BLOCK, WINDOW), 0) // CHUNK
    kc = lax.broadcasted_iota(jnp.int32, (SEQ_BLOCK, WINDOW), 1) // CHUNK
    visible = (kc >= qc) & (kc <= qc + LEFT_CHUNKS)
    for h in range(ATT_HEADS):
        row = jnp.broadcast_to(g_ref[h], (SEQ_BLOCK, TOEPLITZ_WIDTH))
        t = pltpu.roll(row, 0, 1, stride=1, stride_axis=0)[:, :WINDOW]
        bias_ref[h] = jnp.where(visible, t * LOG2E, NEG)


def _retention_tables():
    h = np.arange(RET_HEADS, dtype=np.float64)
    log_g = np.log1p(-np.exp2(-(5.0 + h)))
    pos = np.arange(SEQ_BLOCK, dtype=np.float64)
    dist = np.abs(pos[:, None] - pos[None, :])
    vis = (pos[None, :] // CHUNK) <= (pos[:, None] // CHUNK)
    decay = np.exp(log_g[:, None, None] * dist) * vis
    q_dec = np.exp(log_g[None, :] * (pos + 1.0)[:, None])
    k_dec = np.exp(log_g[None, :] * (SEQ_BLOCK - 1.0 - pos)[:, None])
    blk_dec = np.exp(log_g * SEQ_BLOCK)
    lanes = lambda t: np.repeat(t, HEAD_DIM, axis=1).astype(np.float32)
    return decay.astype(np.float32), lanes(q_dec), lanes(k_dec), [float(v) for v in blk_dec]


def _retention_kernel(q_ref, k_ref, v_ref, gate_ref, dec_ref, qdec_ref, kdec_ref, gn_ref,
                      o_ref, st_ref, *, blocks_per_seq, blk_dec):
    @pl.when(pl.program_id(0) % blocks_per_seq == 0)
    def _():
        st_ref[...] = jnp.zeros_like(st_ref)

    for h in range(RET_HEADS):
        sl = slice(h * HEAD_DIM, (h + 1) * HEAD_DIM)
        state = st_ref[h]
        for blk in range(RET_BLOCKS):
            rows = slice(blk * SEQ_BLOCK, (blk + 1) * SEQ_BLOCK)
            q = q_ref[rows, sl]
            k = k_ref[rows, sl]
            v = v_ref[rows, sl]
            p = (_dot_nt(q, k) * dec_ref[h]).astype(BF16)
            y = _dot(p, v) + _dot(q, state.astype(BF16)) * qdec_ref[:, sl]
            kd = (k.astype(F32) * kdec_ref[:, sl]).astype(BF16)
            state = blk_dec[h] * state + _dot_tn(kd, v)
            mu = jnp.mean(y, axis=-1, keepdims=True)
            yc = y - mu
            var = jnp.mean(yc * yc, axis=-1, keepdims=True)
            yn = yc * lax.rsqrt(var + GN_EPS) * gn_ref[:, sl]
            gate = gate_ref[rows, sl].astype(F32)
            swish = gate * (0.5 + 0.5 * jnp.tanh(0.5 * gate))
            o_ref[rows, sl] = (swish * yn).astype(BF16)
        st_ref[h] = state


def _retention(proj, gn, *, seq):
    m = proj.shape[0]
    width = RET_HEADS * HEAD_DIM
    decay, q_dec, k_dec, blk_dec = _retention_tables()
    step_rows = RET_BLOCKS * SEQ_BLOCK
    assert seq % step_rows == 0
    blocks_per_seq = seq // step_rows
    section = lambda s: pl.BlockSpec((step_rows, width), lambda i: (i, s))
    whole = lambda shape: pl.BlockSpec(shape, lambda i: (0,) * len(shape))
    return pl.pallas_call(
        functools.partial(_retention_kernel, blocks_per_seq=blocks_per_seq, blk_dec=blk_dec),
        out_shape=jax.ShapeDtypeStruct((m, width), BF16),
        grid=(m // step_rows,),
        in_specs=[section(0), section(1), section(2), section(3),
                  whole(decay.shape), whole(q_dec.shape), whole(k_dec.shape), whole((1, width))],
        out_specs=pl.BlockSpec((step_rows, width), lambda i: (i, 0)),
        scratch_shapes=[pltpu.VMEM((RET_HEADS, HEAD_DIM, HEAD_DIM), F32)],
        compiler_params=_params(1),
        name="retention",
    )(proj, proj, proj, proj, decay, q_dec, k_dec, gn)


def _chunk_attn_kernel(q_ref, kp0_ref, kp1_ref, k_ref, vp0_ref, vp1_ref, v_ref, g_ref, *rest,
                       steps_per_seq, n_cast):
    w_refs, o_ref, wb_refs = rest[:n_cast], rest[n_cast], rest[n_cast + 1:2 * n_cast + 1]
    bias_ref = rest[2 * n_cast + 1]
    pl.when(pl.program_id(0) == 0)(functools.partial(_fill_bias_tile, g_ref, bias_ref))
    first_step = pl.program_id(0) % steps_per_seq == 0
    add = lambda a, b: a + b
    lane_tiles = SEQ_BLOCK // HEAD_DIM
    rows = lambda i0, i1: slice(i0 * SEQ_BLOCK, (i1 + 1) * SEQ_BLOCK)

    def key_block(prev_refs, main_ref, b, sl):
        if b < LEFT_BLOCKS:
            return prev_refs[b][:, sl]
        return main_ref[rows(b - LEFT_BLOCKS, b - LEFT_BLOCKS), sl]

    def attend_heads(first_block, group, carry):
        for w_ref, wb_ref in zip(w_refs, wb_refs):
            n = w_ref.shape[0] // ATT_HEAD_GROUPS
            slab_rows = pl.ds(pl.multiple_of(group * n, n), n)
            wb_ref[slab_rows, :] = w_ref[slab_rows, :].astype(BF16)
        key_blocks = range(first_block, ATT_BLOCKS + LEFT_BLOCKS)
        users = {b: (max(0, b - LEFT_BLOCKS), min(ATT_BLOCKS - 1, b)) for b in key_blocks}
        heads_per_group = ATT_HEADS // ATT_HEAD_GROUPS
        for head_in_group in range(heads_per_group):
            h = group * heads_per_group + head_in_group
            sl = pl.ds(pl.multiple_of(h * HEAD_DIM, HEAD_DIM), HEAD_DIM)
            tiles = {i: [] for i in range(ATT_BLOCKS)}
            for b in key_blocks:
                i0, i1 = users[b]
                s = _dot_nt(q_ref[rows(i0, i1), sl], key_block((kp0_ref, kp1_ref), k_ref, b, sl))
                for i in range(i0, i1 + 1):
                    w = b - i
                    sb = (s[rows(i - i0, i - i0), :]
                          + bias_ref[h, :, w * SEQ_BLOCK:(w + 1) * SEQ_BLOCK])
                    tiles[i] += [sb[:, c * HEAD_DIM:(c + 1) * HEAD_DIM] for c in range(lane_tiles)]
            probs, denom = {}, {}
            for i in range(ATT_BLOCKS):
                mx = jnp.max(functools.reduce(jnp.maximum, tiles[i]), axis=-1, keepdims=True)
                p = [jnp.exp2(tile - mx) for tile in tiles[i]]
                denom[i] = jnp.sum(functools.reduce(add, p), axis=-1, keepdims=True)
                visible = [b for b in key_blocks if users[b][0] <= i <= users[b][1]]
                for n, b in enumerate(visible):
                    probs[i, b] = jnp.concatenate(
                        p[n * lane_tiles:(n + 1) * lane_tiles], axis=1).astype(BF16)
            out = {i: [] for i in range(ATT_BLOCKS)}
            for b in key_blocks:
                i0, i1 = users[b]
                lhs = jnp.concatenate([probs[i, b] for i in range(i0, i1 + 1)], axis=0)
                pv = _dot(lhs, key_block((vp0_ref, vp1_ref), v_ref, b, sl))
                for i in range(i0, i1 + 1):
                    out[i].append(pv[rows(i - i0, i - i0), :])
            for i in range(ATT_BLOCKS):
                o_ref[rows(i, i), sl] = (functools.reduce(add, out[i]) / denom[i]).astype(BF16)
        return carry

    def attend(first_block):
        lax.fori_loop(0, ATT_HEAD_GROUPS, functools.partial(attend_heads, first_block), 0)

    pl.when(first_step)(functools.partial(attend, LEFT_BLOCKS))
    pl.when(jnp.logical_not(first_step))(functools.partial(attend, 0))


def _chunk_attn(proj, bias_rows, weights, *, seq):
    m = proj.shape[0]
    width = ATT_HEADS * HEAD_DIM
    step_rows = ATT_BLOCKS * SEQ_BLOCK
    assert seq % step_rows == 0 and ATT_BLOCKS >= LEFT_BLOCKS
    steps_per_seq = seq // step_rows
    steps = m // step_rows

    def previous(section, b):
        def index(i):
            seq_start = (i - i % steps_per_seq) * ATT_BLOCKS
            return (jnp.maximum(i * ATT_BLOCKS - (LEFT_BLOCKS - b), seq_start), section)
        return pl.BlockSpec((SEQ_BLOCK, width), index)

    main = lambda section: pl.BlockSpec((step_rows, width), lambda i: (i, section))

    def slab(w):
        rows, cols = w.shape
        assert rows % (steps * BF16_SUBLANES) == 0
        return pl.BlockSpec((rows // steps, cols), lambda i: (i, 0))

    att, *weights_bf16 = pl.pallas_call(
        functools.partial(_chunk_attn_kernel, steps_per_seq=steps_per_seq, n_cast=len(weights)),
        out_shape=[jax.ShapeDtypeStruct((m, width), BF16)]
                  + [jax.ShapeDtypeStruct(w.shape, BF16) for w in weights],
        grid=(steps,),
        in_specs=[main(4)]
                 + [previous(5, b) for b in range(LEFT_BLOCKS)] + [main(5)]
                 + [previous(6, b) for b in range(LEFT_BLOCKS)] + [main(6)]
                 + [pl.BlockSpec(bias_rows.shape, lambda i: (0, 0, 0))]
                 + [slab(w) for w in weights],
        out_specs=[pl.BlockSpec((step_rows, width), lambda i: (i, 0))] + [slab(w) for w in weights],
        scratch_shapes=[pltpu.VMEM((ATT_HEADS, SEQ_BLOCK, WINDOW), F32)],
        compiler_params=_params(1),
        name="chunk_attn",
    )(proj, proj, proj, proj, proj, proj, proj, bias_rows, *weights)
    return att, weights_bf16


def _out_proj_kernel(ret_ref, att_ref, wr_ref, wa_ref, x_ref, o_ref):
    o_ref[...] = x_ref[...] + (_dot(ret_ref[...], wr_ref[...]) + _dot(att_ref[...], wa_ref[...]))


def _out_proj(ret, att, w_bf16, x2d, *, tm, tn):
    m, d = x2d.shape
    kr, ka = ret.shape[1], att.shape[1]
    assert kr == ka and kr + ka == w_bf16.shape[0] and m % tm == 0 and d % tn == 0
    return pl.pallas_call(
        _out_proj_kernel,
        out_shape=jax.ShapeDtypeStruct((m, d), F32),
        grid=(m // tm, d // tn),
        in_specs=[
            pl.BlockSpec((tm, kr), lambda i, j: (i, 0)),
            pl.BlockSpec((tm, ka), lambda i, j: (i, 0)),
            pl.BlockSpec((kr, tn), lambda i, j: (0, j)),
            pl.BlockSpec((ka, tn), lambda i, j: (1, j)),
            pl.BlockSpec((tm, tn), lambda i, j: (i, j)),
        ],
        out_specs=pl.BlockSpec((tm, tn), lambda i, j: (i, j)),
        compiler_params=_params(2, V7X_VMEM_HIGH_LIMIT_BYTES),
        name="out_proj",
    )(ret, att, w_bf16, w_bf16, x2d)


def _ffn_kernel(x_ref, g_ref, w1_ref, w2_ref, o_ref, h_ref, r_ref):
    f = pl.program_id(1)

    def chunk():
        a = _dot(h_ref[...], w1_ref[...])
        hidden = []
        for c in range(a.shape[1] // HEAD_DIM):
            act = jnp.maximum(a[:, c * HEAD_DIM:(c + 1) * HEAD_DIM] * r_ref[...], 0.0)
            hidden.append((act * act).astype(BF16))
        o_ref[...] += _dot(jnp.concatenate(hidden, axis=1), w2_ref[...])

    @pl.when(f == 0)
    def _():
        x = x_ref[...]
        ms = jnp.mean(x * x, axis=-1, keepdims=True)
        r_ref[...] = jnp.broadcast_to(lax.rsqrt(ms + EPS), r_ref.shape)
        h_ref[...] = (x * g_ref[...]).astype(BF16)
        o_ref[...] = x
        chunk()

    pl.when(f > 0)(chunk)


def _ffn(x2d, g2, w1_bf16, w2_bf16, *, tm, tf):
    m, d = x2d.shape
    d_ff = w1_bf16.shape[1]
    assert m % tm == 0 and d_ff % tf == 0
    return pl.pallas_call(
        _ffn_kernel,
        out_shape=jax.ShapeDtypeStruct((m, d), F32),
        grid=(m // tm, d_ff // tf),
        in_specs=[
            pl.BlockSpec((tm, d), lambda i, f: (i, 0)),
            pl.BlockSpec((1, d), lambda i, f: (0, 0)),
            pl.BlockSpec((d, tf), lambda i, f: (0, f)),
            pl.BlockSpec((tf, d), lambda i, f: (f, 0)),
        ],
        out_specs=pl.BlockSpec((tm, d), lambda i, f: (i, 0)),
        scratch_shapes=[pltpu.VMEM((tm, d), BF16), pltpu.VMEM((tm, HEAD_DIM), F32)],
        compiler_params=_params(2, V7X_VMEM_HIGH_LIMIT_BYTES),
        name="ffn",
    )(x2d, g2, w1_bf16, w2_bf16)


def _rotary_tables(seq):
    half = HEAD_DIM // 2
    inv_freq = ROPE_BASE ** (-np.arange(half, dtype=np.float64) / half)
    ang = np.arange(seq, dtype=np.float64)[:, None] * inv_freq[None, :]
    cos, sin = np.cos(ang), np.sin(ang)
    cos2 = np.concatenate([cos, cos], axis=1).astype(np.float32)
    sin2 = np.concatenate([-sin, sin], axis=1).astype(np.float32)
    return cos2, sin2


def kernel(x, norm1_g, w_in, ret_norm_g, q_norm_g, k_norm_g, rel_bias, w_out, norm2_g, w_ff1, w_ff2):
    b, s, d = x.shape
    depth = w_in.shape[0]
    assert s % SEQ_BLOCK == 0 and SEQ_BLOCK % CHUNK == 0
    assert RET_HEADS == HEADS_PER_SECTION and ATT_HEADS == HEADS_PER_SECTION
    cos2, sin2 = _rotary_tables(s)
    row = lambda v: v.astype(F32).reshape(1, -1)
    x2d = x.reshape(b * s, d)
    for layer in range(depth):
        proj, (w1, w2) = _in_proj(x2d, row(norm1_g[layer]), w_in[layer].astype(BF16), cos2, sin2,
                                  row(q_norm_g[layer]), row(k_norm_g[layer]),
                                  (w_ff1[layer], w_ff2[layer]), seq=s, tm=IN_PROJ_ROWS)
        ret = _retention(proj, row(ret_norm_g[layer]), seq=s)
        att, (wo,) = _chunk_attn(proj, _bias_rows(rel_bias[layer]), (w_out[layer],), seq=s)
        x2d = _out_proj(ret, att, wo, x2d, tm=OUT_PROJ_ROWS, tn=d)
        x2d = _ffn(x2d, row(norm2_g[layer]), w1, w2, tm=FFN_ROWS, tf=FFN_CHUNK)
    return x2d.reshape(b, s, d)
```

```python
import functools
import math

import numpy as np
import jax
import jax.numpy as jnp
from jax import lax
from jax.experimental import pallas as pl
from jax.experimental.pallas import tpu as pltpu

CHUNK = 64
LEFT_CHUNKS = 8
REL_CLIP = 128
HEAD_DIM = 128
RET_HEADS = 8
ATT_HEADS = 8
ROPE_BASE = 10000.0
EPS = 1e-6
GN_EPS = 1e-5
NEG = -1e30
LOG2E = math.log2(math.e)

V7X_VMEM_BYTES = 64 * 1024 * 1024
V7X_VMEM_LIMIT_BYTES = V7X_VMEM_BYTES - 8 * 1024 * 1024
V7X_VMEM_HIGH_LIMIT_BYTES = V7X_VMEM_BYTES - 4 * 1024 * 1024

SEQ_BLOCK = 256
LEFT_BLOCKS = LEFT_CHUNKS * CHUNK // SEQ_BLOCK
WINDOW = (LEFT_BLOCKS + 1) * SEQ_BLOCK
MIX_BLOCKS = 2
MIX_HEAD_GROUPS = 2
TOEPLITZ_WIDTH = 1024
IN_PROJ_ROWS = 1024
OUT_PROJ_ROWS = 512
FFN_ROWS = 1024
FFN_CHUNK = 1024

F32 = jnp.float32
BF16 = jnp.bfloat16
BF16_SUBLANES = 16


def _params(n_axes, vmem_bytes=V7X_VMEM_LIMIT_BYTES):
    return pltpu.CompilerParams(
        dimension_semantics=("arbitrary",) * n_axes, vmem_limit_bytes=vmem_bytes)


def _dot(a, b):
    return jnp.dot(a, b, preferred_element_type=F32)


def _dot_nt(a, b):
    return lax.dot_general(a, b, (((1,), (1,)), ((), ())), preferred_element_type=F32)


def _dot_tn(a, b):
    return lax.dot_general(a, b, (((0,), (0,)), ((), ())), preferred_element_type=F32)


N_SECTIONS = 7
PLAIN_SECTIONS = (2, 3, 6)
HEADS_PER_SECTION = 8
IN_PROJ_TILES = 4
IN_PROJ_PARTS = 2
MXU_COLS = 256


def _in_proj_kernel(x_ref, g_ref, w_ref, cos_ref, sin_ref, qg_ref, kg_ref, *rest, n_cast):
    w_refs, o_ref, wb_refs = rest[:n_cast], rest[n_cast], rest[n_cast + 1:2 * n_cast + 1]
    h_ref, r_ref = rest[2 * n_cast + 1:]
    j = pl.program_id(1)
    heads_per_tile = o_ref.shape[1] // HEAD_DIM
    heads_per_dot = MXU_COLS // HEAD_DIM
    scale = HEAD_DIM ** -0.5

    def epilogue(section, a, rows):
        if section in (0, 1):
            y = a * cos_ref[rows, :] + pltpu.roll(a, HEAD_DIM // 2, 1) * sin_ref[rows, :]
            return y * scale if section == 1 else y
        if section in (4, 5):
            ms = jnp.mean(a * a, axis=-1, keepdims=True)
            y = a * lax.rsqrt(ms + EPS) * (qg_ref if section == 4 else kg_ref)[...]
            return y * (scale * LOG2E) if section == 4 else y
        return a

    def part_of(ref, part):
        n = ref.shape[0] // IN_PROJ_PARTS
        return pl.ds(pl.multiple_of(part * n, n), n)

    def tile_part(tile_index, part, carry):
        rows = part_of(h_ref, part)
        if tile_index == 0:
            x = x_ref[rows, :]
            ms = jnp.mean(x * x, axis=-1, keepdims=True)
            r_ref[rows, :] = jnp.broadcast_to(lax.rsqrt(ms + EPS), (x.shape[0], HEAD_DIM))
            h_ref[rows, :] = (x * g_ref[...]).astype(BF16)
        for src_ref, dst_ref in zip(w_refs, wb_refs):
            slab_rows = part_of(src_ref, part)
            dst_ref[slab_rows, :] = src_ref[slab_rows, :].astype(BF16)
        section_of = lambda head: (tile_index * heads_per_tile + head) // HEADS_PER_SECTION
        firsts = sorted(range(0, heads_per_tile, heads_per_dot),
                        key=lambda first: section_of(first) in PLAIN_SECTIONS)
        for first in firsts:
            cols = slice(first * HEAD_DIM, (first + heads_per_dot) * HEAD_DIM)
            acc = _dot(h_ref[rows, :], w_ref[:, cols])
            for u in range(heads_per_dot):
                section = section_of(first + u)
                a = acc[:, u * HEAD_DIM:(u + 1) * HEAD_DIM] * r_ref[rows, :]
                lanes = slice((first + u) * HEAD_DIM, (first + u + 1) * HEAD_DIM)
                o_ref[rows, lanes] = epilogue(section, a, rows).astype(BF16)
        return carry

    def tile(tile_index):
        lax.fori_loop(0, IN_PROJ_PARTS, functools.partial(tile_part, tile_index), 0)

    for tile_index in range(IN_PROJ_TILES):
        pl.when(j == tile_index)(functools.partial(tile, tile_index))


def _in_proj(x2d, g1, w_bf16, cos2, sin2, qg, kg, weights, *, seq, tm):
    m, d = x2d.shape
    n = w_bf16.shape[1]
    tn = n // IN_PROJ_TILES
    assert n == N_SECTIONS * HEADS_PER_SECTION * HEAD_DIM and tn % MXU_COLS == 0
    assert m % tm == 0 and seq % tm == 0
    blocks_per_seq = seq // tm
    n_slabs = (m // tm) * IN_PROJ_TILES

    def slab(w):
        rows, cols = w.shape
        assert rows % (n_slabs * IN_PROJ_PARTS * BF16_SUBLANES) == 0
        return pl.BlockSpec((rows // n_slabs, cols), lambda i, j: (i * IN_PROJ_TILES + j, 0))

    slabs = [slab(w) for w in weights]
    proj, *weights_bf16 = pl.pallas_call(
        functools.partial(_in_proj_kernel, n_cast=len(weights)),
        out_shape=[jax.ShapeDtypeStruct((m, n), BF16)]
                  + [jax.ShapeDtypeStruct(w.shape, BF16) for w in weights],
        grid=(m // tm, IN_PROJ_TILES),
        in_specs=[
            pl.BlockSpec((tm, d), lambda i, j: (i, 0)),
            pl.BlockSpec((1, d), lambda i, j: (0, 0)),
            pl.BlockSpec((d, tn), lambda i, j: (0, j)),
            pl.BlockSpec((tm, HEAD_DIM), lambda i, j: (i % blocks_per_seq, 0)),
            pl.BlockSpec((tm, HEAD_DIM), lambda i, j: (i % blocks_per_seq, 0)),
            pl.BlockSpec((1, HEAD_DIM), lambda i, j: (0, 0)),
            pl.BlockSpec((1, HEAD_DIM), lambda i, j: (0, 0)),
        ] + slabs,
        out_specs=[pl.BlockSpec((tm, tn), lambda i, j: (i, j))] + slabs,
        scratch_shapes=[pltpu.VMEM((tm, d), BF16), pltpu.VMEM((tm, HEAD_DIM), F32)],
        compiler_params=_params(2, V7X_VMEM_HIGH_LIMIT_BYTES),
        name="in_proj",
    )(x2d, g1, w_bf16, cos2, sin2, qg, kg, *weights)
    return proj, weights_bf16


def _bias_rows(rel_bias):
    h, rel_size = rel_bias.shape
    assert rel_size == CHUNK + REL_CLIP
    far = rel_bias[:, rel_size - 1:]
    near = rel_bias[:, :1]
    left = LEFT_CHUNKS * CHUNK - REL_CLIP
    return jnp.concatenate([
        jnp.broadcast_to(far, (h, left)),
        jnp.flip(rel_bias, axis=-1),
        jnp.broadcast_to(near, (h, WINDOW - left - rel_size)),
        jnp.broadcast_to(far, (h, TOEPLITZ_WIDTH - WINDOW)),
    ], axis=-1).astype(F32).reshape(h, 1, TOEPLITZ_WIDTH)


def _fill_bias_tile(g_ref, bias_ref):
    qc = lax.broadcasted_iota(jnp.int32, (SEQ_BLOCK, WINDOW), 0) // CHUNK
    kc = lax.broadcasted_iota(jnp.int32, (SEQ_BLOCK, WINDOW), 1) // CHUNK
    visible = (kc >= qc) & (kc <= qc + LEFT_CHUNKS)
    for h in range(ATT_HEADS):
        row = jnp.broadcast_to(g_ref[h], (SEQ_BLOCK, TOEPLITZ_WIDTH))
        t = pltpu.roll(row, 0, 1, stride=1, stride_axis=0)[:, :WINDOW]
        bias_ref[h] = jnp.where(visible, t * LOG2E, NEG)


def _retention_tables():
    h = np.arange(RET_HEADS, dtype=np.float64)
    log_g = np.log1p(-np.exp2(-(5.0 + h)))
    pos = np.arange(SEQ_BLOCK, dtype=np.float64)
    dist = np.abs(pos[:, None] - pos[None, :])
    vis = (pos[None, :] // CHUNK) <= (pos[:, None] // CHUNK)
    decay = np.exp(log_g[:, None, None] * dist) * vis
    q_dec = np.exp(log_g[None, :] * (pos + 1.0)[:, None])
    k_dec = np.exp(log_g[None, :] * (SEQ_BLOCK - 1.0 - pos)[:, None])
    blk_dec = np.exp(log_g * SEQ_BLOCK)
    lanes = lambda t: np.repeat(t, HEAD_DIM, axis=1).astype(np.float32)
    blk_dec = np.broadcast_to(blk_dec[:, None, None], (RET_HEADS, 1, HEAD_DIM))
    return decay.astype(np.float32), lanes(q_dec), lanes(k_dec), blk_dec.astype(np.float32)


def _mixer_kernel(rq_ref, rk_ref, rv_ref, gate_ref, q_ref, kp0_ref, kp1_ref, k_ref, vp0_ref, vp1_ref, v_ref,
                  g_ref, dec_ref, qdec_ref, kdec_ref, bdec_ref, gn_ref, *rest, steps_per_seq, n_cast):
    w_refs = rest[:n_cast]
    ret_ref, att_ref = rest[n_cast:n_cast + 2]
    wb_refs = rest[n_cast + 2:2 * n_cast + 2]
    bias_ref, st_ref = rest[2 * n_cast + 2:]
    pl.when(pl.program_id(0) == 0)(functools.partial(_fill_bias_tile, g_ref, bias_ref))
    first_step = pl.program_id(0) % steps_per_seq == 0

    @pl.when(first_step)
    def _():
        st_ref[...] = jnp.zeros_like(st_ref)

    add = lambda a, b: a + b
    lane_tiles = SEQ_BLOCK // HEAD_DIM
    rows = lambda i0, i1: slice(i0 * SEQ_BLOCK, (i1 + 1) * SEQ_BLOCK)
    heads_per_group = RET_HEADS // MIX_HEAD_GROUPS

    def key_block(prev_refs, main_ref, b, sl):
        if b < LEFT_BLOCKS:
            return prev_refs[b][:, sl]
        return main_ref[rows(b - LEFT_BLOCKS, b - LEFT_BLOCKS), sl]

    def retain(h, sl):
        state = st_ref[h]
        for blk in range(MIX_BLOCKS):
            r = rows(blk, blk)
            q = rq_ref[r, sl]
            k = rk_ref[r, sl]
            v = rv_ref[r, sl]
            p = (_dot_nt(q, k) * dec_ref[h]).astype(BF16)
            y = _dot(p, v) + _dot(q, state.astype(BF16)) * qdec_ref[:, sl]
            kd = (k.astype(F32) * kdec_ref[:, sl]).astype(BF16)
            state = bdec_ref[h] * state + _dot_tn(kd, v)
            mu = jnp.mean(y, axis=-1, keepdims=True)
            yc = y - mu
            var = jnp.mean(yc * yc, axis=-1, keepdims=True)
            yn = yc * lax.rsqrt(var + GN_EPS) * gn_ref[:, sl]
            gate = gate_ref[r, sl].astype(F32)
            swish = gate * (0.5 + 0.5 * jnp.tanh(0.5 * gate))
            ret_ref[r, sl] = (swish * yn).astype(BF16)
        st_ref[h] = state

    def attend(h, sl, first_block):
        key_blocks = range(first_block, MIX_BLOCKS + LEFT_BLOCKS)
        users = {b: (max(0, b - LEFT_BLOCKS), min(MIX_BLOCKS - 1, b)) for b in key_blocks}
        tiles = {i: [] for i in range(MIX_BLOCKS)}
        for b in key_blocks:
            i0, i1 = users[b]
            s = _dot_nt(q_ref[rows(i0, i1), sl], key_block((kp0_ref, kp1_ref), k_ref, b, sl))
            for i in range(i0, i1 + 1):
                w = b - i
                sb = s[rows(i - i0, i - i0), :] + bias_ref[h, :, w * SEQ_BLOCK:(w + 1) * SEQ_BLOCK]
                tiles[i] += [sb[:, c * HEAD_DIM:(c + 1) * HEAD_DIM] for c in range(lane_tiles)]
        probs, denom = {}, {}
        for i in range(MIX_BLOCKS):
            mx = jnp.max(functools.reduce(jnp.maximum, tiles[i]), axis=-1, keepdims=True)
            p = [jnp.exp2(tile - mx) for tile in tiles[i]]
            denom[i] = jnp.sum(functools.reduce(add, p), axis=-1, keepdims=True)
            visible = [b for b in key_blocks if users[b][0] <= i <= users[b][1]]
            for n, b in enumerate(visible):
                probs[i, b] = jnp.concatenate(p[n * lane_tiles:(n + 1) * lane_tiles], axis=1).astype(BF16)
        out = {i: [] for i in range(MIX_BLOCKS)}
        for b in key_blocks:
            i0, i1 = users[b]
            lhs = jnp.concatenate([probs[i, b] for i in range(i0, i1 + 1)], axis=0)
            pv = _dot(lhs, key_block((vp0_ref, vp1_ref), v_ref, b, sl))
            for i in range(i0, i1 + 1):
                out[i].append(pv[rows(i - i0, i - i0), :])
        for i in range(MIX_BLOCKS):
            att_ref[rows(i, i), sl] = (functools.reduce(add, out[i]) / denom[i]).astype(BF16)

    def head_group(first_block, group, carry):
        for w_ref, wb_ref in zip(w_refs, wb_refs):
            n = w_ref.shape[0] // MIX_HEAD_GROUPS
            slab_rows = pl.ds(pl.multiple_of(group * n, n), n)
            wb_ref[slab_rows, :] = w_ref[slab_rows, :].astype(BF16)
        for head_in_group in range(heads_per_group):
            h = group * heads_per_group + head_in_group
            sl = pl.ds(pl.multiple_of(h * HEAD_DIM, HEAD_DIM), HEAD_DIM)
            attend(h, sl, first_block)
            retain(h, sl)
        return carry

    def step(first_block):
        lax.fori_loop(0, MIX_HEAD_GROUPS, functools.partial(head_group, first_block), 0)

    pl.when(first_step)(functools.partial(step, LEFT_BLOCKS))
    pl.when(jnp.logical_not(first_step))(functools.partial(step, 0))


def _mixer(proj, bias_rows, gn, weights, *, seq):
    m = proj.shape[0]
    width = RET_HEADS * HEAD_DIM
    assert RET_HEADS == ATT_HEADS and RET_HEADS % MIX_HEAD_GROUPS == 0
    step_rows = MIX_BLOCKS * SEQ_BLOCK
    assert seq % step_rows == 0 and MIX_BLOCKS >= LEFT_BLOCKS
    steps_per_seq = seq // step_rows
    steps = m // step_rows
    decay, q_dec, k_dec, blk_dec = _retention_tables()

    def previous(section, b):
        def index(i):
            seq_start = (i - i % steps_per_seq) * MIX_BLOCKS
            return (jnp.maximum(i * MIX_BLOCKS - (LEFT_BLOCKS - b), seq_start), section)
        return pl.BlockSpec((SEQ_BLOCK, width), index)

    main = lambda section: pl.BlockSpec((step_rows, width), lambda i: (i, section))
    whole = lambda a: pl.BlockSpec(a.shape, lambda i: (0,) * a.ndim)

    def slab(w):
        rows, cols = w.shape
        assert rows % (steps * MIX_HEAD_GROUPS * BF16_SUBLANES) == 0
        return pl.BlockSpec((rows // steps, cols), lambda i: (i, 0))

    tables = (bias_rows, decay, q_dec, k_dec, blk_dec, gn)
    ret, att, *weights_bf16 = pl.pallas_call(
        functools.partial(_mixer_kernel, steps_per_seq=steps_per_seq, n_cast=len(weights)),
        out_shape=[jax.ShapeDtypeStruct((m, width), BF16)] * 2
                  + [jax.ShapeDtypeStruct(w.shape, BF16) for w in weights],
        grid=(steps,),
        in_specs=[main(0), main(1), main(2), main(3), main(4)]
                 + [previous(5, b) for b in range(LEFT_BLOCKS)] + [main(5)]
                 + [previous(6, b) for b in range(LEFT_BLOCKS)] + [main(6)]
                 + [whole(t) for t in tables]
                 + [slab(w) for w in weights],
        out_specs=[pl.BlockSpec((step_rows, width), lambda i: (i, 0))] * 2 + [slab(w) for w in weights],
        scratch_shapes=[pltpu.VMEM((ATT_HEADS, SEQ_BLOCK, WINDOW), F32),
                        pltpu.VMEM((RET_HEADS, HEAD_DIM, HEAD_DIM), F32)],
        compiler_params=_params(1),
        name="mixer",
    )(*([proj] * 11), *tables, *weights)
    return ret, att, weights_bf16


def _out_proj_kernel(ret_ref, att_ref, wr_ref, wa_ref, x_ref, o_ref):
    o_ref[...] = x_ref[...] + (_dot(ret_ref[...], wr_ref[...]) + _dot(att_ref[...], wa_ref[...]))


def _out_proj(ret, att, w_bf16, x2d, *, tm, tn):
    m, d = x2d.shape
    kr, ka = ret.shape[1], att.shape[1]
    assert kr == ka and kr + ka == w_bf16.shape[0] and m % tm == 0 and d % tn == 0
    return pl.pallas_call(
        _out_proj_kernel,
        out_shape=jax.ShapeDtypeStruct((m, d), F32),
        grid=(m // tm, d // tn),
        in_specs=[
            pl.BlockSpec((tm, kr), lambda i, j: (i, 0)),
            pl.BlockSpec((tm, ka), lambda i, j: (i, 0)),
            pl.BlockSpec((kr, tn), lambda i, j: (0, j)),
            pl.BlockSpec((ka, tn), lambda i, j: (1, j)),
            pl.BlockSpec((tm, tn), lambda i, j: (i, j)),
        ],
        out_specs=pl.BlockSpec((tm, tn), lambda i, j: (i, j)),
        compiler_params=_params(2),
        name="out_proj",
    )(ret, att, w_bf16, w_bf16, x2d)


def _ffn_kernel(x_ref, g_ref, w1_ref, w2_ref, o_ref, h_ref, r_ref):
    f = pl.program_id(1)

    def chunk():
        a = _dot(h_ref[...], w1_ref[...])
        hidden = []
        for c in range(a.shape[1] // HEAD_DIM):
            act = jnp.maximum(a[:, c * HEAD_DIM:(c + 1) * HEAD_DIM] * r_ref[...], 0.0)
            hidden.append((act * act).astype(BF16))
        o_ref[...] += _dot(jnp.concatenate(hidden, axis=1), w2_ref[...])

    @pl.when(f == 0)
    def _():
        x = x_ref[...]
        ms = jnp.mean(x * x, axis=-1, keepdims=True)
        r_ref[...] = jnp.broadcast_to(lax.rsqrt(ms + EPS), r_ref.shape)
        h_ref[...] = (x * g_ref[...]).astype(BF16)
        o_ref[...] = x
        chunk()

    pl.when(f > 0)(chunk)


def _ffn(x2d, g2, w1_bf16, w2_bf16, *, tm, tf):
    m, d = x2d.shape
    d_ff = w1_bf16.shape[1]
    assert m % tm == 0 and d_ff % tf == 0
    return pl.pallas_call(
        _ffn_kernel,
        out_shape=jax.ShapeDtypeStruct((m, d), F32),
        grid=(m // tm, d_ff // tf),
        in_specs=[
            pl.BlockSpec((tm, d), lambda i, f: (i, 0)),
            pl.BlockSpec((1, d), lambda i, f: (0, 0)),
            pl.BlockSpec((d, tf), lambda i, f: (0, f)),
            pl.BlockSpec((tf, d), lambda i, f: (f, 0)),
        ],
        out_specs=pl.BlockSpec((tm, d), lambda i, f: (i, 0)),
        scratch_shapes=[pltpu.VMEM((tm, d), BF16), pltpu.VMEM((tm, HEAD_DIM), F32)],
        compiler_params=_params(2, V7X_VMEM_HIGH_LIMIT_BYTES),
        name="ffn",
    )(x2d, g2, w1_bf16, w2_bf16)


def _rotary_tables(seq):
    half = HEAD_DIM // 2
    inv_freq = ROPE_BASE ** (-np.arange(half, dtype=np.float64) / half)
    ang = np.arange(seq, dtype=np.float64)[:, None] * inv_freq[None, :]
    cos, sin = np.cos(ang), np.sin(ang)
    cos2 = np.concatenate([cos, cos], axis=1).astype(np.float32)
    sin2 = np.concatenate([-sin, sin], axis=1).astype(np.float32)
    return cos2, sin2


def kernel(x, norm1_g, w_in, ret_norm_g, q_norm_g, k_norm_g, rel_bias, w_out, norm2_g, w_ff1, w_ff2):
    b, s, d = x.shape
    depth = w_in.shape[0]
    assert s % SEQ_BLOCK == 0 and SEQ_BLOCK % CHUNK == 0
    assert RET_HEADS == HEADS_PER_SECTION and ATT_HEADS == HEADS_PER_SECTION
    cos2, sin2 = _rotary_tables(s)
    row = lambda v: v.astype(F32).reshape(1, -1)
    x2d = x.reshape(b * s, d)
    for layer in range(depth):
        proj, (w1, w2) = _in_proj(x2d, row(norm1_g[layer]), w_in[layer].astype(BF16), cos2, sin2,
                                  row(q_norm_g[layer]), row(k_norm_g[layer]),
                                  (w_ff1[layer], w_ff2[layer]), seq=s, tm=IN_PROJ_ROWS)
        ret, att, (wo,) = _mixer(proj, _bias_rows(rel_bias[layer]), row(ret_norm_g[layer]),
                                 (w_out[layer],), seq=s)
        x2d = _out_proj(ret, att, wo, x2d, tm=OUT_PROJ_ROWS, tn=d)
        x2d = _ffn(x2d, row(norm2_g[layer]), w1, w2, tm=FFN_ROWS, tf=FFN_CHUNK)
    return x2d.reshape(b, s, d)
```

```python
import functools
import math

import numpy as np
import jax
import jax.numpy as jnp
from jax import lax
from jax.experimental import pallas as pl
from jax.experimental.pallas import tpu as pltpu

CHUNK = 64
LEFT_CHUNKS = 8
REL_CLIP = 128
HEAD_DIM = 128
RET_HEADS = 8
ATT_HEADS = 8
ROPE_BASE = 10000.0
EPS = 1e-6
GN_EPS = 1e-5
NEG = -1e30
LOG2E = math.log2(math.e)

V7X_VMEM_BYTES = 64 * 1024 * 1024
V7X_VMEM_LIMIT_BYTES = V7X_VMEM_BYTES - 8 * 1024 * 1024
V7X_VMEM_HIGH_LIMIT_BYTES = V7X_VMEM_BYTES - 4 * 1024 * 1024

SEQ_BLOCK = 256
LEFT_BLOCKS = LEFT_CHUNKS * CHUNK // SEQ_BLOCK
WINDOW = (LEFT_BLOCKS + 1) * SEQ_BLOCK
ATT_BLOCKS = 4
ATT_HEAD_GROUPS = 2
RET_BLOCKS = 4
TOEPLITZ_WIDTH = 1024
IN_PROJ_ROWS = 1024
OUT_PROJ_ROWS = 512
FFN_ROWS = 1024
FFN_CHUNK = 1024

F32 = jnp.float32
BF16 = jnp.bfloat16
BF16_SUBLANES = 16


def _params(n_axes, vmem_bytes=V7X_VMEM_LIMIT_BYTES):
    return pltpu.CompilerParams(
        dimension_semantics=("arbitrary",) * n_axes, vmem_limit_bytes=vmem_bytes)


def _dot(a, b):
    return jnp.dot(a, b, preferred_element_type=F32)


def _dot_nt(a, b):
    return lax.dot_general(a, b, (((1,), (1,)), ((), ())), preferred_element_type=F32)


def _dot_tn(a, b):
    return lax.dot_general(a, b, (((0,), (0,)), ((), ())), preferred_element_type=F32)


N_SECTIONS = 7
PLAIN_SECTIONS = (2, 3, 6)
HEADS_PER_SECTION = 8
IN_PROJ_TILES = 4
IN_PROJ_PARTS = (2, 2, 1, 1)
MXU_COLS = 256


def _in_proj_kernel(x_ref, g_ref, w_ref, cos_ref, sin_ref, qg_ref, kg_ref, *rest, n_cast):
    w_refs, o_ref, wb_refs = rest[:n_cast], rest[n_cast], rest[n_cast + 1:2 * n_cast + 1]
    h_ref, r_ref = rest[2 * n_cast + 1:]
    j = pl.program_id(1)
    heads_per_tile = o_ref.shape[1] // HEAD_DIM
    heads_per_dot = MXU_COLS // HEAD_DIM
    scale = HEAD_DIM ** -0.5

    def epilogue(section, a, rows):
        if section in (0, 1):
            y = a * cos_ref[rows, :] + pltpu.roll(a, HEAD_DIM // 2, 1) * sin_ref[rows, :]
            return y * scale if section == 1 else y
        if section in (4, 5):
            ms = jnp.mean(a * a, axis=-1, keepdims=True)
            y = a * lax.rsqrt(ms + EPS) * (qg_ref if section == 4 else kg_ref)[...]
            return y * (scale * LOG2E) if section == 4 else y
        return a

    def part_of(ref, part, parts):
        if parts == 1:
            return slice(0, ref.shape[0])
        n = ref.shape[0] // parts
        return pl.ds(pl.multiple_of(part * n, n), n)

    def tile_part(tile_index, part, carry):
        parts = IN_PROJ_PARTS[tile_index]
        rows = part_of(h_ref, part, parts)
        if tile_index == 0:
            x = x_ref[rows, :]
            ms = jnp.mean(x * x, axis=-1, keepdims=True)
            r_ref[rows, :] = jnp.broadcast_to(lax.rsqrt(ms + EPS), (x.shape[0], HEAD_DIM))
            h_ref[rows, :] = (x * g_ref[...]).astype(BF16)
        for src_ref, dst_ref in zip(w_refs, wb_refs):
            slab_rows = part_of(src_ref, part, parts)
            dst_ref[slab_rows, :] = src_ref[slab_rows, :].astype(BF16)
        section_of = lambda head: (tile_index * heads_per_tile + head) // HEADS_PER_SECTION
        firsts = sorted(range(0, heads_per_tile, heads_per_dot),
                        key=lambda first: section_of(first) in PLAIN_SECTIONS)
        for first in firsts:
            cols = slice(first * HEAD_DIM, (first + heads_per_dot) * HEAD_DIM)
            acc = _dot(h_ref[rows, :], w_ref[:, cols])
            for u in range(heads_per_dot):
                section = section_of(first + u)
                a = acc[:, u * HEAD_DIM:(u + 1) * HEAD_DIM] * r_ref[rows, :]
                lanes = slice((first + u) * HEAD_DIM, (first + u + 1) * HEAD_DIM)
                o_ref[rows, lanes] = epilogue(section, a, rows).astype(BF16)
        return carry

    def tile(tile_index):
        if IN_PROJ_PARTS[tile_index] == 1:
            tile_part(tile_index, 0, 0)
        else:
            lax.fori_loop(0, IN_PROJ_PARTS[tile_index], functools.partial(tile_part, tile_index), 0)

    for tile_index in range(IN_PROJ_TILES):
        pl.when(j == tile_index)(functools.partial(tile, tile_index))


def _in_proj(x2d, g1, w_bf16, cos2, sin2, qg, kg, weights, *, seq, tm):
    m, d = x2d.shape
    n = w_bf16.shape[1]
    tn = n // IN_PROJ_TILES
    assert n == N_SECTIONS * HEADS_PER_SECTION * HEAD_DIM and tn % MXU_COLS == 0
    assert m % tm == 0 and seq % tm == 0
    blocks_per_seq = seq // tm
    n_slabs = (m // tm) * IN_PROJ_TILES

    def slab(w):
        rows, cols = w.shape
        assert rows % (n_slabs * max(IN_PROJ_PARTS) * BF16_SUBLANES) == 0
        return pl.BlockSpec((rows // n_slabs, cols), lambda i, j: (i * IN_PROJ_TILES + j, 0))

    slabs = [slab(w) for w in weights]
    proj, *weights_bf16 = pl.pallas_call(
        functools.partial(_in_proj_kernel, n_cast=len(weights)),
        out_shape=[jax.ShapeDtypeStruct((m, n), BF16)]
                  + [jax.ShapeDtypeStruct(w.shape, BF16) for w in weights],
        grid=(m // tm, IN_PROJ_TILES),
        in_specs=[
            pl.BlockSpec((tm, d), lambda i, j: (i, 0)),
            pl.BlockSpec((1, d), lambda i, j: (0, 0)),
            pl.BlockSpec((d, tn), lambda i, j: (0, j)),
            pl.BlockSpec((tm, HEAD_DIM), lambda i, j: (i % blocks_per_seq, 0)),
            pl.BlockSpec((tm, HEAD_DIM), lambda i, j: (i % blocks_per_seq, 0)),
            pl.BlockSpec((1, HEAD_DIM), lambda i, j: (0, 0)),
            pl.BlockSpec((1, HEAD_DIM), lambda i, j: (0, 0)),
        ] + slabs,
        out_specs=[pl.BlockSpec((tm, tn), lambda i, j: (i, j))] + slabs,
        scratch_shapes=[pltpu.VMEM((tm, d), BF16), pltpu.VMEM((tm, HEAD_DIM), F32)],
        compiler_params=_params(2, V7X_VMEM_HIGH_LIMIT_BYTES),
        name="in_proj",
    )(x2d, g1, w_bf16, cos2, sin2, qg, kg, *weights)
    return proj, weights_bf16


def _bias_rows(rel_bias):
    h, rel_size = rel_bias.shape
    assert rel_size == CHUNK + REL_CLIP
    far = rel_bias[:, rel_size - 1:]
    near = rel_bias[:, :1]
    left = LEFT_CHUNKS * CHUNK - REL_CLIP
    return jnp.concatenate([
        jnp.broadcast_to(far, (h, left)),
        jnp.flip(rel_bias, axis=-1),
        jnp.broadcast_to(near, (h, WINDOW - left - rel_size)),
        jnp.broadcast_to(far, (h, TOEPLITZ_WIDTH - WINDOW)),
    ], axis=-1).astype(F32).reshape(h, 1, TOEPLITZ_WIDTH)


def _fill_bias_tile(g_ref, bias_ref):
    qc = lax.broadcasted_iota(jnp.int32, (SEQ_BLOCK, WINDOW), 0) // CHUNK
    kc = lax.broadcasted_iota(jnp.int32, (SEQ_BLOCK, WINDOW), 1) // CHUNK
    visible = (kc >= qc) & (kc <= qc + LEFT_CHUNKS)
    for h in range(ATT_HEADS):
        row = jnp.broadcast_to(g_ref[h], (SEQ_BLOCK, TOEPLITZ_WIDTH))
        t = pltpu.roll(row, 0, 1, stride=1, stride_axis=0)[:, :WINDOW]
        bias_ref[h] = jnp.where(visible, t * LOG2E, NEG)


def _retention_tables():
    h = np.arange(RET_HEADS, dtype=np.float64)
    log_g = np.log1p(-np.exp2(-(5.0 + h)))
    pos = np.arange(SEQ_BLOCK, dtype=np.float64)
    dist = np.abs(pos[:, None] - pos[None, :])
    vis = (pos[None, :] // CHUNK) <= (pos[:, None] // CHUNK)
    decay = np.exp(log_g[:, None, None] * dist) * vis
    q_dec = np.exp(log_g[None, :] * (pos + 1.0)[:, None])
    k_dec = np.exp(log_g[None, :] * (SEQ_BLOCK - 1.0 - pos)[:, None])
    blk_dec = np.exp(log_g * SEQ_BLOCK)
    lanes = lambda t: np.repeat(t, HEAD_DIM, axis=1).astype(np.float32)
    return decay.astype(np.float32), lanes(q_dec), lanes(k_dec), [float(v) for v in blk_dec]


def _retention_kernel(q_ref, k_ref, v_ref, gate_ref, dec_ref, qdec_ref, kdec_ref, gn_ref,
                      o_ref, st_ref, *, blocks_per_seq, blk_dec):
    @pl.when(pl.program_id(0) % blocks_per_seq == 0)
    def _():
        st_ref[...] = jnp.zeros_like(st_ref)

    for h in range(RET_HEADS):
        sl = slice(h * HEAD_DIM, (h + 1) * HEAD_DIM)
        state = st_ref[h]
        for blk in range(RET_BLOCKS):
            rows = slice(blk * SEQ_BLOCK, (blk + 1) * SEQ_BLOCK)
            q = q_ref[rows, sl]
            k = k_ref[rows, sl]
            v = v_ref[rows, sl]
            p = (_dot_nt(q, k) * dec_ref[h]).astype(BF16)
            y = _dot(p, v) + _dot(q, state.astype(BF16)) * qdec_ref[:, sl]
            kd = (k.astype(F32) * kdec_ref[:, sl]).astype(BF16)
            state = blk_dec[h] * state + _dot_tn(kd, v)
            mu = jnp.mean(y, axis=-1, keepdims=True)
            yc = y - mu
            var = jnp.mean(yc * yc, axis=-1, keepdims=True)
            yn = yc * lax.rsqrt(var + GN_EPS) * gn_ref[:, sl]
            gate = gate_ref[rows, sl].astype(F32)
            swish = gate * (0.5 + 0.5 * jnp.tanh(0.5 * gate))
            o_ref[rows, sl] = (swish * yn).astype(BF16)
        st_ref[h] = state


def _retention(proj, gn, *, seq):
    m = proj.shape[0]
    width = RET_HEADS * HEAD_DIM
    decay, q_dec, k_dec, blk_dec = _retention_tables()
    step_rows = RET_BLOCKS * SEQ_BLOCK
    assert seq % step_rows == 0
    blocks_per_seq = seq // step_rows
    section = lambda s: pl.BlockSpec((step_rows, width), lambda i: (i, s))
    whole = lambda shape: pl.BlockSpec(shape, lambda i: (0,) * len(shape))
    return pl.pallas_call(
        functools.partial(_retention_kernel, blocks_per_seq=blocks_per_seq, blk_dec=blk_dec),
        out_shape=jax.ShapeDtypeStruct((m, width), BF16),
        grid=(m // step_rows,),
        in_specs=[section(0), section(1), section(2), section(3),
                  whole(decay.shape), whole(q_dec.shape), whole(k_dec.shape), whole((1, width))],
        out_specs=pl.BlockSpec((step_rows, width), lambda i: (i, 0)),
        scratch_shapes=[pltpu.VMEM((RET_HEADS, HEAD_DIM, HEAD_DIM), F32)],
        compiler_params=_params(1),
        name="retention",
    )(proj, proj, proj, proj, decay, q_dec, k_dec, gn)


def _chunk_attn_kernel(q_ref, kp0_ref, kp1_ref, k_ref, vp0_ref, vp1_ref, v_ref, g_ref, *rest,
                       steps_per_seq, n_cast):
    w_refs, o_ref, wb_refs = rest[:n_cast], rest[n_cast], rest[n_cast + 1:2 * n_cast + 1]
    bias_ref = rest[2 * n_cast + 1]
    pl.when(pl.program_id(0) == 0)(functools.partial(_fill_bias_tile, g_ref, bias_ref))
    first_step = pl.program_id(0) % steps_per_seq == 0
    add = lambda a, b: a + b
    lane_tiles = SEQ_BLOCK // HEAD_DIM
    rows = lambda i0, i1: slice(i0 * SEQ_BLOCK, (i1 + 1) * SEQ_BLOCK)

    def key_block(prev_refs, main_ref, b, sl):
        if b < LEFT_BLOCKS:
            return prev_refs[b][:, sl]
        return main_ref[rows(b - LEFT_BLOCKS, b - LEFT_BLOCKS), sl]

    def attend_heads(first_block, group, carry):
        for w_ref, wb_ref in zip(w_refs, wb_refs):
            n = w_ref.shape[0] // ATT_HEAD_GROUPS
            slab_rows = pl.ds(pl.multiple_of(group * n, n), n)
            wb_ref[slab_rows, :] = w_ref[slab_rows, :].astype(BF16)
        key_blocks = range(first_block, ATT_BLOCKS + LEFT_BLOCKS)
        users = {b: (max(0, b - LEFT_BLOCKS), min(ATT_BLOCKS - 1, b)) for b in key_blocks}
        heads_per_group = ATT_HEADS // ATT_HEAD_GROUPS
        for head_in_group in range(heads_per_group):
            h = group * heads_per_group + head_in_group
            sl = pl.ds(pl.multiple_of(h * HEAD_DIM, HEAD_DIM), HEAD_DIM)
            tiles = {i: [] for i in range(ATT_BLOCKS)}
            for b in key_blocks:
                i0, i1 = users[b]
                s = _dot_nt(q_ref[rows(i0, i1), sl], key_block((kp0_ref, kp1_ref), k_ref, b, sl))
                for i in range(i0, i1 + 1):
                    w = b - i
                    sb = (s[rows(i - i0, i - i0), :]
                          + bias_ref[h, :, w * SEQ_BLOCK:(w + 1) * SEQ_BLOCK])
                    tiles[i] += [sb[:, c * HEAD_DIM:(c + 1) * HEAD_DIM] for c in range(lane_tiles)]
            probs, denom = {}, {}
            for i in range(ATT_BLOCKS):
                mx = jnp.max(functools.reduce(jnp.maximum, tiles[i]), axis=-1, keepdims=True)
                p = [jnp.exp2(tile - mx) for tile in tiles[i]]
                denom[i] = jnp.sum(functools.reduce(add, p), axis=-1, keepdims=True)
                visible = [b for b in key_blocks if users[b][0] <= i <= users[b][1]]
                for n, b in enumerate(visible):
                    probs[i, b] = jnp.concatenate(
                        p[n * lane_tiles:(n + 1) * lane_tiles], axis=1).astype(BF16)
            out = {i: [] for i in range(ATT_BLOCKS)}
            for b in key_blocks:
                i0, i1 = users[b]
                lhs = jnp.concatenate([probs[i, b] for i in range(i0, i1 + 1)], axis=0)
                pv = _dot(lhs, key_block((vp0_ref, vp1_ref), v_ref, b, sl))
                for i in range(i0, i1 + 1):
                    out[i].append(pv[rows(i - i0, i - i0), :])
            for i in range(ATT_BLOCKS):
                o_ref[rows(i, i), sl] = (functools.reduce(add, out[i]) / denom[i]).astype(BF16)
        return carry

    def attend(first_block):
        lax.fori_loop(0, ATT_HEAD_GROUPS, functools.partial(attend_heads, first_block), 0)

    pl.when(first_step)(functools.partial(attend, LEFT_BLOCKS))
    pl.when(jnp.logical_not(first_step))(functools.partial(attend, 0))


def _chunk_attn(proj, bias_rows, weights, *, seq):
    m = proj.shape[0]
    width = ATT_HEADS * HEAD_DIM
    step_rows = ATT_BLOCKS * SEQ_BLOCK
    assert seq % step_rows == 0 and ATT_BLOCKS >= LEFT_BLOCKS
    steps_per_seq = seq // step_rows
    steps = m // step_rows

    def previous(section, b):
        def index(i):
            seq_start = (i - i % steps_per_seq) * ATT_BLOCKS
            return (jnp.maximum(i * ATT_BLOCKS - (LEFT_BLOCKS - b), seq_start), section)
        return pl.BlockSpec((SEQ_BLOCK, width), index)

    main = lambda section: pl.BlockSpec((step_rows, width), lambda i: (i, section))

    def slab(w):
        rows, cols = w.shape
        assert rows % (steps * ATT_HEAD_GROUPS * BF16_SUBLANES) == 0
        return pl.BlockSpec((rows // steps, cols), lambda i: (i, 0))

    att, *weights_bf16 = pl.pallas_call(
        functools.partial(_chunk_attn_kernel, steps_per_seq=steps_per_seq, n_cast=len(weights)),
        out_shape=[jax.ShapeDtypeStruct((m, width), BF16)]
                  + [jax.ShapeDtypeStruct(w.shape, BF16) for w in weights],
        grid=(steps,),
        in_specs=[main(4)]
                 + [previous(5, b) for b in range(LEFT_BLOCKS)] + [main(5)]
                 + [previous(6, b) for b in range(LEFT_BLOCKS)] + [main(6)]
                 + [pl.BlockSpec(bias_rows.shape, lambda i: (0, 0, 0))]
                 + [slab(w) for w in weights],
        out_specs=[pl.BlockSpec((step_rows, width), lambda i: (i, 0))] + [slab(w) for w in weights],
        scratch_shapes=[pltpu.VMEM((ATT_HEADS, SEQ_BLOCK, WINDOW), F32)],
        compiler_params=_params(1),
        name="chunk_attn",
    )(proj, proj, proj, proj, proj, proj, proj, bias_rows, *weights)
    return att, weights_bf16


def _out_proj_kernel(ret_ref, att_ref, wr_ref, wa_ref, x_ref, o_ref):
    o_ref[...] = x_ref[...] + (_dot(ret_ref[...], wr_ref[...]) + _dot(att_ref[...], wa_ref[...]))


def _out_proj(ret, att, w_bf16, x2d, *, tm, tn):
    m, d = x2d.shape
    kr, ka = ret.shape[1], att.shape[1]
    assert kr == ka and kr + ka == w_bf16.shape[0] and m % tm == 0 and d % tn == 0
    return pl.pallas_call(
        _out_proj_kernel,
        out_shape=jax.ShapeDtypeStruct((m, d), F32),
        grid=(m // tm, d // tn),
        in_specs=[
            pl.BlockSpec((tm, kr), lambda i, j: (i, 0)),
            pl.BlockSpec((tm, ka), lambda i, j: (i, 0)),
            pl.BlockSpec((kr, tn), lambda i, j: (0, j)),
            pl.BlockSpec((ka, tn), lambda i, j: (1, j)),
            pl.BlockSpec((tm, tn), lambda i, j: (i, j)),
        ],
        out_specs=pl.BlockSpec((tm, tn), lambda i, j: (i, j)),
        compiler_params=_params(2),
        name="out_proj",
    )(ret, att, w_bf16, w_bf16, x2d)


def _ffn_kernel(x_ref, g_ref, w1_ref, w2_ref, o_ref, h_ref, r_ref):
    f = pl.program_id(1)

    def chunk():
        a = _dot(h_ref[...], w1_ref[...])
        hidden = []
        for c in range(a.shape[1] // HEAD_DIM):
            act = jnp.maximum(a[:, c * HEAD_DIM:(c + 1) * HEAD_DIM] * r_ref[...], 0.0)
            hidden.append((act * act).astype(BF16))
        o_ref[...] += _dot(jnp.concatenate(hidden, axis=1), w2_ref[...])

    @pl.when(f == 0)
    def _():
        x = x_ref[...]
        ms = jnp.mean(x * x, axis=-1, keepdims=True)
        r_ref[...] = jnp.broadcast_to(lax.rsqrt(ms + EPS), r_ref.shape)
        h_ref[...] = (x * g_ref[...]).astype(BF16)
        o_ref[...] = x
        chunk()

    pl.when(f > 0)(chunk)


def _ffn(x2d, g2, w1_bf16, w2_bf16, *, tm, tf):
    m, d = x2d.shape
    d_ff = w1_bf16.shape[1]
    assert m % tm == 0 and d_ff % tf == 0
    return pl.pallas_call(
        _ffn_kernel,
        out_shape=jax.ShapeDtypeStruct((m, d), F32),
        grid=(m // tm, d_ff // tf),
        in_specs=[
            pl.BlockSpec((tm, d), lambda i, f: (i, 0)),
            pl.BlockSpec((1, d), lambda i, f: (0, 0)),
            pl.BlockSpec((d, tf), lambda i, f: (0, f)),
            pl.BlockSpec((tf, d), lambda i, f: (f, 0)),
        ],
        out_specs=pl.BlockSpec((tm, d), lambda i, f: (i, 0)),
        scratch_shapes=[pltpu.VMEM((tm, d), BF16), pltpu.VMEM((tm, HEAD_DIM), F32)],
        compiler_params=_params(2, V7X_VMEM_HIGH_LIMIT_BYTES),
        name="ffn",
    )(x2d, g2, w1_bf16, w2_bf16)


def _rotary_tables(seq):
    half = HEAD_DIM // 2
    inv_freq = ROPE_BASE ** (-np.arange(half, dtype=np.float64) / half)
    ang = np.arange(seq, dtype=np.float64)[:, None] * inv_freq[None, :]
    cos, sin = np.cos(ang), np.sin(ang)
    cos2 = np.concatenate([cos, cos], axis=1).astype(np.float32)
    sin2 = np.concatenate([-sin, sin], axis=1).astype(np.float32)
    return cos2, sin2


def kernel(x, norm1_g, w_in, ret_norm_g, q_norm_g, k_norm_g, rel_bias, w_out, norm2_g, w_ff1, w_ff2):
    b, s, d = x.shape
    depth = w_in.shape[0]
    assert s % SEQ_BLOCK == 0 and SEQ_BLOCK % CHUNK == 0
    assert RET_HEADS == HEADS_PER_SECTION and ATT_HEADS == HEADS_PER_SECTION
    cos2, sin2 = _rotary_tables(s)
    row = lambda v: v.astype(F32).reshape(1, -1)
    x2d = x.reshape(b * s, d)
    for layer in range(depth):
        proj, (w1, w2) = _in_proj(x2d, row(norm1_g[layer]), w_in[layer].astype(BF16), cos2, sin2,
                                  row(q_norm_g[layer]), row(k_norm_g[layer]),
                                  (w_ff1[layer], w_ff2[layer]), seq=s, tm=IN_PROJ_ROWS)
        ret = _retention(proj, row(ret_norm_g[layer]), seq=s)
        att, (wo,) = _chunk_attn(proj, _bias_rows(rel_bias[layer]), (w_out[layer],), seq=s)
        x2d = _out_proj(ret, att, wo, x2d, tm=OUT_PROJ_ROWS, tn=d)
        x2d = _ffn(x2d, row(norm2_g[layer]), w1, w2, tm=FFN_ROWS, tf=FFN_CHUNK)
    return x2d.reshape(b, s, d)
```

```python
import functools
import math

import numpy as np
import jax
import jax.numpy as jnp
from jax import lax
from jax.experimental import pallas as pl
from jax.experimental.pallas import tpu as pltpu

CHUNK = 64
LEFT_CHUNKS = 8
REL_CLIP = 128
HEAD_DIM = 128
RET_HEADS = 8
ATT_HEADS = 8
ROPE_BASE = 10000.0
EPS = 1e-6
GN_EPS = 1e-5
NEG = -1e30
LOG2E = math.log2(math.e)

V7X_VMEM_BYTES = 64 * 1024 * 1024
V7X_VMEM_LIMIT_BYTES = V7X_VMEM_BYTES - 8 * 1024 * 1024
V7X_VMEM_HIGH_LIMIT_BYTES = V7X_VMEM_BYTES - 4 * 1024 * 1024

SEQ_BLOCK = 256
LEFT_BLOCKS = LEFT_CHUNKS * CHUNK // SEQ_BLOCK
WINDOW = (LEFT_BLOCKS + 1) * SEQ_BLOCK
ATT_BLOCKS = 4
ATT_HEAD_GROUPS = 1
RET_BLOCKS = 4
TOEPLITZ_WIDTH = 1024
IN_PROJ_ROWS = 1024
OUT_PROJ_ROWS = 512
FFN_ROWS = 1024
FFN_CHUNK = 1024

F32 = jnp.float32
BF16 = jnp.bfloat16
BF16_SUBLANES = 16


def _params(n_axes, vmem_bytes=V7X_VMEM_LIMIT_BYTES):
    return pltpu.CompilerParams(
        dimension_semantics=("arbitrary",) * n_axes, vmem_limit_bytes=vmem_bytes)


def _dot(a, b):
    return jnp.dot(a, b, preferred_element_type=F32)


def _dot_nt(a, b):
    return lax.dot_general(a, b, (((1,), (1,)), ((), ())), preferred_element_type=F32)


def _dot_tn(a, b):
    return lax.dot_general(a, b, (((0,), (0,)), ((), ())), preferred_element_type=F32)


N_SECTIONS = 7
PLAIN_SECTIONS = (2, 3, 6)
HEADS_PER_SECTION = 8
IN_PROJ_TILES = 4
IN_PROJ_PARTS = (2, 2, 1, 1)
MXU_COLS = 256


def _in_proj_kernel(x_ref, g_ref, w_ref, cos_ref, sin_ref, qg_ref, kg_ref, *rest, n_cast):
    w_refs, o_ref, wb_refs = rest[:n_cast], rest[n_cast], rest[n_cast + 1:2 * n_cast + 1]
    h_ref, r_ref = rest[2 * n_cast + 1:]
    j = pl.program_id(1)
    heads_per_tile = o_ref.shape[1] // HEAD_DIM
    heads_per_dot = MXU_COLS // HEAD_DIM
    scale = HEAD_DIM ** -0.5

    def epilogue(section, a, rows):
        if section in (0, 1):
            y = a * cos_ref[rows, :] + pltpu.roll(a, HEAD_DIM // 2, 1) * sin_ref[rows, :]
            return y * scale if section == 1 else y
        if section in (4, 5):
            ms = jnp.mean(a * a, axis=-1, keepdims=True)
            y = a * lax.rsqrt(ms + EPS) * (qg_ref if section == 4 else kg_ref)[...]
            return y * (scale * LOG2E) if section == 4 else y
        return a

    def part_of(ref, part, parts):
        if parts == 1:
            return slice(0, ref.shape[0])
        n = ref.shape[0] // parts
        return pl.ds(pl.multiple_of(part * n, n), n)

    def tile_part(tile_index, part, carry):
        parts = IN_PROJ_PARTS[tile_index]
        rows = part_of(h_ref, part, parts)
        if tile_index == 0:
            x = x_ref[rows, :]
            ms = jnp.mean(x * x, axis=-1, keepdims=True)
            r_ref[rows, :] = jnp.broadcast_to(lax.rsqrt(ms + EPS), (x.shape[0], HEAD_DIM))
            h_ref[rows, :] = (x * g_ref[...]).astype(BF16)
        for src_ref, dst_ref in zip(w_refs, wb_refs):
            slab_rows = part_of(src_ref, part, parts)
            dst_ref[slab_rows, :] = src_ref[slab_rows, :].astype(BF16)
        section_of = lambda head: (tile_index * heads_per_tile + head) // HEADS_PER_SECTION
        firsts = sorted(range(0, heads_per_tile, heads_per_dot),
                        key=lambda first: section_of(first) in PLAIN_SECTIONS)
        for first in firsts:
            cols = slice(first * HEAD_DIM, (first + heads_per_dot) * HEAD_DIM)
            acc = _dot(h_ref[rows, :], w_ref[:, cols])
            for u in range(heads_per_dot):
                section = section_of(first + u)
                a = acc[:, u * HEAD_DIM:(u + 1) * HEAD_DIM] * r_ref[rows, :]
                lanes = slice((first + u) * HEAD_DIM, (first + u + 1) * HEAD_DIM)
                o_ref[rows, lanes] = epilogue(section, a, rows).astype(BF16)
        return carry

    def tile(tile_index):
        if IN_PROJ_PARTS[tile_index] == 1:
            tile_part(tile_index, 0, 0)
        else:
            lax.fori_loop(0, IN_PROJ_PARTS[tile_index], functools.partial(tile_part, tile_index), 0)

    for tile_index in range(IN_PROJ_TILES):
        pl.when(j == tile_index)(functools.partial(tile, tile_index))


def _in_proj(x2d, g1, w_bf16, cos2, sin2, qg, kg, weights, *, seq, tm):
    m, d = x2d.shape
    n = w_bf16.shape[1]
    tn = n // IN_PROJ_TILES
    assert n == N_SECTIONS * HEADS_PER_SECTION * HEAD_DIM and tn % MXU_COLS == 0
    assert m % tm == 0 and seq % tm == 0
    blocks_per_seq = seq // tm
    n_slabs = (m // tm) * IN_PROJ_TILES

    def slab(w):
        rows, cols = w.shape
        assert rows % (n_slabs * max(IN_PROJ_PARTS) * BF16_SUBLANES) == 0
        return pl.BlockSpec((rows // n_slabs, cols), lambda i, j: (i * IN_PROJ_TILES + j, 0))

    slabs = [slab(w) for w in weights]
    proj, *weights_bf16 = pl.pallas_call(
        functools.partial(_in_proj_kernel, n_cast=len(weights)),
        out_shape=[jax.ShapeDtypeStruct((m, n), BF16)]
                  + [jax.ShapeDtypeStruct(w.shape, BF16) for w in weights],
        grid=(m // tm, IN_PROJ_TILES),
        in_specs=[
            pl.BlockSpec((tm, d), lambda i, j: (i, 0)),
            pl.BlockSpec((1, d), lambda i, j: (0, 0)),
            pl.BlockSpec((d, tn), lambda i, j: (0, j)),
            pl.BlockSpec((tm, HEAD_DIM), lambda i, j: (i % blocks_per_seq, 0)),
            pl.BlockSpec((tm, HEAD_DIM), lambda i, j: (i % blocks_per_seq, 0)),
            pl.BlockSpec((1, HEAD_DIM), lambda i, j: (0, 0)),
            pl.BlockSpec((1, HEAD_DIM), lambda i, j: (0, 0)),
        ] + slabs,
        out_specs=[pl.BlockSpec((tm, tn), lambda i, j: (i, j))] + slabs,
        scratch_shapes=[pltpu.VMEM((tm, d), BF16), pltpu.VMEM((tm, HEAD_DIM), F32)],
        compiler_params=_params(2, V7X_VMEM_HIGH_LIMIT_BYTES),
        name="in_proj",
    )(x2d, g1, w_bf16, cos2, sin2, qg, kg, *weights)
    return proj, weights_bf16


def _bias_rows(rel_bias):
    h, rel_size = rel_bias.shape
    assert rel_size == CHUNK + REL_CLIP
    far = rel_bias[:, rel_size - 1:]
    near = rel_bias[:, :1]
    left = LEFT_CHUNKS * CHUNK - REL_CLIP
    return jnp.concatenate([
        jnp.broadcast_to(far, (h, left)),
        jnp.flip(rel_bias, axis=-1),
        jnp.broadcast_to(near, (h, WINDOW - left - rel_size)),
        jnp.broadcast_to(far, (h, TOEPLITZ_WIDTH - WINDOW)),
    ], axis=-1).astype(F32).reshape(h, 1, TOEPLITZ_WIDTH)


def _fill_bias_tile(g_ref, bias_ref):
    qc = lax.broadcasted_iota(jnp.int32, (SEQ_BLOCK, WINDOW), 0) // CHUNK
    kc = lax.broadcasted_iota(jnp.int32, (SEQ_BLOCK, WINDOW), 1) // CHUNK
    visible = (kc >= qc) & (kc <= qc + LEFT_CHUNKS)
    for h in range(ATT_HEADS):
        row = jnp.broadcast_to(g_ref[h], (SEQ_BLOCK, TOEPLITZ_WIDTH))
        t = pltpu.roll(row, 0, 1, stride=1, stride_axis=0)[:, :WINDOW]
        bias_ref[h] = jnp.where(visible, t * LOG2E, NEG)


def _retention_tables():
    h = np.arange(RET_HEADS, dtype=np.float64)
    log_g = np.log1p(-np.exp2(-(5.0 + h)))
    pos = np.arange(SEQ_BLOCK, dtype=np.float64)
    dist = np.abs(pos[:, None] - pos[None, :])
    vis = (pos[None, :] // CHUNK) <= (pos[:, None] // CHUNK)
    decay = np.exp(log_g[:, None, None] * dist) * vis
    q_dec = np.exp(log_g[None, :] * (pos + 1.0)[:, None])
    k_dec = np.exp(log_g[None, :] * (SEQ_BLOCK - 1.0 - pos)[:, None])
    blk_dec = np.exp(log_g * SEQ_BLOCK)
    lanes = lambda t: np.repeat(t, HEAD_DIM, axis=1).astype(np.float32)
    return decay.astype(np.float32), lanes(q_dec), lanes(k_dec), [float(v) for v in blk_dec]


def _retention_kernel(q_ref, k_ref, v_ref, gate_ref, dec_ref, qdec_ref, kdec_ref, gn_ref,
                      o_ref, st_ref, *, blocks_per_seq, blk_dec):
    @pl.when(pl.program_id(0) % blocks_per_seq == 0)
    def _():
        st_ref[...] = jnp.zeros_like(st_ref)

    for h in range(RET_HEADS):
        sl = slice(h * HEAD_DIM, (h + 1) * HEAD_DIM)
        state = st_ref[h]
        for blk in range(RET_BLOCKS):
            rows = slice(blk * SEQ_BLOCK, (blk + 1) * SEQ_BLOCK)
            q = q_ref[rows, sl]
            k = k_ref[rows, sl]
            v = v_ref[rows, sl]
            p = (_dot_nt(q, k) * dec_ref[h]).astype(BF16)
            y = _dot(p, v) + _dot(q, state.astype(BF16)) * qdec_ref[:, sl]
            kd = (k.astype(F32) * kdec_ref[:, sl]).astype(BF16)
            state = blk_dec[h] * state + _dot_tn(kd, v)
            mu = jnp.mean(y, axis=-1, keepdims=True)
            yc = y - mu
            var = jnp.mean(yc * yc, axis=-1, keepdims=True)
            yn = yc * lax.rsqrt(var + GN_EPS) * gn_ref[:, sl]
            gate = gate_ref[rows, sl].astype(F32)
            swish = gate * (0.5 + 0.5 * jnp.tanh(0.5 * gate))
            o_ref[rows, sl] = (swish * yn).astype(BF16)
        st_ref[h] = state


def _retention(proj, gn, *, seq):
    m = proj.shape[0]
    width = RET_HEADS * HEAD_DIM
    decay, q_dec, k_dec, blk_dec = _retention_tables()
    step_rows = RET_BLOCKS * SEQ_BLOCK
    assert seq % step_rows == 0
    blocks_per_seq = seq // step_rows
    section = lambda s: pl.BlockSpec((step_rows, width), lambda i: (i, s))
    whole = lambda shape: pl.BlockSpec(shape, lambda i: (0,) * len(shape))
    return pl.pallas_call(
        functools.partial(_retention_kernel, blocks_per_seq=blocks_per_seq, blk_dec=blk_dec),
        out_shape=jax.ShapeDtypeStruct((m, width), BF16),
        grid=(m // step_rows,),
        in_specs=[section(0), section(1), section(2), section(3),
                  whole(decay.shape), whole(q_dec.shape), whole(k_dec.shape), whole((1, width))],
        out_specs=pl.BlockSpec((step_rows, width), lambda i: (i, 0)),
        scratch_shapes=[pltpu.VMEM((RET_HEADS, HEAD_DIM, HEAD_DIM), F32)],
        compiler_params=_params(1),
        name="retention",
    )(proj, proj, proj, proj, decay, q_dec, k_dec, gn)


def _chunk_attn_kernel(q_ref, kp0_ref, kp1_ref, k_ref, vp0_ref, vp1_ref, v_ref, g_ref, *rest,
                       steps_per_seq, n_cast):
    w_refs, o_ref, wb_refs = rest[:n_cast], rest[n_cast], rest[n_cast + 1:2 * n_cast + 1]
    bias_ref = rest[2 * n_cast + 1]
    pl.when(pl.program_id(0) == 0)(functools.partial(_fill_bias_tile, g_ref, bias_ref))
    first_step = pl.program_id(0) % steps_per_seq == 0
    add = lambda a, b: a + b
    lane_tiles = SEQ_BLOCK // HEAD_DIM
    rows = lambda i0, i1: slice(i0 * SEQ_BLOCK, (i1 + 1) * SEQ_BLOCK)

    def key_block(prev_refs, main_ref, b, sl):
        if b < LEFT_BLOCKS:
            return prev_refs[b][:, sl]
        return main_ref[rows(b - LEFT_BLOCKS, b - LEFT_BLOCKS), sl]

    def attend_heads(first_block, group, carry):
        rolled = ATT_HEAD_GROUPS > 1
        for w_ref, wb_ref in zip(w_refs, wb_refs):
            n = w_ref.shape[0] // ATT_HEAD_GROUPS
            slab_rows = pl.ds(pl.multiple_of(group * n, n), n) if rolled else slice(0, n)
            wb_ref[slab_rows, :] = w_ref[slab_rows, :].astype(BF16)
        key_blocks = range(first_block, ATT_BLOCKS + LEFT_BLOCKS)
        users = {b: (max(0, b - LEFT_BLOCKS), min(ATT_BLOCKS - 1, b)) for b in key_blocks}
        heads_per_group = ATT_HEADS // ATT_HEAD_GROUPS
        for head_in_group in range(heads_per_group):
            h = group * heads_per_group + head_in_group
            sl = (pl.ds(pl.multiple_of(h * HEAD_DIM, HEAD_DIM), HEAD_DIM) if rolled
                  else slice(h * HEAD_DIM, (h + 1) * HEAD_DIM))
            tiles = {i: [] for i in range(ATT_BLOCKS)}
            for b in key_blocks:
                i0, i1 = users[b]
                s = _dot_nt(q_ref[rows(i0, i1), sl], key_block((kp0_ref, kp1_ref), k_ref, b, sl))
                for i in range(i0, i1 + 1):
                    w = b - i
                    sb = (s[rows(i - i0, i - i0), :]
                          + bias_ref[h, :, w * SEQ_BLOCK:(w + 1) * SEQ_BLOCK])
                    tiles[i] += [sb[:, c * HEAD_DIM:(c + 1) * HEAD_DIM] for c in range(lane_tiles)]
            probs, denom = {}, {}
            for i in range(ATT_BLOCKS):
                mx = jnp.max(functools.reduce(jnp.maximum, tiles[i]), axis=-1, keepdims=True)
                p = [jnp.exp2(tile - mx) for tile in tiles[i]]
                denom[i] = jnp.sum(functools.reduce(add, p), axis=-1, keepdims=True)
                visible = [b for b in key_blocks if users[b][0] <= i <= users[b][1]]
                for n, b in enumerate(visible):
                    probs[i, b] = jnp.concatenate(
                        p[n * lane_tiles:(n + 1) * lane_tiles], axis=1).astype(BF16)
            out = {i: [] for i in range(ATT_BLOCKS)}
            for b in key_blocks:
                i0, i1 = users[b]
                lhs = jnp.concatenate([probs[i, b] for i in range(i0, i1 + 1)], axis=0)
                pv = _dot(lhs, key_block((vp0_ref, vp1_ref), v_ref, b, sl))
                for i in range(i0, i1 + 1):
                    out[i].append(pv[rows(i - i0, i - i0), :])
            for i in range(ATT_BLOCKS):
                o_ref[rows(i, i), sl] = (functools.reduce(add, out[i]) / denom[i]).astype(BF16)
        return carry

    def attend(first_block):
        if ATT_HEAD_GROUPS == 1:
            attend_heads(first_block, 0, 0)
        else:
            lax.fori_loop(0, ATT_HEAD_GROUPS, functools.partial(attend_heads, first_block), 0)

    pl.when(first_step)(functools.partial(attend, LEFT_BLOCKS))
    pl.when(jnp.logical_not(first_step))(functools.partial(attend, 0))


def _chunk_attn(proj, bias_rows, weights, *, seq):
    m = proj.shape[0]
    width = ATT_HEADS * HEAD_DIM
    step_rows = ATT_BLOCKS * SEQ_BLOCK
    assert seq % step_rows == 0 and ATT_BLOCKS >= LEFT_BLOCKS
    steps_per_seq = seq // step_rows
    steps = m // step_rows

    def previous(section, b):
        def index(i):
            seq_start = (i - i % steps_per_seq) * ATT_BLOCKS
            return (jnp.maximum(i * ATT_BLOCKS - (LEFT_BLOCKS - b), seq_start), section)
        return pl.BlockSpec((SEQ_BLOCK, width), index)

    main = lambda section: pl.BlockSpec((step_rows, width), lambda i: (i, section))

    def slab(w):
        rows, cols = w.shape
        assert rows % (steps * ATT_HEAD_GROUPS * BF16_SUBLANES) == 0
        return pl.BlockSpec((rows // steps, cols), lambda i: (i, 0))

    att, *weights_bf16 = pl.pallas_call(
        functools.partial(_chunk_attn_kernel, steps_per_seq=steps_per_seq, n_cast=len(weights)),
        out_shape=[jax.ShapeDtypeStruct((m, width), BF16)]
                  + [jax.ShapeDtypeStruct(w.shape, BF16) for w in weights],
        grid=(steps,),
        in_specs=[main(4)]
                 + [previous(5, b) for b in range(LEFT_BLOCKS)] + [main(5)]
                 + [previous(6, b) for b in range(LEFT_BLOCKS)] + [main(6)]
                 + [pl.BlockSpec(bias_rows.shape, lambda i: (0, 0, 0))]
                 + [slab(w) for w in weights],
        out_specs=[pl.BlockSpec((step_rows, width), lambda i: (i, 0))] + [slab(w) for w in weights],
        scratch_shapes=[pltpu.VMEM((ATT_HEADS, SEQ_BLOCK, WINDOW), F32)],
        compiler_params=_params(1),
        name="chunk_attn",
    )(proj, proj, proj, proj, proj, proj, proj, bias_rows, *weights)
    return att, weights_bf16


def _out_proj_kernel(ret_ref, att_ref, wr_ref, wa_ref, x_ref, o_ref):
    o_ref[...] = x_ref[...] + (_dot(ret_ref[...], wr_ref[...]) + _dot(att_ref[...], wa_ref[...]))


def _out_proj(ret, att, w_bf16, x2d, *, tm, tn):
    m, d = x2d.shape
    kr, ka = ret.shape[1], att.shape[1]
    assert kr == ka and kr + ka == w_bf16.shape[0] and m % tm == 0 and d % tn == 0
    return pl.pallas_call(
        _out_proj_kernel,
        out_shape=jax.ShapeDtypeStruct((m, d), F32),
        grid=(m // tm, d // tn),
        in_specs=[
            pl.BlockSpec((tm, kr), lambda i, j: (i, 0)),
            pl.BlockSpec((tm, ka), lambda i, j: (i, 0)),
            pl.BlockSpec((kr, tn), lambda i, j: (0, j)),
            pl.BlockSpec((ka, tn), lambda i, j: (1, j)),
            pl.BlockSpec((tm, tn), lambda i, j: (i, j)),
        ],
        out_specs=pl.BlockSpec((tm, tn), lambda i, j: (i, j)),
        compiler_params=_params(2),
        name="out_proj",
    )(ret, att, w_bf16, w_bf16, x2d)


def _ffn_kernel(x_ref, g_ref, w1_ref, w2_ref, o_ref, h_ref, r_ref):
    f = pl.program_id(1)

    def chunk():
        a = _dot(h_ref[...], w1_ref[...])
        hidden = []
        for c in range(a.shape[1] // HEAD_DIM):
            act = jnp.maximum(a[:, c * HEAD_DIM:(c + 1) * HEAD_DIM] * r_ref[...], 0.0)
            hidden.append((act * act).astype(BF16))
        o_ref[...] += _dot(jnp.concatenate(hidden, axis=1), w2_ref[...])

    @pl.when(f == 0)
    def _():
        x = x_ref[...]
        ms = jnp.mean(x * x, axis=-1, keepdims=True)
        r_ref[...] = jnp.broadcast_to(lax.rsqrt(ms + EPS), r_ref.shape)
        h_ref[...] = (x * g_ref[...]).astype(BF16)
        o_ref[...] = x
        chunk()

    pl.when(f > 0)(chunk)


def _ffn(x2d, g2, w1_bf16, w2_bf16, *, tm, tf):
    m, d = x2d.shape
    d_ff = w1_bf16.shape[1]
    assert m % tm == 0 and d_ff % tf == 0
    return pl.pallas_call(
        _ffn_kernel,
        out_shape=jax.ShapeDtypeStruct((m, d), F32),
        grid=(m // tm, d_ff // tf),
        in_specs=[
            pl.BlockSpec((tm, d), lambda i, f: (i, 0)),
            pl.BlockSpec((1, d), lambda i, f: (0, 0)),
            pl.BlockSpec((d, tf), lambda i, f: (0, f)),
            pl.BlockSpec((tf, d), lambda i, f: (f, 0)),
        ],
        out_specs=pl.BlockSpec((tm, d), lambda i, f: (i, 0)),
        scratch_shapes=[pltpu.VMEM((tm, d), BF16), pltpu.VMEM((tm, HEAD_DIM), F32)],
        compiler_params=_params(2, V7X_VMEM_HIGH_LIMIT_BYTES),
        name="ffn",
    )(x2d, g2, w1_bf16, w2_bf16)


def _rotary_tables(seq):
    half = HEAD_DIM // 2
    inv_freq = ROPE_BASE ** (-np.arange(half, dtype=np.float64) / half)
    ang = np.arange(seq, dtype=np.float64)[:, None] * inv_freq[None, :]
    cos, sin = np.cos(ang), np.sin(ang)
    cos2 = np.concatenate([cos, cos], axis=1).astype(np.float32)
    sin2 = np.concatenate([-sin, sin], axis=1).astype(np.float32)
    return cos2, sin2


def kernel(x, norm1_g, w_in, ret_norm_g, q_norm_g, k_norm_g, rel_bias, w_out, norm2_g, w_ff1, w_ff2):
    b, s, d = x.shape
    depth = w_in.shape[0]
    assert s % SEQ_BLOCK == 0 and SEQ_BLOCK % CHUNK == 0
    assert RET_HEADS == HEADS_PER_SECTION and ATT_HEADS == HEADS_PER_SECTION
    cos2, sin2 = _rotary_tables(s)
    row = lambda v: v.astype(F32).reshape(1, -1)
    x2d = x.reshape(b * s, d)
    for layer in range(depth):
        proj, (w1, w2) = _in_proj(x2d, row(norm1_g[layer]), w_in[layer].astype(BF16), cos2, sin2,
                                  row(q_norm_g[layer]), row(k_norm_g[layer]),
                                  (w_ff1[layer], w_ff2[layer]), seq=s, tm=IN_PROJ_ROWS)
        ret = _retention(proj, row(ret_norm_g[layer]), seq=s)
        att, (wo,) = _chunk_attn(proj, _bias_rows(rel_bias[layer]), (w_out[layer],), seq=s)
        x2d = _out_proj(ret, att, wo, x2d, tm=OUT_PROJ_ROWS, tn=d)
        x2d = _ffn(x2d, row(norm2_g[layer]), w1, w2, tm=FFN_ROWS, tf=FFN_CHUNK)
    return x2d.reshape(b, s, d)
```

```python
import functools
import math

import numpy as np
import jax
import jax.numpy as jnp
from jax import lax
from jax.experimental import pallas as pl
from jax.experimental.pallas import tpu as pltpu

CHUNK = 64
LEFT_CHUNKS = 8
REL_CLIP = 128
HEAD_DIM = 128
RET_HEADS = 8
ATT_HEADS = 8
ROPE_BASE = 10000.0
EPS = 1e-6
GN_EPS = 1e-5
NEG = -1e30
LOG2E = math.log2(math.e)

V7X_VMEM_BYTES = 64 * 1024 * 1024
V7X_VMEM_LIMIT_BYTES = V7X_VMEM_BYTES - 8 * 1024 * 1024
V7X_VMEM_HIGH_LIMIT_BYTES = V7X_VMEM_BYTES - 4 * 1024 * 1024

SEQ_BLOCK = 256
LEFT_BLOCKS = LEFT_CHUNKS * CHUNK // SEQ_BLOCK
WINDOW = (LEFT_BLOCKS + 1) * SEQ_BLOCK
ATT_BLOCKS = 4
ATT_HEAD_GROUPS = 2
RET_BLOCKS = 2
TOEPLITZ_WIDTH = 1024
IN_PROJ_ROWS = 1024
OUT_PROJ_ROWS = 512
FFN_ROWS = 1024
FFN_CHUNK = 1024

F32 = jnp.float32
BF16 = jnp.bfloat16
BF16_SUBLANES = 16


def _params(n_axes, vmem_bytes=V7X_VMEM_LIMIT_BYTES):
    return pltpu.CompilerParams(
        dimension_semantics=("arbitrary",) * n_axes, vmem_limit_bytes=vmem_bytes)


def _dot(a, b):
    return jnp.dot(a, b, preferred_element_type=F32)


def _dot_nt(a, b):
    return lax.dot_general(a, b, (((1,), (1,)), ((), ())), preferred_element_type=F32)


def _dot_tn(a, b):
    return lax.dot_general(a, b, (((0,), (0,)), ((), ())), preferred_element_type=F32)


N_SECTIONS = 7
PLAIN_SECTIONS = (2, 3, 6)
HEADS_PER_SECTION = 8
IN_PROJ_TILES = 4
IN_PROJ_PARTS = (2, 2, 2, 2)
MXU_COLS = 256


def _in_proj_kernel(x_ref, g_ref, w_ref, cos_ref, sin_ref, qg_ref, kg_ref, *rest, n_cast):
    w_refs, o_ref, wb_refs = rest[:n_cast], rest[n_cast], rest[n_cast + 1:2 * n_cast + 1]
    h_ref, r_ref = rest[2 * n_cast + 1:]
    j = pl.program_id(1)
    heads_per_tile = o_ref.shape[1] // HEAD_DIM
    heads_per_dot = MXU_COLS // HEAD_DIM
    scale = HEAD_DIM ** -0.5

    def epilogue(section, a, rows):
        if section in (0, 1):
            y = a * cos_ref[rows, :] + pltpu.roll(a, HEAD_DIM // 2, 1) * sin_ref[rows, :]
            return y * scale if section == 1 else y
        if section in (4, 5):
            ms = jnp.mean(a * a, axis=-1, keepdims=True)
            y = a * lax.rsqrt(ms + EPS) * (qg_ref if section == 4 else kg_ref)[...]
            return y * (scale * LOG2E) if section == 4 else y
        return a

    def part_of(ref, part, parts):
        if parts == 1:
            return slice(0, ref.shape[0])
        n = ref.shape[0] // parts
        return pl.ds(pl.multiple_of(part * n, n), n)

    def tile_part(tile_index, part, carry):
        parts = IN_PROJ_PARTS[tile_index]
        rows = part_of(h_ref, part, parts)
        if tile_index == 0:
            x = x_ref[rows, :]
            ms = jnp.mean(x * x, axis=-1, keepdims=True)
            r_ref[rows, :] = jnp.broadcast_to(lax.rsqrt(ms + EPS), (x.shape[0], HEAD_DIM))
            h_ref[rows, :] = (x * g_ref[...]).astype(BF16)
        for src_ref, dst_ref in zip(w_refs, wb_refs):
            slab_rows = part_of(src_ref, part, parts)
            dst_ref[slab_rows, :] = src_ref[slab_rows, :].astype(BF16)
        section_of = lambda head: (tile_index * heads_per_tile + head) // HEADS_PER_SECTION
        firsts = sorted(range(0, heads_per_tile, heads_per_dot),
                        key=lambda first: section_of(first) in PLAIN_SECTIONS)
        for first in firsts:
            cols = slice(first * HEAD_DIM, (first + heads_per_dot) * HEAD_DIM)
            acc = _dot(h_ref[rows, :], w_ref[:, cols])
            for u in range(heads_per_dot):
                section = section_of(first + u)
                a = acc[:, u * HEAD_DIM:(u + 1) * HEAD_DIM] * r_ref[rows, :]
                lanes = slice((first + u) * HEAD_DIM, (first + u + 1) * HEAD_DIM)
                o_ref[rows, lanes] = epilogue(section, a, rows).astype(BF16)
        return carry

    def tile(tile_index):
        if IN_PROJ_PARTS[tile_index] == 1:
            tile_part(tile_index, 0, 0)
        else:
            lax.fori_loop(0, IN_PROJ_PARTS[tile_index], functools.partial(tile_part, tile_index), 0)

    for tile_index in range(IN_PROJ_TILES):
        pl.when(j == tile_index)(functools.partial(tile, tile_index))


def _in_proj(x2d, g1, w_bf16, cos2, sin2, qg, kg, weights, *, seq, tm):
    m, d = x2d.shape
    n = w_bf16.shape[1]
    tn = n // IN_PROJ_TILES
    assert n == N_SECTIONS * HEADS_PER_SECTION * HEAD_DIM and tn % MXU_COLS == 0
    assert m % tm == 0 and seq % tm == 0
    blocks_per_seq = seq // tm
    n_slabs = (m // tm) * IN_PROJ_TILES

    def slab(w):
        rows, cols = w.shape
        assert rows % (n_slabs * max(IN_PROJ_PARTS) * BF16_SUBLANES) == 0
        return pl.BlockSpec((rows // n_slabs, cols), lambda i, j: (i * IN_PROJ_TILES + j, 0))

    slabs = [slab(w) for w in weights]
    proj, *weights_bf16 = pl.pallas_call(
        functools.partial(_in_proj_kernel, n_cast=len(weights)),
        out_shape=[jax.ShapeDtypeStruct((m, n), BF16)]
                  + [jax.ShapeDtypeStruct(w.shape, BF16) for w in weights],
        grid=(m // tm, IN_PROJ_TILES),
        in_specs=[
            pl.BlockSpec((tm, d), lambda i, j: (i, 0)),
            pl.BlockSpec((1, d), lambda i, j: (0, 0)),
            pl.BlockSpec((d, tn), lambda i, j: (0, j)),
            pl.BlockSpec((tm, HEAD_DIM), lambda i, j: (i % blocks_per_seq, 0)),
            pl.BlockSpec((tm, HEAD_DIM), lambda i, j: (i % blocks_per_seq, 0)),
            pl.BlockSpec((1, HEAD_DIM), lambda i, j: (0, 0)),
            pl.BlockSpec((1, HEAD_DIM), lambda i, j: (0, 0)),
        ] + slabs,
        out_specs=[pl.BlockSpec((tm, tn), lambda i, j: (i, j))] + slabs,
        scratch_shapes=[pltpu.VMEM((tm, d), BF16), pltpu.VMEM((tm, HEAD_DIM), F32)],
        compiler_params=_params(2, V7X_VMEM_HIGH_LIMIT_BYTES),
        name="in_proj",
    )(x2d, g1, w_bf16, cos2, sin2, qg, kg, *weights)
    return proj, weights_bf16


def _bias_rows(rel_bias):
    h, rel_size = rel_bias.shape
    assert rel_size == CHUNK + REL_CLIP
    far = rel_bias[:, rel_size - 1:]
    near = rel_bias[:, :1]
    left = LEFT_CHUNKS * CHUNK - REL_CLIP
    return jnp.concatenate([
        jnp.broadcast_to(far, (h, left)),
        jnp.flip(rel_bias, axis=-1),
        jnp.broadcast_to(near, (h, WINDOW - left - rel_size)),
        jnp.broadcast_to(far, (h, TOEPLITZ_WIDTH - WINDOW)),
    ], axis=-1).astype(F32).reshape(h, 1, TOEPLITZ_WIDTH)


def _fill_bias_tile(g_ref, bias_ref):
    qc = lax.broadcasted_iota(jnp.int32, (SEQ_BLOCK, WINDOW), 0) // CHUNK
    kc = lax.broadcasted_iota(jnp.int32, (SEQ_BLOCK, WINDOW), 1) // CHUNK
    visible = (kc >= qc) & (kc <= qc + LEFT_CHUNKS)
    for h in range(ATT_HEADS):
        row = jnp.broadcast_to(g_ref[h], (SEQ_BLOCK, TOEPLITZ_WIDTH))
        t = pltpu.roll(row, 0, 1, stride=1, stride_axis=0)[:, :WINDOW]
        bias_ref[h] = jnp.where(visible, t * LOG2E, NEG)


def _retention_tables():
    h = np.arange(RET_HEADS, dtype=np.float64)
    log_g = np.log1p(-np.exp2(-(5.0 + h)))
    pos = np.arange(SEQ_BLOCK, dtype=np.float64)
    dist = np.abs(pos[:, None] - pos[None, :])
    vis = (pos[None, :] // CHUNK) <= (pos[:, None] // CHUNK)
    decay = np.exp(log_g[:, None, None] * dist) * vis
    q_dec = np.exp(log_g[None, :] * (pos + 1.0)[:, None])
    k_dec = np.exp(log_g[None, :] * (SEQ_BLOCK - 1.0 - pos)[:, None])
    blk_dec = np.exp(log_g * SEQ_BLOCK)
    lanes = lambda t: np.repeat(t, HEAD_DIM, axis=1).astype(np.float32)
    return decay.astype(np.float32), lanes(q_dec), lanes(k_dec), [float(v) for v in blk_dec]


def _retention_kernel(q_ref, k_ref, v_ref, gate_ref, dec_ref, qdec_ref, kdec_ref, gn_ref,
                      o_ref, st_ref, *, blocks_per_seq, blk_dec):
    @pl.when(pl.program_id(0) % blocks_per_seq == 0)
    def _():
        st_ref[...] = jnp.zeros_like(st_ref)

    for h in range(RET_HEADS):
        sl = slice(h * HEAD_DIM, (h + 1) * HEAD_DIM)
        state = st_ref[h]
        for blk in range(RET_BLOCKS):
            rows = slice(blk * SEQ_BLOCK, (blk + 1) * SEQ_BLOCK)
            q = q_ref[rows, sl]
            k = k_ref[rows, sl]
            v = v_ref[rows, sl]
            p = (_dot_nt(q, k) * dec_ref[h]).astype(BF16)
            y = _dot(p, v) + _dot(q, state.astype(BF16)) * qdec_ref[:, sl]
            kd = (k.astype(F32) * kdec_ref[:, sl]).astype(BF16)
            state = blk_dec[h] * state + _dot_tn(kd, v)
            mu = jnp.mean(y, axis=-1, keepdims=True)
            yc = y - mu
            var = jnp.mean(yc * yc, axis=-1, keepdims=True)
            yn = yc * lax.rsqrt(var + GN_EPS) * gn_ref[:, sl]
            gate = gate_ref[rows, sl].astype(F32)
            swish = gate * (0.5 + 0.5 * jnp.tanh(0.5 * gate))
            o_ref[rows, sl] = (swish * yn).astype(BF16)
        st_ref[h] = state


def _retention(proj, gn, *, seq):
    m = proj.shape[0]
    width = RET_HEADS * HEAD_DIM
    decay, q_dec, k_dec, blk_dec = _retention_tables()
    step_rows = RET_BLOCKS * SEQ_BLOCK
    assert seq % step_rows == 0
    blocks_per_seq = seq // step_rows
    section = lambda s: pl.BlockSpec((step_rows, width), lambda i: (i, s))
    whole = lambda shape: pl.BlockSpec(shape, lambda i: (0,) * len(shape))
    return pl.pallas_call(
        functools.partial(_retention_kernel, blocks_per_seq=blocks_per_seq, blk_dec=blk_dec),
        out_shape=jax.ShapeDtypeStruct((m, width), BF16),
        grid=(m // step_rows,),
        in_specs=[section(0), section(1), section(2), section(3),
                  whole(decay.shape), whole(q_dec.shape), whole(k_dec.shape), whole((1, width))],
        out_specs=pl.BlockSpec((step_rows, width), lambda i: (i, 0)),
        scratch_shapes=[pltpu.VMEM((RET_HEADS, HEAD_DIM, HEAD_DIM), F32)],
        compiler_params=_params(1),
        name="retention",
    )(proj, proj, proj, proj, decay, q_dec, k_dec, gn)


def _chunk_attn_kernel(q_ref, kp0_ref, kp1_ref, k_ref, vp0_ref, vp1_ref, v_ref, g_ref, *rest,
                       steps_per_seq, n_cast):
    w_refs, o_ref, wb_refs = rest[:n_cast], rest[n_cast], rest[n_cast + 1:2 * n_cast + 1]
    bias_ref = rest[2 * n_cast + 1]
    pl.when(pl.program_id(0) == 0)(functools.partial(_fill_bias_tile, g_ref, bias_ref))
    first_step = pl.program_id(0) % steps_per_seq == 0
    add = lambda a, b: a + b
    lane_tiles = SEQ_BLOCK // HEAD_DIM
    rows = lambda i0, i1: slice(i0 * SEQ_BLOCK, (i1 + 1) * SEQ_BLOCK)

    def key_block(prev_refs, main_ref, b, sl):
        if b < LEFT_BLOCKS:
            return prev_refs[b][:, sl]
        return main_ref[rows(b - LEFT_BLOCKS, b - LEFT_BLOCKS), sl]

    def attend_heads(first_block, group, carry):
        for w_ref, wb_ref in zip(w_refs, wb_refs):
            n = w_ref.shape[0] // ATT_HEAD_GROUPS
            slab_rows = pl.ds(pl.multiple_of(group * n, n), n)
            wb_ref[slab_rows, :] = w_ref[slab_rows, :].astype(BF16)
        key_blocks = range(first_block, ATT_BLOCKS + LEFT_BLOCKS)
        users = {b: (max(0, b - LEFT_BLOCKS), min(ATT_BLOCKS - 1, b)) for b in key_blocks}
        heads_per_group = ATT_HEADS // ATT_HEAD_GROUPS
        for head_in_group in range(heads_per_group):
            h = group * heads_per_group + head_in_group
            sl = pl.ds(pl.multiple_of(h * HEAD_DIM, HEAD_DIM), HEAD_DIM)
            tiles = {i: [] for i in range(ATT_BLOCKS)}
            for b in key_blocks:
                i0, i1 = users[b]
                s = _dot_nt(q_ref[rows(i0, i1), sl], key_block((kp0_ref, kp1_ref), k_ref, b, sl))
                for i in range(i0, i1 + 1):
                    w = b - i
                    sb = (s[rows(i - i0, i - i0), :]
                          + bias_ref[h, :, w * SEQ_BLOCK:(w + 1) * SEQ_BLOCK])
                    tiles[i] += [sb[:, c * HEAD_DIM:(c + 1) * HEAD_DIM] for c in range(lane_tiles)]
            probs, denom = {}, {}
            for i in range(ATT_BLOCKS):
                mx = jnp.max(functools.reduce(jnp.maximum, tiles[i]), axis=-1, keepdims=True)
                p = [jnp.exp2(tile - mx) for tile in tiles[i]]
                denom[i] = jnp.sum(functools.reduce(add, p), axis=-1, keepdims=True)
                visible = [b for b in key_blocks if users[b][0] <= i <= users[b][1]]
                for n, b in enumerate(visible):
                    probs[i, b] = jnp.concatenate(
                        p[n * lane_tiles:(n + 1) * lane_tiles], axis=1).astype(BF16)
            out = {i: [] for i in range(ATT_BLOCKS)}
            for b in key_blocks:
                i0, i1 = users[b]
                lhs = jnp.concatenate([probs[i, b] for i in range(i0, i1 + 1)], axis=0)
                pv = _dot(lhs, key_block((vp0_ref, vp1_ref), v_ref, b, sl))
                for i in range(i0, i1 + 1):
                    out[i].append(pv[rows(i - i0, i - i0), :])
            for i in range(ATT_BLOCKS):
                o_ref[rows(i, i), sl] = (functools.reduce(add, out[i]) / denom[i]).astype(BF16)
        return carry

    def attend(first_block):
        lax.fori_loop(0, ATT_HEAD_GROUPS, functools.partial(attend_heads, first_block), 0)

    pl.when(first_step)(functools.partial(attend, LEFT_BLOCKS))
    pl.when(jnp.logical_not(first_step))(functools.partial(attend, 0))


def _chunk_attn(proj, bias_rows, weights, *, seq):
    m = proj.shape[0]
    width = ATT_HEADS * HEAD_DIM
    step_rows = ATT_BLOCKS * SEQ_BLOCK
    assert seq % step_rows == 0 and ATT_BLOCKS >= LEFT_BLOCKS
    steps_per_seq = seq // step_rows
    steps = m // step_rows

    def previous(section, b):
        def index(i):
            seq_start = (i - i % steps_per_seq) * ATT_BLOCKS
            return (jnp.maximum(i * ATT_BLOCKS - (LEFT_BLOCKS - b), seq_start), section)
        return pl.BlockSpec((SEQ_BLOCK, width), index)

    main = lambda section: pl.BlockSpec((step_rows, width), lambda i: (i, section))

    def slab(w):
        rows, cols = w.shape
        assert rows % (steps * ATT_HEAD_GROUPS * BF16_SUBLANES) == 0
        return pl.BlockSpec((rows // steps, cols), lambda i: (i, 0))

    att, *weights_bf16 = pl.pallas_call(
        functools.partial(_chunk_attn_kernel, steps_per_seq=steps_per_seq, n_cast=len(weights)),
        out_shape=[jax.ShapeDtypeStruct((m, width), BF16)]
                  + [jax.ShapeDtypeStruct(w.shape, BF16) for w in weights],
        grid=(steps,),
        in_specs=[main(4)]
                 + [previous(5, b) for b in range(LEFT_BLOCKS)] + [main(5)]
                 + [previous(6, b) for b in range(LEFT_BLOCKS)] + [main(6)]
                 + [pl.BlockSpec(bias_rows.shape, lambda i: (0, 0, 0))]
                 + [slab(w) for w in weights],
        out_specs=[pl.BlockSpec((step_rows, width), lambda i: (i, 0))] + [slab(w) for w in weights],
        scratch_shapes=[pltpu.VMEM((ATT_HEADS, SEQ_BLOCK, WINDOW), F32)],
        compiler_params=_params(1),
        name="chunk_attn",
    )(proj, proj, proj, proj, proj, proj, proj, bias_rows, *weights)
    return att, weights_bf16


def _out_proj_kernel(ret_ref, att_ref, wr_ref, wa_ref, x_ref, o_ref):
    o_ref[...] = x_ref[...] + (_dot(ret_ref[...], wr_ref[...]) + _dot(att_ref[...], wa_ref[...]))


def _out_proj(ret, att, w_bf16, x2d, *, tm, tn):
    m, d = x2d.shape
    kr, ka = ret.shape[1], att.shape[1]
    assert kr == ka and kr + ka == w_bf16.shape[0] and m % tm == 0 and d % tn == 0
    return pl.pallas_call(
        _out_proj_kernel,
        out_shape=jax.ShapeDtypeStruct((m, d), F32),
        grid=(m // tm, d // tn),
        in_specs=[
            pl.BlockSpec((tm, kr), lambda i, j: (i, 0)),
            pl.BlockSpec((tm, ka), lambda i, j: (i, 0)),
            pl.BlockSpec((kr, tn), lambda i, j: (0, j)),
            pl.BlockSpec((ka, tn), lambda i, j: (1, j)),
            pl.BlockSpec((tm, tn), lambda i, j: (i, j)),
        ],
        out_specs=pl.BlockSpec((tm, tn), lambda i, j: (i, j)),
        compiler_params=_params(2),
        name="out_proj",
    )(ret, att, w_bf16, w_bf16, x2d)


def _ffn_kernel(x_ref, g_ref, w1_ref, w2_ref, o_ref, h_ref, r_ref):
    f = pl.program_id(1)

    def chunk():
        a = _dot(h_ref[...], w1_ref[...])
        hidden = []
        for c in range(a.shape[1] // HEAD_DIM):
            act = jnp.maximum(a[:, c * HEAD_DIM:(c + 1) * HEAD_DIM] * r_ref[...], 0.0)
            hidden.append((act * act).astype(BF16))
        o_ref[...] += _dot(jnp.concatenate(hidden, axis=1), w2_ref[...])

    @pl.when(f == 0)
    def _():
        x = x_ref[...]
        ms = jnp.mean(x * x, axis=-1, keepdims=True)
        r_ref[...] = jnp.broadcast_to(lax.rsqrt(ms + EPS), r_ref.shape)
        h_ref[...] = (x * g_ref[...]).astype(BF16)
        o_ref[...] = x
        chunk()

    pl.when(f > 0)(chunk)


def _ffn(x2d, g2, w1_bf16, w2_bf16, *, tm, tf):
    m, d = x2d.shape
    d_ff = w1_bf16.shape[1]
    assert m % tm == 0 and d_ff % tf == 0
    return pl.pallas_call(
        _ffn_kernel,
        out_shape=jax.ShapeDtypeStruct((m, d), F32),
        grid=(m // tm, d_ff // tf),
        in_specs=[
            pl.BlockSpec((tm, d), lambda i, f: (i, 0)),
            pl.BlockSpec((1, d), lambda i, f: (0, 0)),
            pl.BlockSpec((d, tf), lambda i, f: (0, f)),
            pl.BlockSpec((tf, d), lambda i, f: (f, 0)),
        ],
        out_specs=pl.BlockSpec((tm, d), lambda i, f: (i, 0)),
        scratch_shapes=[pltpu.VMEM((tm, d), BF16), pltpu.VMEM((tm, HEAD_DIM), F32)],
        compiler_params=_params(2, V7X_VMEM_HIGH_LIMIT_BYTES),
        name="ffn",
    )(x2d, g2, w1_bf16, w2_bf16)


def _rotary_tables(seq):
    half = HEAD_DIM // 2
    inv_freq = ROPE_BASE ** (-np.arange(half, dtype=np.float64) / half)
    ang = np.arange(seq, dtype=np.float64)[:, None] * inv_freq[None, :]
    cos, sin = np.cos(ang), np.sin(ang)
    cos2 = np.concatenate([cos, cos], axis=1).astype(np.float32)
    sin2 = np.concatenate([-sin, sin], axis=1).astype(np.float32)
    return cos2, sin2


def kernel(x, norm1_g, w_in, ret_norm_g, q_norm_g, k_norm_g, rel_bias, w_out, norm2_g, w_ff1, w_ff2):
    b, s, d = x.shape
    depth = w_in.shape[0]
    assert s % SEQ_BLOCK == 0 and SEQ_BLOCK % CHUNK == 0
    assert RET_HEADS == HEADS_PER_SECTION and ATT_HEADS == HEADS_PER_SECTION
    cos2, sin2 = _rotary_tables(s)
    row = lambda v: v.astype(F32).reshape(1, -1)
    x2d = x.reshape(b * s, d)
    for layer in range(depth):
        proj, (w1, w2) = _in_proj(x2d, row(norm1_g[layer]), w_in[layer].astype(BF16), cos2, sin2,
                                  row(q_norm_g[layer]), row(k_norm_g[layer]),
                                  (w_ff1[layer], w_ff2[layer]), seq=s, tm=IN_PROJ_ROWS)
        ret = _retention(proj, row(ret_norm_g[layer]), seq=s)
        att, (wo,) = _chunk_attn(proj, _bias_rows(rel_bias[layer]), (w_out[layer],), seq=s)
        x2d = _out_proj(ret, att, wo, x2d, tm=OUT_PROJ_ROWS, tn=d)
        x2d = _ffn(x2d, row(norm2_g[layer]), w1, w2, tm=FFN_ROWS, tf=FFN_CHUNK)
    return x2d.reshape(b, s, d)
```

```python
import functools
import math

import numpy as np
import jax
import jax.numpy as jnp
from jax import lax
from jax.experimental import pallas as pl
from jax.experimental.pallas import tpu as pltpu

CHUNK = 64
LEFT_CHUNKS = 8
REL_CLIP = 128
HEAD_DIM = 128
RET_HEADS = 8
ATT_HEADS = 8
ROPE_BASE = 10000.0
EPS = 1e-6
GN_EPS = 1e-5
NEG = -1e30
LOG2E = math.log2(math.e)

V7X_VMEM_BYTES = 64 * 1024 * 1024
V7X_VMEM_LIMIT_BYTES = V7X_VMEM_BYTES - 8 * 1024 * 1024
V7X_VMEM_HIGH_LIMIT_BYTES = V7X_VMEM_BYTES - 4 * 1024 * 1024

SEQ_BLOCK = 256
LEFT_BLOCKS = LEFT_CHUNKS * CHUNK // SEQ_BLOCK
WINDOW = (LEFT_BLOCKS + 1) * SEQ_BLOCK
ATT_BLOCKS = 4
ATT_HEAD_GROUPS = 2
RET_BLOCKS = 4
TOEPLITZ_WIDTH = 1024
IN_PROJ_ROWS = 1024
OUT_PROJ_ROWS = 512
FFN_ROWS = 1024
FFN_CHUNK = 1024

F32 = jnp.float32
BF16 = jnp.bfloat16
BF16_SUBLANES = 16


def _params(n_axes, vmem_bytes=V7X_VMEM_LIMIT_BYTES):
    return pltpu.CompilerParams(
        dimension_semantics=("arbitrary",) * n_axes, vmem_limit_bytes=vmem_bytes)


def _dot(a, b):
    return jnp.dot(a, b, preferred_element_type=F32)


def _dot_nt(a, b):
    return lax.dot_general(a, b, (((1,), (1,)), ((), ())), preferred_element_type=F32)


def _dot_tn(a, b):
    return lax.dot_general(a, b, (((0,), (0,)), ((), ())), preferred_element_type=F32)


N_SECTIONS = 7
PLAIN_SECTIONS = (2, 3, 6)
HEADS_PER_SECTION = 8
IN_PROJ_TILES = 4
IN_PROJ_PARTS = (2, 1, 2, 1)
MXU_COLS = 256


def _in_proj_kernel(x_ref, g_ref, w_ref, cos_ref, sin_ref, qg_ref, kg_ref, *rest, n_cast):
    w_refs, o_ref, wb_refs = rest[:n_cast], rest[n_cast], rest[n_cast + 1:2 * n_cast + 1]
    h_ref, r_ref = rest[2 * n_cast + 1:]
    j = pl.program_id(1)
    heads_per_tile = o_ref.shape[1] // HEAD_DIM
    heads_per_dot = MXU_COLS // HEAD_DIM
    scale = HEAD_DIM ** -0.5

    def epilogue(section, a, rows):
        if section in (0, 1):
            y = a * cos_ref[rows, :] + pltpu.roll(a, HEAD_DIM // 2, 1) * sin_ref[rows, :]
            return y * scale if section == 1 else y
        if section in (4, 5):
            ms = jnp.mean(a * a, axis=-1, keepdims=True)
            y = a * lax.rsqrt(ms + EPS) * (qg_ref if section == 4 else kg_ref)[...]
            return y * (scale * LOG2E) if section == 4 else y
        return a

    def part_of(ref, part, parts):
        if parts == 1:
            return slice(0, ref.shape[0])
        n = ref.shape[0] // parts
        return pl.ds(pl.multiple_of(part * n, n), n)

    def tile_part(tile_index, part, carry):
        parts = IN_PROJ_PARTS[tile_index]
        rows = part_of(h_ref, part, parts)
        if tile_index == 0:
            x = x_ref[rows, :]
            ms = jnp.mean(x * x, axis=-1, keepdims=True)
            r_ref[rows, :] = jnp.broadcast_to(lax.rsqrt(ms + EPS), (x.shape[0], HEAD_DIM))
            h_ref[rows, :] = (x * g_ref[...]).astype(BF16)
        for src_ref, dst_ref in zip(w_refs, wb_refs):
            slab_rows = part_of(src_ref, part, parts)
            dst_ref[slab_rows, :] = src_ref[slab_rows, :].astype(BF16)
        section_of = lambda head: (tile_index * heads_per_tile + head) // HEADS_PER_SECTION
        firsts = sorted(range(0, heads_per_tile, heads_per_dot),
                        key=lambda first: section_of(first) in PLAIN_SECTIONS)
        for first in firsts:
            cols = slice(first * HEAD_DIM, (first + heads_per_dot) * HEAD_DIM)
            acc = _dot(h_ref[rows, :], w_ref[:, cols])
            for u in range(heads_per_dot):
                section = section_of(first + u)
                a = acc[:, u * HEAD_DIM:(u + 1) * HEAD_DIM] * r_ref[rows, :]
                lanes = slice((first + u) * HEAD_DIM, (first + u + 1) * HEAD_DIM)
                o_ref[rows, lanes] = epilogue(section, a, rows).astype(BF16)
        return carry

    def tile(tile_index):
        if IN_PROJ_PARTS[tile_index] == 1:
            tile_part(tile_index, 0, 0)
        else:
            lax.fori_loop(0, IN_PROJ_PARTS[tile_index], functools.partial(tile_part, tile_index), 0)

    for tile_index in range(IN_PROJ_TILES):
        pl.when(j == tile_index)(functools.partial(tile, tile_index))


def _in_proj(x2d, g1, w_bf16, cos2, sin2, qg, kg, weights, *, seq, tm):
    m, d = x2d.shape
    n = w_bf16.shape[1]
    tn = n // IN_PROJ_TILES
    assert n == N_SECTIONS * HEADS_PER_SECTION * HEAD_DIM and tn % MXU_COLS == 0
    assert m % tm == 0 and seq % tm == 0
    blocks_per_seq = seq // tm
    n_slabs = (m // tm) * IN_PROJ_TILES

    def slab(w):
        rows, cols = w.shape
        assert rows % (n_slabs * max(IN_PROJ_PARTS) * BF16_SUBLANES) == 0
        return pl.BlockSpec((rows // n_slabs, cols), lambda i, j: (i * IN_PROJ_TILES + j, 0))

    slabs = [slab(w) for w in weights]
    proj, *weights_bf16 = pl.pallas_call(
        functools.partial(_in_proj_kernel, n_cast=len(weights)),
        out_shape=[jax.ShapeDtypeStruct((m, n), BF16)]
                  + [jax.ShapeDtypeStruct(w.shape, BF16) for w in weights],
        grid=(m // tm, IN_PROJ_TILES),
        in_specs=[
            pl.BlockSpec((tm, d), lambda i, j: (i, 0)),
            pl.BlockSpec((1, d), lambda i, j: (0, 0)),
            pl.BlockSpec((d, tn), lambda i, j: (0, j)),
            pl.BlockSpec((tm, HEAD_DIM), lambda i, j: (i % blocks_per_seq, 0)),
            pl.BlockSpec((tm, HEAD_DIM), lambda i, j: (i % blocks_per_seq, 0)),
            pl.BlockSpec((1, HEAD_DIM), lambda i, j: (0, 0)),
            pl.BlockSpec((1, HEAD_DIM), lambda i, j: (0, 0)),
        ] + slabs,
        out_specs=[pl.BlockSpec((tm, tn), lambda i, j: (i, j))] + slabs,
        scratch_shapes=[pltpu.VMEM((tm, d), BF16), pltpu.VMEM((tm, HEAD_DIM), F32)],
        compiler_params=_params(2, V7X_VMEM_HIGH_LIMIT_BYTES),
        name="in_proj",
    )(x2d, g1, w_bf16, cos2, sin2, qg, kg, *weights)
    return proj, weights_bf16


def _bias_rows(rel_bias):
    h, rel_size = rel_bias.shape
    assert rel_size == CHUNK + REL_CLIP
    far = rel_bias[:, rel_size - 1:]
    near = rel_bias[:, :1]
    left = LEFT_CHUNKS * CHUNK - REL_CLIP
    return jnp.concatenate([
        jnp.broadcast_to(far, (h, left)),
        jnp.flip(rel_bias, axis=-1),
        jnp.broadcast_to(near, (h, WINDOW - left - rel_size)),
        jnp.broadcast_to(far, (h, TOEPLITZ_WIDTH - WINDOW)),
    ], axis=-1).astype(F32).reshape(h, 1, TOEPLITZ_WIDTH)


def _fill_bias_tile(g_ref, bias_ref):
    qc = lax.broadcasted_iota(jnp.int32, (SEQ_BLOCK, WINDOW), 0) // CHUNK
    kc = lax.broadcasted_iota(jnp.int32, (SEQ_BLOCK, WINDOW), 1) // CHUNK
    visible = (kc >= qc) & (kc <= qc + LEFT_CHUNKS)
    for h in range(ATT_HEADS):
        row = jnp.broadcast_to(g_ref[h], (SEQ_BLOCK, TOEPLITZ_WIDTH))
        t = pltpu.roll(row, 0, 1, stride=1, stride_axis=0)[:, :WINDOW]
        bias_ref[h] = jnp.where(visible, t * LOG2E, NEG)


def _retention_tables():
    h = np.arange(RET_HEADS, dtype=np.float64)
    log_g = np.log1p(-np.exp2(-(5.0 + h)))
    pos = np.arange(SEQ_BLOCK, dtype=np.float64)
    dist = np.abs(pos[:, None] - pos[None, :])
    vis = (pos[None, :] // CHUNK) <= (pos[:, None] // CHUNK)
    decay = np.exp(log_g[:, None, None] * dist) * vis
    q_dec = np.exp(log_g[None, :] * (pos + 1.0)[:, None])
    k_dec = np.exp(log_g[None, :] * (SEQ_BLOCK - 1.0 - pos)[:, None])
    blk_dec = np.exp(log_g * SEQ_BLOCK)
    lanes = lambda t: np.repeat(t, HEAD_DIM, axis=1).astype(np.float32)
    return decay.astype(np.float32), lanes(q_dec), lanes(k_dec), [float(v) for v in blk_dec]


def _retention_kernel(q_ref, k_ref, v_ref, gate_ref, dec_ref, qdec_ref, kdec_ref, gn_ref,
                      o_ref, st_ref, *, blocks_per_seq, blk_dec):
    @pl.when(pl.program_id(0) % blocks_per_seq == 0)
    def _():
        st_ref[...] = jnp.zeros_like(st_ref)

    for h in range(RET_HEADS):
        sl = slice(h * HEAD_DIM, (h + 1) * HEAD_DIM)
        state = st_ref[h]
        for blk in range(RET_BLOCKS):
            rows = slice(blk * SEQ_BLOCK, (blk + 1) * SEQ_BLOCK)
            q = q_ref[rows, sl]
            k = k_ref[rows, sl]
            v = v_ref[rows, sl]
            p = (_dot_nt(q, k) * dec_ref[h]).astype(BF16)
            y = _dot(p, v) + _dot(q, state.astype(BF16)) * qdec_ref[:, sl]
            kd = (k.astype(F32) * kdec_ref[:, sl]).astype(BF16)
            state = blk_dec[h] * state + _dot_tn(kd, v)
            mu = jnp.mean(y, axis=-1, keepdims=True)
            yc = y - mu
            var = jnp.mean(yc * yc, axis=-1, keepdims=True)
            yn = yc * lax.rsqrt(var + GN_EPS) * gn_ref[:, sl]
            gate = gate_ref[rows, sl].astype(F32)
            swish = gate * (0.5 + 0.5 * jnp.tanh(0.5 * gate))
            o_ref[rows, sl] = (swish * yn).astype(BF16)
        st_ref[h] = state


def _retention(proj, gn, *, seq):
    m = proj.shape[0]
    width = RET_HEADS * HEAD_DIM
    decay, q_dec, k_dec, blk_dec = _retention_tables()
    step_rows = RET_BLOCKS * SEQ_BLOCK
    assert seq % step_rows == 0
    blocks_per_seq = seq // step_rows
    section = lambda s: pl.BlockSpec((step_rows, width), lambda i: (i, s))
    whole = lambda shape: pl.BlockSpec(shape, lambda i: (0,) * len(shape))
    return pl.pallas_call(
        functools.partial(_retention_kernel, blocks_per_seq=blocks_per_seq, blk_dec=blk_dec),
        out_shape=jax.ShapeDtypeStruct((m, width), BF16),
        grid=(m // step_rows,),
        in_specs=[section(0), section(1), section(2), section(3),
                  whole(decay.shape), whole(q_dec.shape), whole(k_dec.shape), whole((1, width))],
        out_specs=pl.BlockSpec((step_rows, width), lambda i: (i, 0)),
        scratch_shapes=[pltpu.VMEM((RET_HEADS, HEAD_DIM, HEAD_DIM), F32)],
        compiler_params=_params(1),
        name="retention",
    )(proj, proj, proj, proj, decay, q_dec, k_dec, gn)


def _chunk_attn_kernel(q_ref, kp0_ref, kp1_ref, k_ref, vp0_ref, vp1_ref, v_ref, g_ref, *rest,
                       steps_per_seq, n_cast):
    w_refs, o_ref, wb_refs = rest[:n_cast], rest[n_cast], rest[n_cast + 1:2 * n_cast + 1]
    bias_ref = rest[2 * n_cast + 1]
    pl.when(pl.program_id(0) == 0)(functools.partial(_fill_bias_tile, g_ref, bias_ref))
    first_step = pl.program_id(0) % steps_per_seq == 0
    add = lambda a, b: a + b
    lane_tiles = SEQ_BLOCK // HEAD_DIM
    rows = lambda i0, i1: slice(i0 * SEQ_BLOCK, (i1 + 1) * SEQ_BLOCK)

    def key_block(prev_refs, main_ref, b, sl):
        if b < LEFT_BLOCKS:
            return prev_refs[b][:, sl]
        return main_ref[rows(b - LEFT_BLOCKS, b - LEFT_BLOCKS), sl]

    def attend_heads(first_block, group, carry):
        for w_ref, wb_ref in zip(w_refs, wb_refs):
            n = w_ref.shape[0] // ATT_HEAD_GROUPS
            slab_rows = pl.ds(pl.multiple_of(group * n, n), n)
            wb_ref[slab_rows, :] = w_ref[slab_rows, :].astype(BF16)
        key_blocks = range(first_block, ATT_BLOCKS + LEFT_BLOCKS)
        users = {b: (max(0, b - LEFT_BLOCKS), min(ATT_BLOCKS - 1, b)) for b in key_blocks}
        heads_per_group = ATT_HEADS // ATT_HEAD_GROUPS
        for head_in_group in range(heads_per_group):
            h = group * heads_per_group + head_in_group
            sl = pl.ds(pl.multiple_of(h * HEAD_DIM, HEAD_DIM), HEAD_DIM)
            tiles = {i: [] for i in range(ATT_BLOCKS)}
            for b in key_blocks:
                i0, i1 = users[b]
                s = _dot_nt(q_ref[rows(i0, i1), sl], key_block((kp0_ref, kp1_ref), k_ref, b, sl))
                for i in range(i0, i1 + 1):
                    w = b - i
                    sb = (s[rows(i - i0, i - i0), :]
                          + bias_ref[h, :, w * SEQ_BLOCK:(w + 1) * SEQ_BLOCK])
                    tiles[i] += [sb[:, c * HEAD_DIM:(c + 1) * HEAD_DIM] for c in range(lane_tiles)]
            probs, denom = {}, {}
            for i in range(ATT_BLOCKS):
                mx = jnp.max(functools.reduce(jnp.maximum, tiles[i]), axis=-1, keepdims=True)
                p = [jnp.exp2(tile - mx) for tile in tiles[i]]
                denom[i] = jnp.sum(functools.reduce(add, p), axis=-1, keepdims=True)
                visible = [b for b in key_blocks if users[b][0] <= i <= users[b][1]]
                for n, b in enumerate(visible):
                    probs[i, b] = jnp.concatenate(
                        p[n * lane_tiles:(n + 1) * lane_tiles], axis=1).astype(BF16)
            out = {i: [] for i in range(ATT_BLOCKS)}
            for b in key_blocks:
                i0, i1 = users[b]
                lhs = jnp.concatenate([probs[i, b] for i in range(i0, i1 + 1)], axis=0)
                pv = _dot(lhs, key_block((vp0_ref, vp1_ref), v_ref, b, sl))
                for i in range(i0, i1 + 1):
                    out[i].append(pv[rows(i - i0, i - i0), :])
            for i in range(ATT_BLOCKS):
                o_ref[rows(i, i), sl] = (functools.reduce(add, out[i]) / denom[i]).astype(BF16)
        return carry

    def attend(first_block):
        lax.fori_loop(0, ATT_HEAD_GROUPS, functools.partial(attend_heads, first_block), 0)

    pl.when(first_step)(functools.partial(attend, LEFT_BLOCKS))
    pl.when(jnp.logical_not(first_step))(functools.partial(attend, 0))


def _chunk_attn(proj, bias_rows, weights, *, seq):
    m = proj.shape[0]
    width = ATT_HEADS * HEAD_DIM
    step_rows = ATT_BLOCKS * SEQ_BLOCK
    assert seq % step_rows == 0 and ATT_BLOCKS >= LEFT_BLOCKS
    steps_per_seq = seq // step_rows
    steps = m // step_rows

    def previous(section, b):
        def index(i):
            seq_start = (i - i % steps_per_seq) * ATT_BLOCKS
            return (jnp.maximum(i * ATT_BLOCKS - (LEFT_BLOCKS - b), seq_start), section)
        return pl.BlockSpec((SEQ_BLOCK, width), index)

    main = lambda section: pl.BlockSpec((step_rows, width), lambda i: (i, section))

    def slab(w):
        rows, cols = w.shape
        assert rows % (steps * ATT_HEAD_GROUPS * BF16_SUBLANES) == 0
        return pl.BlockSpec((rows // steps, cols), lambda i: (i, 0))

    att, *weights_bf16 = pl.pallas_call(
        functools.partial(_chunk_attn_kernel, steps_per_seq=steps_per_seq, n_cast=len(weights)),
        out_shape=[jax.ShapeDtypeStruct((m, width), BF16)]
                  + [jax.ShapeDtypeStruct(w.shape, BF16) for w in weights],
        grid=(steps,),
        in_specs=[main(4)]
                 + [previous(5, b) for b in range(LEFT_BLOCKS)] + [main(5)]
                 + [previous(6, b) for b in range(LEFT_BLOCKS)] + [main(6)]
                 + [pl.BlockSpec(bias_rows.shape, lambda i: (0, 0, 0))]
                 + [slab(w) for w in weights],
        out_specs=[pl.BlockSpec((step_rows, width), lambda i: (i, 0))] + [slab(w) for w in weights],
        scratch_shapes=[pltpu.VMEM((ATT_HEADS, SEQ_BLOCK, WINDOW), F32)],
        compiler_params=_params(1),
        name="chunk_attn",
    )(proj, proj, proj, proj, proj, proj, proj, bias_rows, *weights)
    return att, weights_bf16


OUT_PROJ_X_BUFFERS = 3


def _out_proj_kernel(ret_hbm, att_hbm, w_hbm, x_hbm, o_hbm, *, tm):
    m, d = x_hbm.shape
    kr = ret_hbm.shape[1]

    def tile(ret_ref, att_ref, wr_ref, wa_ref, x_ref, o_ref):
        o_ref[...] = x_ref[...] + (_dot(ret_ref[...], wr_ref[...]) + _dot(att_ref[...], wa_ref[...]))

    pltpu.emit_pipeline(
        tile,
        grid=(m // tm,),
        in_specs=[
            pl.BlockSpec((tm, kr), lambda i: (i, 0)),
            pl.BlockSpec((tm, kr), lambda i: (i, 0)),
            pl.BlockSpec((kr, d), lambda i: (0, 0)),
            pl.BlockSpec((kr, d), lambda i: (1, 0)),
            pl.BlockSpec((tm, d), lambda i: (i, 0), pipeline_mode=pl.Buffered(OUT_PROJ_X_BUFFERS)),
        ],
        out_specs=[pl.BlockSpec((tm, d), lambda i: (i, 0))],
    )(ret_hbm, att_hbm, w_hbm, w_hbm, x_hbm, o_hbm)


def _out_proj(ret, att, w_bf16, x2d, *, tm, tn):
    m, d = x2d.shape
    kr, ka = ret.shape[1], att.shape[1]
    assert kr == ka and kr + ka == w_bf16.shape[0] and m % tm == 0 and tn == d
    any_space = pl.BlockSpec(memory_space=pl.ANY)
    return pl.pallas_call(
        functools.partial(_out_proj_kernel, tm=tm),
        out_shape=jax.ShapeDtypeStruct((m, d), F32),
        in_specs=[any_space] * 4,
        out_specs=any_space,
        compiler_params=pltpu.CompilerParams(vmem_limit_bytes=V7X_VMEM_LIMIT_BYTES),
        name="out_proj",
    )(ret, att, w_bf16, x2d)


def _ffn_kernel(x_ref, g_ref, w1_ref, w2_ref, o_ref, h_ref, r_ref):
    f = pl.program_id(1)

    def chunk():
        a = _dot(h_ref[...], w1_ref[...])
        hidden = []
        for c in range(a.shape[1] // HEAD_DIM):
            act = jnp.maximum(a[:, c * HEAD_DIM:(c + 1) * HEAD_DIM] * r_ref[...], 0.0)
            hidden.append((act * act).astype(BF16))
        o_ref[...] += _dot(jnp.concatenate(hidden, axis=1), w2_ref[...])

    @pl.when(f == 0)
    def _():
        x = x_ref[...]
        ms = jnp.mean(x * x, axis=-1, keepdims=True)
        r_ref[...] = jnp.broadcast_to(lax.rsqrt(ms + EPS), r_ref.shape)
        h_ref[...] = (x * g_ref[...]).astype(BF16)
        o_ref[...] = x
        chunk()

    pl.when(f > 0)(chunk)


def _ffn(x2d, g2, w1_bf16, w2_bf16, *, tm, tf):
    m, d = x2d.shape
    d_ff = w1_bf16.shape[1]
    assert m % tm == 0 and d_ff % tf == 0
    return pl.pallas_call(
        _ffn_kernel,
        out_shape=jax.ShapeDtypeStruct((m, d), F32),
        grid=(m // tm, d_ff // tf),
        in_specs=[
            pl.BlockSpec((tm, d), lambda i, f: (i, 0)),
            pl.BlockSpec((1, d), lambda i, f: (0, 0)),
            pl.BlockSpec((d, tf), lambda i, f: (0, f)),
            pl.BlockSpec((tf, d), lambda i, f: (f, 0)),
        ],
        out_specs=pl.BlockSpec((tm, d), lambda i, f: (i, 0)),
        scratch_shapes=[pltpu.VMEM((tm, d), BF16), pltpu.VMEM((tm, HEAD_DIM), F32)],
        compiler_params=_params(2, V7X_VMEM_HIGH_LIMIT_BYTES),
        name="ffn",
    )(x2d, g2, w1_bf16, w2_bf16)


def _rotary_tables(seq):
    half = HEAD_DIM // 2
    inv_freq = ROPE_BASE ** (-np.arange(half, dtype=np.float64) / half)
    ang = np.arange(seq, dtype=np.float64)[:, None] * inv_freq[None, :]
    cos, sin = np.cos(ang), np.sin(ang)
    cos2 = np.concatenate([cos, cos], axis=1).astype(np.float32)
    sin2 = np.concatenate([-sin, sin], axis=1).astype(np.float32)
    return cos2, sin2


def kernel(x, norm1_g, w_in, ret_norm_g, q_norm_g, k_norm_g, rel_bias, w_out, norm2_g, w_ff1, w_ff2):
    b, s, d = x.shape
    depth = w_in.shape[0]
    assert s % SEQ_BLOCK == 0 and SEQ_BLOCK % CHUNK == 0
    assert RET_HEADS == HEADS_PER_SECTION and ATT_HEADS == HEADS_PER_SECTION
    cos2, sin2 = _rotary_tables(s)
    row = lambda v: v.astype(F32).reshape(1, -1)
    x2d = x.reshape(b * s, d)
    for layer in range(depth):
        proj, (w1, w2) = _in_proj(x2d, row(norm1_g[layer]), w_in[layer].astype(BF16), cos2, sin2,
                                  row(q_norm_g[layer]), row(k_norm_g[layer]),
                                  (w_ff1[layer], w_ff2[layer]), seq=s, tm=IN_PROJ_ROWS)
        ret = _retention(proj, row(ret_norm_g[layer]), seq=s)
        att, (wo,) = _chunk_attn(proj, _bias_rows(rel_bias[layer]), (w_out[layer],), seq=s)
        x2d = _out_proj(ret, att, wo, x2d, tm=OUT_PROJ_ROWS, tn=d)
        x2d = _ffn(x2d, row(norm2_g[layer]), w1, w2, tm=FFN_ROWS, tf=FFN_CHUNK)
    return x2d.reshape(b, s, d)
```
